```python
import jax, jax.numpy as jnp
from jax import lax
import numpy as np

D_MODEL = 2048
BATCH = 8
SEQ = 2048
DEPTH = 1

ATT_HEADS = 16
ATT_KV_HEADS = 2
HEAD_DIM = 64
ATT_WIDTH = ATT_HEADS * HEAD_DIM
KV_WIDTH = ATT_KV_HEADS * HEAD_DIM
GROUP = ATT_HEADS // ATT_KV_HEADS
WINDOW = 128
BLOCK = 128
ROT_DIM = HEAD_DIM // 4
ROPE_THETA = 500000.0
HG_HEADS = 8
HG_EXPAND = 128
HG_HEAD_V = 128
HG_F_WIDTH = HG_HEADS * HG_EXPAND
HG_V_WIDTH = HG_HEADS * HG_HEAD_V
CHUNK = 64
N_LB = DEPTH + 1
FFN_HIDDEN = ((8 * D_MODEL // 3 + 255) // 256) * 256
N_MOD = 6
IN_COLS = ATT_WIDTH + 2 * KV_WIDTH + 2 * HG_F_WIDTH + 2 * HG_V_WIDTH + 2 * D_MODEL
EPS = 1e-6

kernel_name = "hybrid_swa_sink_hgrn2_gated_block"


def rmsnorm(t, gain):
    t32 = t.astype(jnp.float32)
    y = t32 * lax.rsqrt(jnp.mean(t32 * t32, axis=-1, keepdims=True) + EPS)
    return (y * gain.astype(jnp.float32)).astype(t.dtype)


def rope_partial(t, cos, sin):
    half = ROT_DIM // 2
    t1 = t[..., :half].astype(jnp.float32)
    t2 = t[..., half:ROT_DIM].astype(jnp.float32)
    rot = jnp.concatenate([t1 * cos - t2 * sin, t2 * cos + t1 * sin], axis=-1)
    return jnp.concatenate([rot.astype(t.dtype), t[..., ROT_DIM:]], axis=-1)


def sliding_window_attention(q, k, v, sinks):
    B, S = q.shape[0], q.shape[1]
    nb = S // BLOCK
    qb = q.reshape(B, nb, BLOCK, ATT_KV_HEADS, GROUP, HEAD_DIM).astype(jnp.float32)
    kb = k.reshape(B, nb, BLOCK, ATT_KV_HEADS, HEAD_DIM).astype(jnp.float32)
    vb = v.reshape(B, nb, BLOCK, ATT_KV_HEADS, HEAD_DIM).astype(jnp.float32)
    prev = lambda t: jnp.concatenate([jnp.zeros_like(t[:, :1]), t[:, :-1]], axis=1)
    kk = jnp.concatenate([prev(kb), kb], axis=2)
    vv = jnp.concatenate([prev(vb), vb], axis=2)
    logits = jnp.einsum('bnqhgd,bnkhd->bnhgqk', qb, kk) * (HEAD_DIM ** -0.5)
    qi = jnp.arange(BLOCK)[:, None]
    kj = jnp.arange(2 * BLOCK)[None, :]
    rel = BLOCK + qi - kj
    band = (rel >= 0) & (rel < WINDOW)
    has_prev = (jnp.arange(nb) > 0)[:, None, None] | (kj >= BLOCK)[None]
    mask = band[None] & has_prev
    logits = jnp.where(mask[None, :, None, None], logits, -jnp.inf)
    sink = sinks.astype(jnp.float32).reshape(ATT_KV_HEADS, GROUP)[None, None, :, :, None, None]
    m = jnp.maximum(jnp.max(logits, axis=-1, keepdims=True), sink)
    p = jnp.exp(logits - m)
    denom = jnp.sum(p, axis=-1, keepdims=True) + jnp.exp(sink - m)
    out = jnp.einsum('bnhgqk,bnkhd->bnqhgd', p / denom, vv)
    return out.reshape(B, S, ATT_WIDTH).astype(q.dtype)


def hgrn2_recurrence(q, f_raw, i, lb):
    B, S = q.shape[0], q.shape[1]
    nc = S // CHUNK
    f = lb + (1.0 - lb) * jax.nn.sigmoid(f_raw.astype(jnp.float32))
    key = 1.0 - f
    logf = jnp.log(f)

    def chunks(t, d):
        return t.astype(jnp.float32).reshape(B, nc, CHUNK, HG_HEADS, d).transpose(1, 0, 3, 2, 4)

    qc = chunks(jax.nn.silu(q.astype(jnp.float32)), HG_EXPAND)
    kc = chunks(key, HG_EXPAND)
    vc = chunks(i, HG_HEAD_V)
    bc = jnp.cumsum(chunks(logf, HG_EXPAND), axis=3)
    causal = jnp.tril(jnp.ones((CHUNK, CHUNK), dtype=bool))[:, :, None]

    def step(state, xs):
        qt, kt, vt, bt = xs
        o_inter = jnp.einsum('bhck,bhkv->bhcv', qt * jnp.exp(bt), state)
        diff = bt[:, :, :, None, :] - bt[:, :, None, :, :]
        decay = jnp.where(causal, jnp.exp(jnp.where(causal, diff, 0.0)), 0.0)
        scores = jnp.einsum('bhtk,bhtsk,bhsk->bhts', qt, decay, kt)
        o = o_inter + jnp.einsum('bhts,bhsv->bhtv', scores, vt)
        b_last = bt[:, :, -1:, :]
        new_state = (jnp.exp(b_last[:, :, 0, :])[..., None] * state
                     + jnp.einsum('bhsk,bhsv->bhkv', kt * jnp.exp(b_last - bt), vt))
        return new_state, o

    state0 = jnp.zeros((B, HG_HEADS, HG_EXPAND, HG_HEAD_V), jnp.float32)
    _, o = lax.scan(step, state0, (qc, kc, vc, bc))
    return o.transpose(1, 0, 3, 2, 4).reshape(B, S, HG_HEADS, HG_HEAD_V)


def setup_inputs(seed: int = 0) -> dict:
    key = jax.random.key(seed)
    ks = jax.random.split(key, 20)
    f32 = jnp.float32
    nrm = lambda k, shape, scale: (jax.random.normal(k, shape, f32) * scale)
    gain = lambda k, shape: 1.0 + 0.05 * jax.random.normal(k, shape, f32)
    offsets = jax.random.randint(ks[2], (BATCH, 1), 0, 4096, dtype=jnp.int32)
    positions = (offsets + jnp.arange(SEQ, dtype=jnp.int32)[None, :]).astype(jnp.int32)
    return {
        "x": nrm(ks[0], (BATCH, SEQ, D_MODEL), 1.0),
        "c": nrm(ks[1], (BATCH, D_MODEL), 1.0),
        "positions": positions,
        "w_ada": nrm(ks[3], (DEPTH, D_MODEL, N_MOD * D_MODEL), 0.5 * D_MODEL ** -0.5),
        "b_ada": nrm(ks[4], (DEPTH, N_MOD * D_MODEL), 0.02),
        "g_pre_mix": gain(ks[5], (DEPTH, D_MODEL)),
        "g_post_mix": gain(ks[6], (DEPTH, D_MODEL)),
        "g_pre_ffn": gain(ks[7], (DEPTH, D_MODEL)),
        "g_post_ffn": gain(ks[8], (DEPTH, D_MODEL)),
        "w_in": nrm(ks[9], (DEPTH, D_MODEL, IN_COLS), D_MODEL ** -0.5),
        "attn_sinks": nrm(ks[10], (DEPTH, ATT_HEADS), 1.0),
        "w_attn_proj": nrm(ks[11], (DEPTH, ATT_WIDTH, D_MODEL), ATT_WIDTH ** -0.5),
        "hg_lower_bounds": nrm(ks[12], (N_LB, HG_F_WIDTH), 0.5),
        "hg_norm": gain(ks[13], (DEPTH, HG_HEAD_V)),
        "w_hgrn_proj": nrm(ks[14], (DEPTH, HG_V_WIDTH, D_MODEL), HG_V_WIDTH ** -0.5),
        "w_out": nrm(ks[15], (DEPTH, D_MODEL, D_MODEL), D_MODEL ** -0.5),
        "w_ffn_in": nrm(ks[16], (DEPTH, D_MODEL, 2 * FFN_HIDDEN), D_MODEL ** -0.5),
        "w_ffn_out": nrm(ks[17], (DEPTH, FFN_HIDDEN, D_MODEL), FFN_HIDDEN ** -0.5),
    }


def reference(x, c, positions, w_ada, b_ada, g_pre_mix, g_post_mix, g_pre_ffn, g_post_ffn,
              w_in, attn_sinks, w_attn_proj, hg_lower_bounds, hg_norm, w_hgrn_proj, w_out,
              w_ffn_in, w_ffn_out):
    B, S = x.shape[0], x.shape[1]
    inv_freq = ROPE_THETA ** (-jnp.arange(0, ROT_DIM, 2, dtype=jnp.float32) / ROT_DIM)
    ang = positions.astype(jnp.float32)[..., None] * inv_freq
    cos, sin = jnp.cos(ang)[:, :, None, :], jnp.sin(ang)[:, :, None, :]
    lb_table = jnp.cumsum(jax.nn.softmax(hg_lower_bounds.astype(jnp.float32), axis=0), axis=0)
    splits = [ATT_WIDTH, ATT_WIDTH + KV_WIDTH, ATT_WIDTH + 2 * KV_WIDTH,
              ATT_WIDTH + 2 * KV_WIDTH + HG_F_WIDTH,
              ATT_WIDTH + 2 * KV_WIDTH + 2 * HG_F_WIDTH,
              ATT_WIDTH + 2 * KV_WIDTH + 2 * HG_F_WIDTH + HG_V_WIDTH,
              ATT_WIDTH + 2 * KV_WIDTH + 2 * HG_F_WIDTH + 2 * HG_V_WIDTH,
              ATT_WIDTH + 2 * KV_WIDTH + 2 * HG_F_WIDTH + 2 * HG_V_WIDTH + D_MODEL]
    for l in range(DEPTH):
        mod = (c @ w_ada[l] + b_ada[l])[:, None, :]
        shift1, scale1, gate1, shift2, scale2, gate2 = jnp.split(mod, N_MOD, axis=-1)

        h = rmsnorm(x, g_pre_mix[l]) * (1.0 + scale1) + shift1
        proj = h @ w_in[l]
        q_a, k_a, v_a, q_h, f_h, i_h, g_h, gate_a, gate_h = jnp.split(proj, splits, axis=-1)
        qa = rope_partial(q_a.reshape(B, S, ATT_HEADS, HEAD_DIM), cos, sin)
        ka = rope_partial(k_a.reshape(B, S, ATT_KV_HEADS, HEAD_DIM), cos, sin)
        va = v_a.reshape(B, S, ATT_KV_HEADS, HEAD_DIM)
        y_a = sliding_window_attention(qa, ka, va, attn_sinks[l]) @ w_attn_proj[l]
        o_h = hgrn2_recurrence(q_h, f_h, i_h, lb_table[l])
        o_h = rmsnorm(o_h, hg_norm[l]).reshape(B, S, HG_V_WIDTH).astype(x.dtype)
        y_h = (o_h * jax.nn.sigmoid(g_h)) @ w_hgrn_proj[l]
        merged = jax.nn.sigmoid(gate_a) * y_a + jax.nn.sigmoid(gate_h) * y_h
        y = merged @ w_out[l]
        x = x + gate1 * rmsnorm(y, g_post_mix[l])

        h = rmsnorm(x, g_pre_ffn[l]) * (1.0 + scale2) + shift2
        gu = h @ w_ffn_in[l]
        g_ffn, u_ffn = jnp.split(gu, 2, axis=-1)
        y = (jax.nn.silu(g_ffn) * u_ffn) @ w_ffn_out[l]
        x = x + gate2 * rmsnorm(y, g_post_ffn[l])
    return x
```

```python
import functools

import numpy as np
import jax
import jax.numpy as jnp
from jax import lax
from jax.experimental import pallas as pl
from jax.experimental.pallas import tpu as pltpu

F32 = jnp.float32
BF16 = jnp.bfloat16

D_MODEL = 2048
ATT_HEADS = 16
ATT_KV_HEADS = 2
HEAD_DIM = 64
ATT_WIDTH = ATT_HEADS * HEAD_DIM
KV_WIDTH = ATT_KV_HEADS * HEAD_DIM
WINDOW = 128
ROT_DIM = HEAD_DIM // 4
ROPE_THETA = 500000.0
HG_HEADS = 8
HG_DIM = 128
HG_WIDTH = HG_HEADS * HG_DIM
N_MOD = 6
EPS = 1e-6

LANES = 128
V7X_VMEM_BYTES = 64 * 1024 * 1024

CHUNK = 64
DIAG = 8
LEVELS = (32, 16, 8)


def _vmem_limit(estimate_bytes):
    return int(min(estimate_bytes * 3 // 2, V7X_VMEM_BYTES - 8 * 1024 * 1024))


def _dot(a, b):
    return jnp.dot(a, b, preferred_element_type=F32)


def _dot_nt(a, b):
    return lax.dot_general(a, b, (((1,), (1,)), ((), ())), preferred_element_type=F32)


def _dot_tn(a, b):
    return lax.dot_general(a, b, (((0,), (0,)), ((), ())), preferred_element_type=F32)


def _rms(t):
    return t * lax.rsqrt(jnp.mean(t * t, axis=-1, keepdims=True) + EPS)


def _adaln_kernel(c_ref, w_ref, b_ref, o_ref):
    o_ref[...] = jnp.dot(c_ref[...], w_ref[...], preferred_element_type=F32,
                         precision=lax.Precision.HIGHEST) + b_ref[...]


def _adaln(c, w, b):
    bsz, d = c.shape
    n = w.shape[1]
    tn = 1024
    est = 2 * (d * tn * 4) + 2 * bsz * d * 4 + 4 * bsz * tn * 4
    return pl.pallas_call(
        _adaln_kernel,
        grid=(n // tn,),
        in_specs=[pl.BlockSpec((bsz, d), lambda j: (0, 0)),
                  pl.BlockSpec((d, tn), lambda j: (0, j)),
                  pl.BlockSpec((1, tn), lambda j: (0, j))],
        out_specs=pl.BlockSpec((bsz, tn), lambda j: (0, j)),
        out_shape=jax.ShapeDtypeStruct((bsz, n), F32),
        compiler_params=pltpu.CompilerParams(
            dimension_semantics=("arbitrary",), vmem_limit_bytes=_vmem_limit(est)),
        name="adaln_mod",
    )(c, w, b.reshape(1, n))


def _in_proj_kernel(x_ref, mod_ref, g_ref, w_ref, wkv_ref, o_ref, okv_ref, h_ref):
    @pl.when(pl.program_id(1) == 0)
    def _():
        y = _rms(x_ref[...]) * g_ref[...]
        h = y * (1.0 + mod_ref[1:2, :]) + mod_ref[0:1, :]
        h_ref[...] = h.astype(BF16)
        okv_ref[...] = _dot(h_ref[...], wkv_ref[...]).astype(BF16)

    o_ref[...] = _dot(h_ref[...], w_ref[...]).astype(BF16)


def _in_proj(x2, mod3, gain, w_main, w_kv, seq):
    n, d = x2.shape
    cols = w_main.shape[1]
    tm, tn = 1024, 512
    per_batch = seq // tm
    est = (2 * tm * d * 4 + tm * d * 2 + 2 * d * tn * 2 + 2 * tm * tn * 2
           + 2 * d * KV_WIDTH * 2 * 2 + 2 * tm * 2 * KV_WIDTH * 2 + tm * tn * 4 + tm * d * 4)
    return pl.pallas_call(
        _in_proj_kernel,
        grid=(n // tm, cols // tn),
        in_specs=[pl.BlockSpec((tm, d), lambda i, j: (i, 0)),
                  pl.BlockSpec((None, N_MOD, d), lambda i, j: (i // per_batch, 0, 0)),
                  pl.BlockSpec((1, d), lambda i, j: (0, 0)),
                  pl.BlockSpec((d, tn), lambda i, j: (0, j)),
                  pl.BlockSpec((d, 2 * KV_WIDTH), lambda i, j: (0, 0))],
        out_specs=[pl.BlockSpec((tm, tn), lambda i, j: (i, j)),
                   pl.BlockSpec((tm, 2 * KV_WIDTH), lambda i, j: (i, 0))],
        out_shape=[jax.ShapeDtypeStruct((n, cols), BF16),
                   jax.ShapeDtypeStruct((n, 2 * KV_WIDTH), BF16)],
        scratch_shapes=[pltpu.VMEM((tm, d), BF16)],
        compiler_params=pltpu.CompilerParams(
            dimension_semantics=("parallel", "arbitrary"), vmem_limit_bytes=_vmem_limit(est)),
        name="in_proj",
    )(x2, mod3, gain, w_main, w_kv)


def _row_to_col(row):
    n = row.shape[1]
    r = lax.broadcasted_iota(jnp.int32, (n, n), 0)
    c = lax.broadcasted_iota(jnp.int32, (n, n), 1)
    return jnp.sum(jnp.where(r == c, jnp.broadcast_to(row, (n, n)), 0.0), axis=1, keepdims=True)


def _rope_tables(pos_row, freq_row):
    ang = _row_to_col(pos_row) * freq_row
    cos, sin = jnp.cos(ang), jnp.sin(ang)
    dim = lax.broadcasted_iota(jnp.int32, ang.shape, 1) % HEAD_DIM
    half = ROT_DIM // 2
    t_cos = jnp.where(dim < ROT_DIM, cos, 1.0)
    t_lo = jnp.where(dim < half, -sin, 0.0)
    t_hi = jnp.where((dim >= half) & (dim < ROT_DIM), sin, 0.0)
    return t_cos, t_lo, t_hi


def _rope(x, tabs, scale=1.0):
    t_cos, t_lo, t_hi = tabs
    half = ROT_DIM // 2
    up = pltpu.roll(x, LANES - half, axis=1)
    dn = pltpu.roll(x, half, axis=1)
    y = x * t_cos + up * t_lo + dn * t_hi
    return y if scale == 1.0 else y * scale


def _attn_kernel(sink_ref, freq_ref, posc_ref, posp_ref, q_ref, kvc_ref, kvp_ref, o_ref):
    nblk = pl.program_id(1)
    blk = WINDOW
    tab_c = _rope_tables(posc_ref[...], freq_ref[...])
    tab_p = _rope_tables(posp_ref[...], freq_ref[...])

    k_c = _rope(kvc_ref[:, 0:KV_WIDTH].astype(F32), tab_c)
    k_p = _rope(kvp_ref[:, 0:KV_WIDTH].astype(F32), tab_p)
    kk = jnp.concatenate([k_p, k_c], axis=0)
    vv = jnp.concatenate([kvp_ref[:, KV_WIDTH:], kvc_ref[:, KV_WIDTH:]], axis=0).astype(F32)
    kk_sw = pltpu.roll(kk, HEAD_DIM, axis=1)
    vv_sw = pltpu.roll(vv, HEAD_DIM, axis=1)
    lane = lax.broadcasted_iota(jnp.int32, kk.shape, 1)
    lo = lane < HEAD_DIM

    def pair_operand(x, x_sw, kvh):
        a, b = (x, x_sw) if kvh == 0 else (x_sw, x)
        return jnp.concatenate([jnp.where(lo, a, 0.0), jnp.where(lo, 0.0, b)], axis=0).astype(BF16)

    qi = lax.broadcasted_iota(jnp.int32, (blk, 2 * blk), 0)
    kj = lax.broadcasted_iota(jnp.int32, (blk, 2 * blk), 1)
    mask = (kj > qi) & (kj <= qi + blk) & ((nblk > 0) | (kj >= blk))
    lane_o = lax.broadcasted_iota(jnp.int32, (blk, LANES), 1) < HEAD_DIM

    pairs_per_kv = ATT_HEADS // ATT_KV_HEADS // 2
    for kvh in range(ATT_KV_HEADS):
        kb = pair_operand(kk, kk_sw, kvh)
        vb = pair_operand(vv, vv_sw, kvh)
        for pp in range(pairs_per_kv):
            p = kvh * pairs_per_kv + pp
            cs = slice(p * LANES, (p + 1) * LANES)
            q = _rope(q_ref[:, cs].astype(F32), tab_c, HEAD_DIM ** -0.5).astype(BF16)
            s = _dot_nt(q, kb)
            probs, rden = [], []
            for e in range(2):
                sink = sink_ref[2 * p + e]
                se = jnp.where(mask, s[:, e * 2 * blk:(e + 1) * 2 * blk], -jnp.inf)
                m = jnp.maximum(jnp.max(se, axis=-1, keepdims=True), sink)
                pe = jnp.exp(se - m)
                den = jnp.sum(pe, axis=-1, keepdims=True) + jnp.exp(sink - m)
                probs.append(pe.astype(BF16))
                rden.append(1.0 / den)
            o = _dot(jnp.concatenate(probs, axis=1), vb)
            o_ref[:, cs] = (o * jnp.where(lane_o, rden[0], rden[1])).astype(BF16)


def _attention(sinks, freq_row, pos3, proj, kv, bsz, seq, q_col_block):
    n = proj.shape[0]
    blk = WINDOW
    nb = seq // blk
    row = lambda b, i: b * nb + i
    prow = lambda b, i: b * nb + jnp.maximum(i - 1, 0)
    est = 2 * (blk * ATT_WIDTH * 2 * 2 + 2 * blk * 2 * KV_WIDTH * 2) + 64 * blk * 4 * blk * 4
    return pl.pallas_call(
        _attn_kernel,
        grid=(bsz, nb),
        in_specs=[pl.BlockSpec(memory_space=pltpu.SMEM),
                  pl.BlockSpec((1, LANES), lambda b, i: (0, 0)),
                  pl.BlockSpec((None, 1, blk), lambda b, i: (row(b, i), 0, 0)),
                  pl.BlockSpec((None, 1, blk), lambda b, i: (prow(b, i), 0, 0)),
                  pl.BlockSpec((blk, ATT_WIDTH), lambda b, i: (row(b, i), q_col_block)),
                  pl.BlockSpec((blk, 2 * KV_WIDTH), lambda b, i: (row(b, i), 0)),
                  pl.BlockSpec((blk, 2 * KV_WIDTH), lambda b, i: (prow(b, i), 0))],
        out_specs=pl.BlockSpec((blk, ATT_WIDTH), lambda b, i: (row(b, i), 0)),
        out_shape=jax.ShapeDtypeStruct((n, ATT_WIDTH), BF16),
        compiler_params=pltpu.CompilerParams(
            dimension_semantics=("parallel", "arbitrary"), vmem_limit_bytes=_vmem_limit(est)),
        name="swa_attn",
    )(sinks, freq_row, pos3, pos3, proj, kv, kv)


def _hgrn_cumsum_matrix():
    t = np.arange(CHUNK)[:, None]
    j = np.arange(CHUNK)[None, :]
    tri = (j <= t).astype(np.float32)
    blocks = [tri]
    for lv in LEVELS:
        ref = (t // (2 * lv)) * 2 * lv + lv - 1
        blocks.append(tri - (j <= ref).astype(np.float32))
    blocks.append(1.0 - tri)
    m = np.concatenate(blocks, axis=0)
    return jnp.asarray(np.concatenate([m, m, m], axis=1), dtype=BF16)


def _split3(x):
    hi = x.astype(BF16)
    r1 = x - hi.astype(F32)
    mid = r1.astype(BF16)
    lo = (r1 - mid.astype(F32)).astype(BF16)
    return hi, mid, lo


def _hgrn_kernel(cm_ref, lbp_ref, gn_ref, q_ref, f_ref, i_ref, g_ref, o_ref, st_ref, *, tokens):
    @pl.when(pl.program_id(1) == 0)
    def _():
        st_ref[...] = jnp.zeros_like(st_ref)

    lbp = lbp_ref[...]
    e = jnp.exp(lbp - jnp.max(lbp, axis=0, keepdims=True))
    lb = e[0:1, :] / jnp.sum(e, axis=0, keepdims=True)

    row = lax.broadcasted_iota(jnp.int32, (CHUNK, CHUNK), 0)
    col = lax.broadcasted_iota(jnp.int32, (CHUNK, CHUNK), 1)
    trow = lax.broadcasted_iota(jnp.int32, (CHUNK, HG_WIDTH), 0)

    for c in range(tokens // CHUNK):
        rs = slice(c * CHUNK, (c + 1) * CHUNK)
        qx = q_ref[rs, :].astype(F32)
        qs = qx * jax.nn.sigmoid(qx)
        f = lb + (1.0 - lb) * jax.nn.sigmoid(f_ref[rs, :].astype(F32))
        key = 1.0 - f
        logf = jnp.log(f)
        val = i_ref[rs, :]

        sums = _dot(cm_ref[...], jnp.concatenate(_split3(logf), axis=0))
        b = sums[0:CHUNK]
        q_state = (qs * jnp.exp(b)).astype(BF16)
        k_state = (key * jnp.exp(sums[4 * CHUNK:5 * CHUNK])).astype(BF16)
        decay = jnp.exp(b[CHUNK - 1:CHUNK, :])

        q_lv, k_lv = [], []
        for n, lv in enumerate(LEVELS):
            w = jnp.exp(-jnp.abs(sums[(n + 1) * CHUNK:(n + 2) * CHUNK]))
            upper = (trow // lv) % 2 == 1
            q_lv.append(jnp.where(upper, qs * w, 0.0).astype(BF16))
            k_lv.append(jnp.where(upper, 0.0, key * w).astype(BF16))

        diag = [qs * key]
        for d in range(1, DIAG):
            w = jnp.exp(jnp.minimum(b - pltpu.roll(b, d, axis=0), 0.0))
            diag.append(qs * w * pltpu.roll(key, d, axis=0))

        for h in range(HG_HEADS):
            hs = slice(h * HG_DIM, (h + 1) * HG_DIM)
            g_mat = None
            for n, lv in enumerate(LEVELS):
                part = _dot_nt(q_lv[n][:, hs], k_lv[n][:, hs])
                if 2 * lv < CHUNK:
                    part = jnp.where(row // (2 * lv) == col // (2 * lv), part, 0.0)
                g_mat = part if g_mat is None else g_mat + part
            for d in range(DIAG):
                r = jnp.sum(diag[d][:, hs], axis=-1, keepdims=True)
                g_mat = g_mat + jnp.where((col == row - d) & (row % DIAG >= d), r, 0.0)

            v_h = val[:, hs]
            st = st_ref[h]
            o = _dot_nt(q_state[:, hs], st.astype(BF16)) + _dot(g_mat.astype(BF16), v_h)
            st_ref[h] = st * decay[:, hs] + _dot_tn(v_h, k_state[:, hs])

            y = _rms(o) * gn_ref[...]
            gate = jax.nn.sigmoid(g_ref[rs, hs].astype(F32))
            o_ref[rs, hs] = (y * gate).astype(BF16)


def _hgrn(cmat, lb_params, gnorm, proj, bsz, seq, col_blocks):
    n = proj.shape[0]
    tokens = 2 * CHUNK
    nt = seq // tokens
    row = lambda b, i: b * nt + i
    spec = lambda cb: pl.BlockSpec((tokens, HG_WIDTH), lambda b, i: (row(b, i), cb))
    est = 2 * 5 * tokens * HG_WIDTH * 2 + HG_HEADS * HG_DIM * HG_DIM * 4 + 40 * CHUNK * HG_WIDTH * 4
    return pl.pallas_call(
        functools.partial(_hgrn_kernel, tokens=tokens),
        grid=(bsz, nt),
        in_specs=[pl.BlockSpec(cmat.shape, lambda b, i: (0, 0)),
                  pl.BlockSpec(lb_params.shape, lambda b, i: (0, 0)),
                  pl.BlockSpec((1, HG_DIM), lambda b, i: (0, 0)),
                  spec(col_blocks[0]), spec(col_blocks[1]), spec(col_blocks[2]), spec(col_blocks[3])],
        out_specs=pl.BlockSpec((tokens, HG_WIDTH), lambda b, i: (row(b, i), 0)),
        out_shape=jax.ShapeDtypeStruct((n, HG_WIDTH), BF16),
        scratch_shapes=[pltpu.VMEM((HG_HEADS, HG_DIM, HG_DIM), F32)],
        compiler_params=pltpu.CompilerParams(
            dimension_semantics=("parallel", "arbitrary"), vmem_limit_bytes=_vmem_limit(est)),
        name="hgrn2",
    )(cmat, lb_params, gnorm, proj, proj, proj, proj)


def _mix_kernel(x_ref, mod_ref, g_ref, a_ref, hg_ref, ga_ref, gh_ref, wa_ref, wh_ref, wo_ref,
                o_ref, m_ref, *, tc):
    d = x_ref.shape[1]
    for c in range(d // tc):
        cs = slice(c * tc, (c + 1) * tc)
        ya = _dot(a_ref[...], wa_ref[:, cs])
        yh = _dot(hg_ref[...], wh_ref[:, cs])
        merged = (jax.nn.sigmoid(ga_ref[:, cs].astype(F32)) * ya
                  + jax.nn.sigmoid(gh_ref[:, cs].astype(F32)) * yh)
        m_ref[:, cs] = merged.astype(BF16)
    y = _dot(m_ref[...], wo_ref[...])
    o_ref[...] = x_ref[...] + mod_ref[2:3, :] * (_rms(y) * g_ref[...])


def _mix_out(x2, mod3, gain, attn, hg, proj, wa, wh, wo, seq, ga_block, gh_block):
    n, d = x2.shape
    tm, tc = 256, 512
    per_batch = seq // tm
    const = lambda shape: pl.BlockSpec(shape, lambda i: (0, 0), pipeline_mode=pl.Buffered(1))
    est = ((wa.size + wh.size + wo.size) * 2 + 2 * 2 * tm * d * 4 + 2 * 2 * tm * d * 2
           + 2 * 2 * tm * ATT_WIDTH * 2 + tm * d * 2 + 3 * tm * d * 4)
    return pl.pallas_call(
        functools.partial(_mix_kernel, tc=tc),
        grid=(n // tm,),
        in_specs=[pl.BlockSpec((tm, d), lambda i: (i, 0)),
                  pl.BlockSpec((None, N_MOD, d), lambda i: (i // per_batch, 0, 0)),
                  pl.BlockSpec((1, d), lambda i: (0, 0)),
                  pl.BlockSpec((tm, ATT_WIDTH), lambda i: (i, 0)),
                  pl.BlockSpec((tm, HG_WIDTH), lambda i: (i, 0)),
                  pl.BlockSpec((tm, d), lambda i: (i, ga_block)),
                  pl.BlockSpec((tm, d), lambda i: (i, gh_block)),
                  const(wa.shape), const(wh.shape), const(wo.shape)],
        out_specs=pl.BlockSpec((tm, d), lambda i: (i, 0)),
        out_shape=jax.ShapeDtypeStruct((n, d), F32),
        scratch_shapes=[pltpu.VMEM((tm, d), BF16)],
        compiler_params=pltpu.CompilerParams(
            dimension_semantics=("parallel",), vmem_limit_bytes=_vmem_limit(est)),
        name="mix_out",
    )(x2, mod3, gain, attn, hg, proj, proj, wa, wh, wo)


def _ffn_kernel(x_ref, mod_ref, gpre_ref, gpost_ref, wg_ref, wu_ref, wd_ref, o_ref, h_ref, acc_ref):
    j = pl.program_id(1)

    @pl.when(j == 0)
    def _():
        y = _rms(x_ref[...]) * gpre_ref[...]
        h_ref[...] = (y * (1.0 + mod_ref[4:5, :]) + mod_ref[3:4, :]).astype(BF16)

    h = h_ref[...]
    g = _dot(h, wg_ref[...])
    u = _dot(h, wu_ref[...])
    part = _dot((g * jax.nn.sigmoid(g) * u).astype(BF16), wd_ref[...])

    @pl.when(j == 0)
    def _():
        acc_ref[...] = part

    @pl.when(j > 0)
    def _():
        acc_ref[...] += part

    @pl.when(j == pl.num_programs(1) - 1)
    def _():
        o_ref[...] = x_ref[...] + mod_ref[5:6, :] * (_rms(acc_ref[...]) * gpost_ref[...])


def _ffn(x1, mod3, gpre, gpost, w_in, w_down, seq):
    n, d = x1.shape
    hidden = w_down.shape[0]
    tm, th = 512, 512
    nh = hidden // th
    per_batch = seq // tm
    est = (2 * 2 * tm * d * 4 + tm * d * 2 + tm * d * 4 + 2 * 3 * d * th * 2 + 4 * tm * th * 4
           + tm * d * 4)
    return pl.pallas_call(
        _ffn_kernel,
        grid=(n // tm, nh),
        in_specs=[pl.BlockSpec((tm, d), lambda i, j: (i, 0)),
                  pl.BlockSpec((None, N_MOD, d), lambda i, j: (i // per_batch, 0, 0)),
                  pl.BlockSpec((1, d), lambda i, j: (0, 0)),
                  pl.BlockSpec((1, d), lambda i, j: (0, 0)),
                  pl.BlockSpec((d, th), lambda i, j: (0, j)),
                  pl.BlockSpec((d, th), lambda i, j: (0, nh + j)),
                  pl.BlockSpec((th, d), lambda i, j: (j, 0))],
        out_specs=pl.BlockSpec((tm, d), lambda i, j: (i, 0)),
        out_shape=jax.ShapeDtypeStruct((n, d), F32),
        scratch_shapes=[pltpu.VMEM((tm, d), BF16), pltpu.VMEM((tm, d), F32)],
        compiler_params=pltpu.CompilerParams(
            dimension_semantics=("parallel", "arbitrary"), vmem_limit_bytes=_vmem_limit(est)),
        name="ffn",
    )(x1, mod3, gpre, gpost, w_in, w_in, w_down)


def kernel(x, c, positions, w_ada, b_ada, g_pre_mix, g_post_mix, g_pre_ffn, g_post_ffn, w_in,
           attn_sinks, w_attn_proj, hg_lower_bounds, hg_norm, w_hgrn_proj, w_out, w_ffn_in,
           w_ffn_out):
    bsz, seq, d = x.shape
    n = bsz * seq
    assert d == D_MODEL and w_ada.shape[0] == 1 and seq % (2 * CHUNK) == 0 and seq % 1024 == 0

    o_q, o_k = 0, ATT_WIDTH
    o_qh = o_k + 2 * KV_WIDTH
    o_ga = o_qh + 4 * HG_WIDTH
    o_gh = o_ga + D_MODEL
    wi = w_in[0]
    w_main = jnp.concatenate(
        [wi[:, o_ga:o_gh + D_MODEL], wi[:, o_q:o_k], wi[:, o_qh:o_ga]], axis=1).astype(BF16)
    w_kv = wi[:, o_k:o_qh].astype(BF16)
    ga_block, gh_block = 0, 1
    q_block = 2 * D_MODEL // ATT_WIDTH
    hg_blocks = tuple((2 * D_MODEL + ATT_WIDTH) // HG_WIDTH + k for k in range(4))

    x2 = x.reshape(n, d)
    mod3 = _adaln(c, w_ada[0], b_ada[0]).reshape(bsz, N_MOD, d)

    proj, kv = _in_proj(x2, mod3, g_pre_mix, w_main, w_kv, seq)

    dim = np.arange(LANES) % HEAD_DIM
    inv_freq = ROPE_THETA ** (-jnp.arange(0, ROT_DIM, 2, dtype=F32) / ROT_DIM)
    freq_row = jnp.where(dim < ROT_DIM, inv_freq[dim % (ROT_DIM // 2)], 0.0).reshape(1, LANES)
    pos3 = positions.astype(F32).reshape(n // WINDOW, 1, WINDOW)
    attn = _attention(attn_sinks[0], freq_row, pos3, proj, kv, bsz, seq, q_block)

    hg = _hgrn(_hgrn_cumsum_matrix(), hg_lower_bounds, hg_norm, proj, bsz, seq, hg_blocks)

    x1 = _mix_out(x2, mod3, g_post_mix, attn, hg, proj, w_attn_proj[0].astype(BF16),
                  w_hgrn_proj[0].astype(BF16), w_out[0].astype(BF16), seq, ga_block, gh_block)

    out = _ffn(x1, mod3, g_pre_ffn, g_post_ffn, w_ffn_in[0].astype(BF16),
               w_ffn_out[0].astype(BF16), seq)
    return out.reshape(bsz, seq, d)
```

```python
import functools
import math

import numpy as np
import jax
import jax.numpy as jnp
from jax import lax
from jax.experimental import pallas as pl
from jax.experimental.pallas import tpu as pltpu

F32 = jnp.float32
BF16 = jnp.bfloat16

D_MODEL = 2048
ATT_HEADS = 16
ATT_KV_HEADS = 2
HEAD_DIM = 64
ATT_WIDTH = ATT_HEADS * HEAD_DIM
KV_WIDTH = ATT_KV_HEADS * HEAD_DIM
WINDOW = 128
ROT_DIM = HEAD_DIM // 4
ROT_HALF = ROT_DIM // 2
ROPE_THETA = 500000.0
HG_HEADS = 8
HG_DIM = 128
HG_WIDTH = HG_HEADS * HG_DIM
N_MOD = 6
EPS = 1e-6
LOG2E = math.log2(math.e)
MASK_BIAS = -1e30

LANES = 128
SUBLANES = 8
V7X_VMEM_BYTES = 64 * 1024 * 1024

CHUNK = 64
DIAG = SUBLANES
LEVELS = (32, 16, 8)


def _vmem_limit(estimate_bytes):
    return int(min(estimate_bytes * 3 // 2, V7X_VMEM_BYTES - 8 * 1024 * 1024))


def _dot(a, b):
    return jnp.dot(a, b, preferred_element_type=F32)


def _dot_nt(a, b):
    return lax.dot_general(a, b, (((1,), (1,)), ((), ())), preferred_element_type=F32)


def _dot_tn(a, b):
    return lax.dot_general(a, b, (((0,), (0,)), ((), ())), preferred_element_type=F32)


def _rms(t):
    return t * lax.rsqrt(jnp.mean(t * t, axis=-1, keepdims=True) + EPS)


def _sigmoid(t):
    return 0.5 + 0.5 * jnp.tanh(0.5 * t)


def _split_bf16(x, parts):
    out = []
    for _ in range(parts - 1):
        p = x.astype(BF16).astype(F32)
        out.append(p)
        x = x - p
    out.append(x.astype(BF16).astype(F32))
    return out


def _adaln_kernel(c_ref, w_ref, b_ref, o_ref):
    o_ref[...] = jnp.dot(c_ref[...], w_ref[...], preferred_element_type=F32,
                         precision=lax.Precision.HIGHEST) + b_ref[...]


def _adaln(c, w, b):
    bsz, d = c.shape
    n = w.shape[1]
    tn = 1024
    est = 2 * (d * tn * 4) + 2 * bsz * d * 4 + 4 * bsz * tn * 4
    return pl.pallas_call(
        _adaln_kernel,
        grid=(n // tn,),
        in_specs=[pl.BlockSpec((bsz, d), lambda j: (0, 0)),
                  pl.BlockSpec((d, tn), lambda j: (0, j)),
                  pl.BlockSpec((1, tn), lambda j: (0, j))],
        out_specs=pl.BlockSpec((bsz, tn), lambda j: (0, j)),
        out_shape=jax.ShapeDtypeStruct((bsz, n), F32),
        compiler_params=pltpu.CompilerParams(
            dimension_semantics=("arbitrary",), vmem_limit_bytes=_vmem_limit(est)),
        name="adaln_mod",
    )(c, w, b.reshape(1, n))


def _in_proj_kernel(x_ref, mod_ref, g_ref, w_ref, wkv_ref, o_ref, okv_ref, h_ref):
    @pl.when(pl.program_id(1) == 0)
    def _():
        y = _rms(x_ref[...]) * g_ref[...]
        h = y * (1.0 + mod_ref[1:2, :]) + mod_ref[0:1, :]
        h_ref[...] = h.astype(BF16)
        okv_ref[...] = _dot(h_ref[...], wkv_ref[...]).astype(BF16)

    o_ref[...] = _dot(h_ref[...], w_ref[...]).astype(BF16)


def _in_proj(x2, mod3, gain, w_main, w_kv, seq):
    n, d = x2.shape
    cols = w_main.shape[1]
    tm, tn = 1024, 1024
    per_batch = seq // tm
    est = (2 * tm * d * 4 + tm * d * 2 + 2 * d * tn * 2 + 2 * tm * tn * 2
           + 2 * d * KV_WIDTH * 2 * 2 + 2 * tm * 2 * KV_WIDTH * 2 + tm * tn * 4 + tm * d * 4)
    return pl.pallas_call(
        _in_proj_kernel,
        grid=(n // tm, cols // tn),
        in_specs=[pl.BlockSpec((tm, d), lambda i, j: (i, 0)),
                  pl.BlockSpec((None, N_MOD, d), lambda i, j: (i // per_batch, 0, 0)),
                  pl.BlockSpec((1, d), lambda i, j: (0, 0)),
                  pl.BlockSpec((d, tn), lambda i, j: (0, j)),
                  pl.BlockSpec((d, 2 * KV_WIDTH), lambda i, j: (0, 0))],
        out_specs=[pl.BlockSpec((tm, tn), lambda i, j: (i, j)),
                   pl.BlockSpec((tm, 2 * KV_WIDTH), lambda i, j: (i, 0))],
        out_shape=[jax.ShapeDtypeStruct((n, cols), BF16),
                   jax.ShapeDtypeStruct((n, 2 * KV_WIDTH), BF16)],
        scratch_shapes=[pltpu.VMEM((tm, d), BF16)],
        compiler_params=pltpu.CompilerParams(
            dimension_semantics=("parallel", "arbitrary"), vmem_limit_bytes=_vmem_limit(est)),
        name="in_proj",
    )(x2, mod3, gain, w_main, w_kv)


def _attn_constants():
    lane = np.arange(LANES)
    dim = lane % HEAD_DIM
    expand = np.zeros((2, 3, ROT_HALF, 3, LANES), np.float32)
    for j in range(ROT_HALF):
        expand[0, :, j, 0, (dim < ROT_DIM) & (dim % ROT_HALF == j)] = 1.0
        expand[1, :, j, 1, (dim < ROT_HALF) & (dim == j)] = -1.0
        expand[1, :, j, 2, (dim >= ROT_HALF) & (dim < ROT_DIM) & (dim - ROT_HALF == j)] = 1.0
    expand = expand.reshape(6 * ROT_HALF, 3 * LANES)
    shift = np.zeros((LANES, 2 * LANES), np.float32)
    for l in range(LANES):
        if l + ROT_HALF < LANES:
            shift[l + ROT_HALF, l] = 1.0
        if l - ROT_HALF >= 0:
            shift[l - ROT_HALF, LANES + l] = 1.0
    return jnp.asarray(expand, BF16), jnp.asarray(shift, BF16)


def _attn_kernel(sink_ref, freq_ref, expand_ref, shift_ref, pos_ref, q_ref, kv_ref, o_ref,
                 kb_ref, vb_ref):
    nblk = pl.program_id(1)
    parity = nblk % 2
    blk = WINDOW
    lane = lax.broadcasted_iota(jnp.int32, (blk, LANES), 1)
    lo = lane < HEAD_DIM

    @pl.when(nblk == 0)
    def _():
        kb_ref[...] = jnp.zeros_like(kb_ref)
        vb_ref[...] = jnp.zeros_like(vb_ref)

    ang = freq_ref[...] * pos_ref[...]
    parts = _split_bf16(jnp.cos(ang), 3) + _split_bf16(jnp.sin(ang), 3)
    tabs = _dot_tn(jnp.concatenate(parts, axis=0).astype(BF16), expand_ref[...])
    dim = lane % HEAD_DIM
    k_tabs = (tabs[:, 0:LANES] + jnp.where(dim < ROT_DIM, 0.0, 1.0),
              tabs[:, LANES:2 * LANES], tabs[:, 2 * LANES:3 * LANES])
    q_tabs = tuple(t * (HEAD_DIM ** -0.5 * LOG2E) for t in k_tabs)

    def rope(x_bf16, tables):
        t_cos, t_lo, t_hi = tables
        ud = _dot(x_bf16, shift_ref[...])
        return x_bf16.astype(F32) * t_cos + ud[:, 0:LANES] * t_lo + ud[:, LANES:] * t_hi

    k = rope(kv_ref[:, 0:KV_WIDTH], k_tabs)
    v = kv_ref[:, KV_WIDTH:].astype(F32)
    k_sw = pltpu.roll(k, HEAD_DIM, axis=1)
    v_sw = pltpu.roll(v, HEAD_DIM, axis=1)
    for kvh in range(ATT_KV_HEADS):
        for slot in range(2):
            ks = (k, k_sw)[kvh ^ slot]
            vs = (v, v_sw)[kvh ^ slot]
            keep = lo if slot == 0 else ~lo
            rows = pl.ds(pl.multiple_of((2 * slot + parity) * blk, blk), blk)
            kb_ref[kvh, rows, 0:LANES] = jnp.where(keep, ks, 0.0).astype(BF16)
            vb_ref[kvh, rows, 0:LANES] = jnp.where(keep, vs, 0.0).astype(BF16)

    kj = lax.broadcasted_iota(jnp.int32, (blk, blk), 0)
    qi = lax.broadcasted_iota(jnp.int32, (blk, blk), 1)
    ones_lo = jnp.where(lo, 1.0, 0.0).astype(BF16)
    ones_hi = jnp.where(lo, 0.0, 1.0).astype(BF16)
    for half in range(2):
        is_cur = parity == half
        valid = (is_cur & (kj <= qi)) | ((~is_cur) & (nblk > 0) & (kj > qi))
        bias = jnp.where(valid, 0.0, MASK_BIAS).astype(BF16)
        for kvh in range(ATT_KV_HEADS):
            for slot in range(2):
                rows = slice((2 * slot + half) * blk, (2 * slot + half + 1) * blk)
                kb_ref[kvh, rows, LANES:] = bias
                vb_ref[kvh, rows, LANES:] = ones_hi if slot else ones_lo

    eye = jnp.where(kj == qi, 1.0, 0.0).astype(BF16)
    pairs = range(ATT_HEADS // 2)
    pairs_per_kv = ATT_HEADS // ATT_KV_HEADS // 2
    cols = [slice(p * LANES, (p + 1) * LANES) for p in pairs]
    q = [rope(q_ref[:, cs], q_tabs).astype(BF16) for cs in cols]
    s = [_dot_nt(jnp.concatenate([q[p], eye], axis=1), kb_ref[p // pairs_per_kv])
         for p in pairs]
    probs, sink = [], []
    for p in pairs:
        pe, se = [], []
        for e in range(2):
            logits = s[p][:, e * 2 * blk:(e + 1) * 2 * blk]
            sink2 = sink_ref[2 * p + e] * LOG2E
            m = jnp.maximum(jnp.max(logits, axis=-1, keepdims=True), sink2)
            pe.append(jnp.exp2(logits - m).astype(BF16))
            se.append(jnp.exp2(sink2 - m))
        probs.append(jnp.concatenate(pe, axis=1))
        sink.append(jnp.where(lo, se[0], se[1]))
    o = [_dot(probs[p], vb_ref[p // pairs_per_kv]) for p in pairs]
    for p in pairs:
        o_ref[:, cols[p]] = (o[p][:, 0:LANES] / (o[p][:, LANES:] + sink[p])).astype(BF16)


def _attention(sinks, freq8, pos3, proj, kv, bsz, seq, q_col_block):
    n = proj.shape[0]
    blk = WINDOW
    nb = seq // blk
    expand, shift = _attn_constants()
    row = lambda b, i: b * nb + i
    full = lambda a: pl.BlockSpec(a.shape, lambda b, i: (0,) * a.ndim)
    est = (2 * (blk * ATT_WIDTH * 2 * 2 + blk * 2 * KV_WIDTH * 2)
           + 2 * ATT_KV_HEADS * 4 * blk * 2 * LANES * 2 + 64 * blk * 4 * blk * 4)
    return pl.pallas_call(
        _attn_kernel,
        grid=(bsz, nb),
        in_specs=[pl.BlockSpec(memory_space=pltpu.SMEM),
                  full(freq8), full(expand), full(shift),
                  pl.BlockSpec((None, 1, blk), lambda b, i: (row(b, i), 0, 0)),
                  pl.BlockSpec((blk, ATT_WIDTH), lambda b, i: (row(b, i), q_col_block)),
                  pl.BlockSpec((blk, 2 * KV_WIDTH), lambda b, i: (row(b, i), 0))],
        out_specs=pl.BlockSpec((blk, ATT_WIDTH), lambda b, i: (row(b, i), 0)),
        out_shape=jax.ShapeDtypeStruct((n, ATT_WIDTH), BF16),
        scratch_shapes=[pltpu.VMEM((ATT_KV_HEADS, 4 * blk, 2 * LANES), BF16),
                        pltpu.VMEM((ATT_KV_HEADS, 4 * blk, 2 * LANES), BF16)],
        compiler_params=pltpu.CompilerParams(
            dimension_semantics=("parallel", "arbitrary"), vmem_limit_bytes=_vmem_limit(est)),
        name="swa_attn",
    )(sinks, freq8, expand, shift, pos3, proj, kv)


def _hgrn_constants():
    t = np.arange(CHUNK)[:, None]
    j = np.arange(CHUNK)[None, :]
    tri = (j <= t).astype(np.float32)
    blocks = [tri]
    for lv in LEVELS:
        ref = (t // (2 * lv)) * 2 * lv + lv - 1
        sel = (j <= ref).astype(np.float32)
        upper = (t // lv) % 2 == 1
        blocks.append(np.where(upper, tri - sel, sel - tri))
    blocks.append(1.0 - tri)
    m = np.concatenate(blocks, axis=0)
    cumsum = np.concatenate([m, m], axis=1)
    reduce = np.zeros((DIAG, HG_DIM, LANES), np.float32)
    for d in range(DIAG):
        reduce[d, :, DIAG - 1 - d] = 1.0
    return jnp.asarray(cumsum, BF16), jnp.asarray(reduce.reshape(DIAG * HG_DIM, LANES), BF16)


def _hgrn_kernel(cm_ref, red_ref, lbp_ref, gn_ref, q_ref, f_ref, i_ref, g_ref, o_ref, st_ref, *,
                 tokens):
    @pl.when(pl.program_id(1) == 0)
    def _():
        st_ref[...] = jnp.zeros_like(st_ref)

    lbp = lbp_ref[...]
    e = jnp.exp(lbp - jnp.max(lbp, axis=0, keepdims=True))
    lb = e[0:1, :] / jnp.sum(e, axis=0, keepdims=True)
    half_key_scale = 0.5 * (1.0 - lb)

    row = lax.broadcasted_iota(jnp.int32, (CHUNK, CHUNK), 0)
    col = lax.broadcasted_iota(jnp.int32, (CHUNK, CHUNK), 1)
    same = {w: row // w == col // w for w in (DIAG, 2 * DIAG, 4 * DIAG)}
    groups = CHUNK // DIAG

    heads = [slice(h * HG_DIM, (h + 1) * HG_DIM) for h in range(HG_HEADS)]
    chunks = [slice(c * CHUNK, (c + 1) * CHUNK) for c in range(tokens // CHUNK)]
    per_chunk = []

    for rs in chunks:
        half_q = 0.5 * q_ref[rs, :].astype(F32)
        qs = half_q + half_q * jnp.tanh(half_q)
        key = half_key_scale - half_key_scale * jnp.tanh(0.5 * f_ref[rs, :].astype(F32))
        f = 1.0 - key
        val = i_ref[rs, :]

        sums = _dot(cm_ref[...], jnp.concatenate(_split_bf16(jnp.log(f), 2), axis=0).astype(BF16))
        eb = jnp.exp(sums[0:CHUNK])
        q_state = (qs * eb).astype(BF16)
        k_state = (key * jnp.exp(sums[4 * CHUNK:5 * CHUNK])).astype(BF16)
        decay = eb[CHUNK - 1:CHUNK, :]

        q_lv, k_lv = [], []
        for n, lv in enumerate(LEVELS):
            w = jnp.exp(sums[(n + 1) * CHUNK:(n + 2) * CHUNK])
            qp, kp = [], []
            for piece in range(CHUNK // lv):
                ps = slice(piece * lv, (piece + 1) * lv)
                zero = jnp.zeros((lv, HG_WIDTH), F32)
                if piece % 2:
                    qp.append(qs[ps] * w[ps]); kp.append(zero)
                else:
                    qp.append(zero); kp.append(key[ps] * w[ps])
            q_lv.append(jnp.concatenate(qp, axis=0).astype(BF16))
            k_lv.append(jnp.concatenate(kp, axis=0).astype(BF16))

        carry = key.reshape(groups, DIAG, HG_WIDTH)
        f3 = f.reshape(groups, DIAG, HG_WIDTH)
        qs3 = qs.reshape(groups, DIAG, HG_WIDTH)
        diag = [(qs3 * carry).reshape(CHUNK, HG_WIDTH).astype(BF16)]
        for d in range(1, DIAG):
            carry = f3 * pltpu.roll(carry, 1, axis=1)
            diag.append((qs3 * carry).reshape(CHUNK, HG_WIDTH).astype(BF16))

        parts = [[_dot_nt(q_lv[n][:, hs], k_lv[n][:, hs]) for n in range(len(LEVELS))]
                 for hs in heads]
        r = _dot(jnp.concatenate([jnp.concatenate([dg[:, hs] for dg in diag], axis=1)
                                  for hs in heads], axis=0), red_ref[...])
        update = [_dot_tn(val[:, hs], k_state[:, hs]) for hs in heads]
        per_chunk.append((parts, r, update, q_state, val, decay))

    g_all = []
    for parts, r, _, _, _, _ in per_chunk:
        g_chunk = []
        for h in range(HG_HEADS):
            g_diag = pltpu.roll(r[h * CHUNK:(h + 1) * CHUNK], LANES - (DIAG - 1), axis=1,
                                stride=1, stride_axis=0)[:, 0:CHUNK]
            g_mat = jnp.where(same[DIAG], g_diag,
                              jnp.where(same[2 * DIAG], parts[h][2],
                                        jnp.where(same[4 * DIAG], parts[h][1], parts[h][0])))
            g_chunk.append(g_mat.astype(BF16))
        g_all.append(g_chunk)

    state = [st_ref[h] for h in range(HG_HEADS)]
    for rs, g_chunk, (_, _, update, q_state, val, decay) in zip(chunks, g_all, per_chunk):
        for h, hs in enumerate(heads):
            o = _dot_nt(q_state[:, hs], state[h].astype(BF16)) + _dot(g_chunk[h], val[:, hs])
            state[h] = state[h] * decay[:, hs] + update[h]
            y = _rms(o) * gn_ref[...]
            o_ref[rs, hs] = (y * _sigmoid(g_ref[rs, hs].astype(F32))).astype(BF16)
    for h in range(HG_HEADS):
        st_ref[h] = state[h]


def _hgrn(lb_params, gnorm, proj, bsz, seq, col_blocks):
    n = proj.shape[0]
    tokens = 2 * CHUNK
    nt = seq // tokens
    cmat, rmat = _hgrn_constants()
    row = lambda b, i: b * nt + i
    spec = lambda cb: pl.BlockSpec((tokens, HG_WIDTH), lambda b, i: (row(b, i), cb))
    full = lambda a: pl.BlockSpec(a.shape, lambda b, i: (0,) * a.ndim)
    est = 2 * 5 * tokens * HG_WIDTH * 2 + HG_HEADS * HG_DIM * HG_DIM * 4 + 40 * CHUNK * HG_WIDTH * 4
    return pl.pallas_call(
        functools.partial(_hgrn_kernel, tokens=tokens),
        grid=(bsz, nt),
        in_specs=[full(cmat), full(rmat), full(lb_params), full(gnorm),
                  spec(col_blocks[0]), spec(col_blocks[1]), spec(col_blocks[2]), spec(col_blocks[3])],
        out_specs=pl.BlockSpec((tokens, HG_WIDTH), lambda b, i: (row(b, i), 0)),
        out_shape=jax.ShapeDtypeStruct((n, HG_WIDTH), BF16),
        scratch_shapes=[pltpu.VMEM((HG_HEADS, HG_DIM, HG_DIM), F32)],
        compiler_params=pltpu.CompilerParams(
            dimension_semantics=("parallel", "arbitrary"), vmem_limit_bytes=_vmem_limit(est)),
        name="hgrn2",
    )(cmat, rmat, lb_params, gnorm, proj, proj, proj, proj)


def _mix_kernel(x_ref, mod_ref, g_ref, a_ref, hg_ref, ga_ref, gh_ref, wa_ref, wh_ref, wo_ref,
                o_ref, m_ref, *, tc):
    d = x_ref.shape[1]
    for c in range(d // tc):
        cs = slice(c * tc, (c + 1) * tc)
        ya = _dot(a_ref[...], wa_ref[:, cs])
        yh = _dot(hg_ref[...], wh_ref[:, cs])
        merged = (_sigmoid(ga_ref[:, cs].astype(F32)) * ya
                  + _sigmoid(gh_ref[:, cs].astype(F32)) * yh)
        m_ref[:, cs] = merged.astype(BF16)
    y = _dot(m_ref[...], wo_ref[...])
    o_ref[...] = x_ref[...] + mod_ref[2:3, :] * (_rms(y) * g_ref[...])


def _mix_out(x2, mod3, gain, attn, hg, proj, wa, wh, wo, seq, ga_block, gh_block):
    n, d = x2.shape
    tm, tc = 256, 512
    per_batch = seq // tm
    const = lambda shape: pl.BlockSpec(shape, lambda i: (0, 0), pipeline_mode=pl.Buffered(1))
    est = ((wa.size + wh.size + wo.size) * 2 + 2 * 2 * tm * d * 4 + 2 * 2 * tm * d * 2
           + 2 * 2 * tm * ATT_WIDTH * 2 + tm * d * 2 + 3 * tm * d * 4)
    return pl.pallas_call(
        functools.partial(_mix_kernel, tc=tc),
        grid=(n // tm,),
        in_specs=[pl.BlockSpec((tm, d), lambda i: (i, 0)),
                  pl.BlockSpec((None, N_MOD, d), lambda i: (i // per_batch, 0, 0)),
                  pl.BlockSpec((1, d), lambda i: (0, 0)),
                  pl.BlockSpec((tm, ATT_WIDTH), lambda i: (i, 0)),
                  pl.BlockSpec((tm, HG_WIDTH), lambda i: (i, 0)),
                  pl.BlockSpec((tm, d), lambda i: (i, ga_block)),
                  pl.BlockSpec((tm, d), lambda i: (i, gh_block)),
                  const(wa.shape), const(wh.shape), const(wo.shape)],
        out_specs=pl.BlockSpec((tm, d), lambda i: (i, 0)),
        out_shape=jax.ShapeDtypeStruct((n, d), F32),
        scratch_shapes=[pltpu.VMEM((tm, d), BF16)],
        compiler_params=pltpu.CompilerParams(
            dimension_semantics=("parallel",), vmem_limit_bytes=_vmem_limit(est)),
        name="mix_out",
    )(x2, mod3, gain, attn, hg, proj, proj, wa, wh, wo)


def _ffn_up_kernel(x_ref, mod_ref, gpre_ref, wg_ref, wu_ref, o_ref, h_ref):
    @pl.when(pl.program_id(1) == 0)
    def _():
        y = _rms(x_ref[...]) * gpre_ref[...]
        h_ref[...] = (y * (1.0 + mod_ref[4:5, :]) + mod_ref[3:4, :]).astype(BF16)

    h = h_ref[...]
    g = _dot(h, wg_ref[...])
    u = _dot(h, wu_ref[...])
    o_ref[...] = (g * _sigmoid(g) * u).astype(BF16)


def _ffn_down_kernel(a_ref, x_ref, mod_ref, gpost_ref, wd_ref, o_ref):
    y = _dot(a_ref[...], wd_ref[...])
    o_ref[...] = x_ref[...] + mod_ref[5:6, :] * (_rms(y) * gpost_ref[...])


def _ffn(x1, mod3, gpre, gpost, w_in, w_down, seq):
    n, d = x1.shape
    hidden = w_down.shape[0]

    tm, th = 1024, 512
    nh = hidden // th
    per_batch = seq // tm
    est = (2 * tm * d * 4 + tm * d * 2 + 2 * 2 * d * th * 2 + 2 * tm * th * 2 + 4 * tm * th * 4
           + tm * d * 4)
    act = pl.pallas_call(
        _ffn_up_kernel,
        grid=(n // tm, nh),
        in_specs=[pl.BlockSpec((tm, d), lambda i, j: (i, 0)),
                  pl.BlockSpec((None, N_MOD, d), lambda i, j: (i // per_batch, 0, 0)),
                  pl.BlockSpec((1, d), lambda i, j: (0, 0)),
                  pl.BlockSpec((d, th), lambda i, j: (0, j)),
                  pl.BlockSpec((d, th), lambda i, j: (0, nh + j))],
        out_specs=pl.BlockSpec((tm, th), lambda i, j: (i, j)),
        out_shape=jax.ShapeDtypeStruct((n, hidden), BF16),
        scratch_shapes=[pltpu.VMEM((tm, d), BF16)],
        compiler_params=pltpu.CompilerParams(
            dimension_semantics=("parallel", "arbitrary"), vmem_limit_bytes=_vmem_limit(est)),
        name="ffn_up",
    )(x1, mod3, gpre, w_in, w_in)

    tm = 256
    per_batch = seq // tm
    est = hidden * d * 2 + 2 * tm * hidden * 2 + 2 * 2 * tm * d * 4 + 2 * tm * d * 4
    return pl.pallas_call(
        _ffn_down_kernel,
        grid=(n // tm,),
        in_specs=[pl.BlockSpec((tm, hidden), lambda i: (i, 0)),
                  pl.BlockSpec((tm, d), lambda i: (i, 0)),
                  pl.BlockSpec((None, N_MOD, d), lambda i: (i // per_batch, 0, 0)),
                  pl.BlockSpec((1, d), lambda i: (0, 0)),
                  pl.BlockSpec((hidden, d), lambda i: (0, 0), pipeline_mode=pl.Buffered(1))],
        out_specs=pl.BlockSpec((tm, d), lambda i: (i, 0)),
        out_shape=jax.ShapeDtypeStruct((n, d), F32),
        compiler_params=pltpu.CompilerParams(
            dimension_semantics=("parallel",), vmem_limit_bytes=_vmem_limit(est)),
        name="ffn_down",
    )(act, x1, mod3, gpost, w_down)


def kernel(x, c, positions, w_ada, b_ada, g_pre_mix, g_post_mix, g_pre_ffn, g_post_ffn, w_in,
           attn_sinks, w_attn_proj, hg_lower_bounds, hg_norm, w_hgrn_proj, w_out, w_ffn_in,
           w_ffn_out):
    bsz, seq, d = x.shape
    n = bsz * seq
    assert d == D_MODEL and w_ada.shape[0] == 1 and seq % (2 * CHUNK) == 0 and seq % 1024 == 0

    o_q, o_k = 0, ATT_WIDTH
    o_qh = o_k + 2 * KV_WIDTH
    o_ga = o_qh + 4 * HG_WIDTH
    o_gh = o_ga + D_MODEL
    wi = w_in[0]
    w_main = jnp.concatenate(
        [wi[:, o_ga:o_gh + D_MODEL], wi[:, o_q:o_k], wi[:, o_qh:o_ga]], axis=1).astype(BF16)
    w_kv = wi[:, o_k:o_qh].astype(BF16)
    ga_block, gh_block = 0, 1
    q_block = 2 * D_MODEL // ATT_WIDTH
    hg_blocks = tuple((2 * D_MODEL + ATT_WIDTH) // HG_WIDTH + k for k in range(4))

    x2 = x.reshape(n, d)
    mod3 = _adaln(c, w_ada[0], b_ada[0]).reshape(bsz, N_MOD, d)

    proj, kv = _in_proj(x2, mod3, g_pre_mix, w_main, w_kv, seq)

    inv_freq = ROPE_THETA ** (-jnp.arange(0, ROT_DIM, 2, dtype=F32) / ROT_DIM)
    freq8 = jnp.broadcast_to(inv_freq[:, None], (ROT_HALF, WINDOW))
    pos3 = positions.astype(F32).reshape(n // WINDOW, 1, WINDOW)
    attn = _attention(attn_sinks[0], freq8, pos3, proj, kv, bsz, seq, q_block)

    hg = _hgrn(hg_lower_bounds, hg_norm, proj, bsz, seq, hg_blocks)

    x1 = _mix_out(x2, mod3, g_post_mix, attn, hg, proj, w_attn_proj[0].astype(BF16),
                  w_hgrn_proj[0].astype(BF16), w_out[0].astype(BF16), seq, ga_block, gh_block)

    out = _ffn(x1, mod3, g_pre_ffn, g_post_ffn, w_ffn_in[0].astype(BF16),
               w_ffn_out[0].astype(BF16), seq)
    return out.reshape(bsz, seq, d)
```

```python
import functools
import math

import numpy as np
import jax
import jax.numpy as jnp
from jax import lax
from jax.experimental import pallas as pl
from jax.experimental.pallas import tpu as pltpu

F32 = jnp.float32
BF16 = jnp.bfloat16

D_MODEL = 2048
ATT_HEADS = 16
ATT_KV_HEADS = 2
HEAD_DIM = 64
ATT_WIDTH = ATT_HEADS * HEAD_DIM
KV_WIDTH = ATT_KV_HEADS * HEAD_DIM
WINDOW = 128
ROT_DIM = HEAD_DIM // 4
ROT_HALF = ROT_DIM // 2
ROPE_THETA = 500000.0
HG_HEADS = 8
HG_DIM = 128
HG_WIDTH = HG_HEADS * HG_DIM
N_MOD = 6
EPS = 1e-6
LOG2E = math.log2(math.e)
MASK_BIAS = -1e30

LANES = 128
SUBLANES = 8
V7X_VMEM_BYTES = 64 * 1024 * 1024

CHUNK = 64
DIAG = SUBLANES
LEVELS = (32, 16, 8)


def _vmem_limit(estimate_bytes):
    return int(min(estimate_bytes * 3 // 2, V7X_VMEM_BYTES - 8 * 1024 * 1024))


def _dot(a, b):
    return jnp.dot(a, b, preferred_element_type=F32)


def _dot_nt(a, b):
    return lax.dot_general(a, b, (((1,), (1,)), ((), ())), preferred_element_type=F32)


def _dot_tn(a, b):
    return lax.dot_general(a, b, (((0,), (0,)), ((), ())), preferred_element_type=F32)


def _rms(t):
    return t * lax.rsqrt(jnp.mean(t * t, axis=-1, keepdims=True) + EPS)


def _sigmoid(t):
    return 0.5 + 0.5 * jnp.tanh(0.5 * t)


def _split_bf16(x, parts):
    out = []
    for _ in range(parts - 1):
        p = x.astype(BF16).astype(F32)
        out.append(p)
        x = x - p
    out.append(x.astype(BF16).astype(F32))
    return out


def _adaln_kernel(c_ref, w_ref, b_ref, o_ref):
    o_ref[...] = jnp.dot(c_ref[...], w_ref[...], preferred_element_type=F32,
                         precision=lax.Precision.HIGHEST) + b_ref[...]


def _adaln(c, w, b):
    bsz, d = c.shape
    n = w.shape[1]
    tn = 1024
    est = 2 * (d * tn * 4) + 2 * bsz * d * 4 + 4 * bsz * tn * 4
    return pl.pallas_call(
        _adaln_kernel,
        grid=(n // tn,),
        in_specs=[pl.BlockSpec((bsz, d), lambda j: (0, 0)),
                  pl.BlockSpec((d, tn), lambda j: (0, j)),
                  pl.BlockSpec((1, tn), lambda j: (0, j))],
        out_specs=pl.BlockSpec((bsz, tn), lambda j: (0, j)),
        out_shape=jax.ShapeDtypeStruct((bsz, n), F32),
        compiler_params=pltpu.CompilerParams(
            dimension_semantics=("arbitrary",), vmem_limit_bytes=_vmem_limit(est)),
        name="adaln_mod",
    )(c, w, b.reshape(1, n))


def _in_proj_kernel(x_ref, mod_ref, g_ref, w_ref, wkv_ref, o_ref, okv_ref, h_ref):
    @pl.when(pl.program_id(1) == 0)
    def _():
        y = _rms(x_ref[...]) * g_ref[...]
        h = y * (1.0 + mod_ref[1:2, :]) + mod_ref[0:1, :]
        h_ref[...] = h.astype(BF16)
        okv_ref[...] = _dot(h_ref[...], wkv_ref[...]).astype(BF16)

    o_ref[...] = _dot(h_ref[...], w_ref[...]).astype(BF16)


def _in_proj(x2, mod3, gain, w_in, seq):
    n, d = x2.shape
    tm, tn = 1024, 1024
    per_batch = seq // tm
    o_kv = ATT_WIDTH
    o_qh = o_kv + 2 * KV_WIDTH
    o_ga = o_qh + 4 * HG_WIDTH
    cols = w_in.shape[1] - 2 * KV_WIDTH
    n_gate = 2 * D_MODEL // tn
    n_q = ATT_WIDTH // tn

    def src_col(j):
        unit = tn // LANES
        lane_tile = jnp.where(j < n_gate, o_ga // LANES + unit * j,
                              jnp.where(j < n_gate + n_q, unit * (j - n_gate),
                                        o_qh // LANES + unit * (j - n_gate - n_q)))
        return lane_tile * LANES

    est = (2 * tm * d * 4 + tm * d * 2 + 2 * d * tn * 2 + 2 * tm * tn * 2
           + 2 * d * KV_WIDTH * 2 * 2 + 2 * tm * 2 * KV_WIDTH * 2 + tm * tn * 4 + tm * d * 4)
    return pl.pallas_call(
        _in_proj_kernel,
        grid=(n // tm, cols // tn),
        in_specs=[pl.BlockSpec((tm, d), lambda i, j: (i, 0)),
                  pl.BlockSpec((None, N_MOD, d), lambda i, j: (i // per_batch, 0, 0)),
                  pl.BlockSpec((1, d), lambda i, j: (0, 0)),
                  pl.BlockSpec((pl.Element(d), pl.Element(tn)), lambda i, j: (0, src_col(j))),
                  pl.BlockSpec((d, 2 * KV_WIDTH), lambda i, j: (0, o_kv // (2 * KV_WIDTH)))],
        out_specs=[pl.BlockSpec((tm, tn), lambda i, j: (i, j)),
                   pl.BlockSpec((tm, 2 * KV_WIDTH), lambda i, j: (i, 0))],
        out_shape=[jax.ShapeDtypeStruct((n, cols), BF16),
                   jax.ShapeDtypeStruct((n, 2 * KV_WIDTH), BF16)],
        scratch_shapes=[pltpu.VMEM((tm, d), BF16)],
        compiler_params=pltpu.CompilerParams(
            dimension_semantics=("parallel", "arbitrary"), vmem_limit_bytes=_vmem_limit(est)),
        name="in_proj",
    )(x2, mod3, gain, w_in, w_in)


def _attn_constants():
    lane = np.arange(LANES)
    dim = lane % HEAD_DIM
    expand = np.zeros((2, 3, ROT_HALF, 3, LANES), np.float32)
    for j in range(ROT_HALF):
        expand[0, :, j, 0, (dim < ROT_DIM) & (dim % ROT_HALF == j)] = 1.0
        expand[1, :, j, 1, (dim < ROT_HALF) & (dim == j)] = -1.0
        expand[1, :, j, 2, (dim >= ROT_HALF) & (dim < ROT_DIM) & (dim - ROT_HALF == j)] = 1.0
    expand = expand.reshape(6 * ROT_HALF, 3 * LANES)
    shift = np.zeros((LANES, 2 * LANES), np.float32)
    for l in range(LANES):
        if l + ROT_HALF < LANES:
            shift[l + ROT_HALF, l] = 1.0
        if l - ROT_HALF >= 0:
            shift[l - ROT_HALF, LANES + l] = 1.0
    return jnp.asarray(expand, BF16), jnp.asarray(shift, BF16)


def _attn_kernel(sink_ref, freq_ref, expand_ref, shift_ref, pos_ref, q_ref, kv_ref, o_ref,
                 kb_ref, vb_ref):
    nblk = pl.program_id(1)
    parity = nblk % 2
    blk = WINDOW
    lane = lax.broadcasted_iota(jnp.int32, (blk, LANES), 1)
    lo = lane < HEAD_DIM

    @pl.when(nblk == 0)
    def _():
        kb_ref[...] = jnp.zeros_like(kb_ref)
        vb_ref[...] = jnp.zeros_like(vb_ref)

    ang = freq_ref[...] * pos_ref[...]
    parts = _split_bf16(jnp.cos(ang), 3) + _split_bf16(jnp.sin(ang), 3)
    tabs = _dot_tn(jnp.concatenate(parts, axis=0).astype(BF16), expand_ref[...])
    dim = lane % HEAD_DIM
    k_tabs = (tabs[:, 0:LANES] + jnp.where(dim < ROT_DIM, 0.0, 1.0),
              tabs[:, LANES:2 * LANES], tabs[:, 2 * LANES:3 * LANES])
    q_tabs = tuple(t * (HEAD_DIM ** -0.5 * LOG2E) for t in k_tabs)

    def rope(x_bf16, tables):
        t_cos, t_lo, t_hi = tables
        ud = _dot(x_bf16, shift_ref[...])
        return x_bf16.astype(F32) * t_cos + ud[:, 0:LANES] * t_lo + ud[:, LANES:] * t_hi

    k = rope(kv_ref[:, 0:KV_WIDTH], k_tabs)
    v = kv_ref[:, KV_WIDTH:].astype(F32)
    k_sw = pltpu.roll(k, HEAD_DIM, axis=1)
    v_sw = pltpu.roll(v, HEAD_DIM, axis=1)
    for kvh in range(ATT_KV_HEADS):
        for slot in range(2):
            ks = (k, k_sw)[kvh ^ slot]
            vs = (v, v_sw)[kvh ^ slot]
            keep = lo if slot == 0 else ~lo
            rows = pl.ds(pl.multiple_of((2 * slot + parity) * blk, blk), blk)
            kb_ref[kvh, rows, 0:LANES] = jnp.where(keep, ks, 0.0).astype(BF16)
            vb_ref[kvh, rows, 0:LANES] = jnp.where(keep, vs, 0.0).astype(BF16)

    kj = lax.broadcasted_iota(jnp.int32, (blk, blk), 0)
    qi = lax.broadcasted_iota(jnp.int32, (blk, blk), 1)
    ones_lo = jnp.where(lo, 1.0, 0.0).astype(BF16)
    ones_hi = jnp.where(lo, 0.0, 1.0).astype(BF16)
    for half in range(2):
        is_cur = parity == half
        valid = (is_cur & (kj <= qi)) | ((~is_cur) & (nblk > 0) & (kj > qi))
        bias = jnp.where(valid, 0.0, MASK_BIAS).astype(BF16)
        for kvh in range(ATT_KV_HEADS):
            for slot in range(2):
                rows = slice((2 * slot + half) * blk, (2 * slot + half + 1) * blk)
                kb_ref[kvh, rows, LANES:] = bias
                vb_ref[kvh, rows, LANES:] = ones_hi if slot else ones_lo

    eye = jnp.where(kj == qi, 1.0, 0.0).astype(BF16)
    pairs = range(ATT_HEADS // 2)
    pairs_per_kv = ATT_HEADS // ATT_KV_HEADS // 2
    cols = [slice(p * LANES, (p + 1) * LANES) for p in pairs]
    q = [rope(q_ref[:, cs], q_tabs).astype(BF16) for cs in cols]
    s = [_dot_nt(jnp.concatenate([q[p], eye], axis=1), kb_ref[p // pairs_per_kv])
         for p in pairs]
    probs, sink = [], []
    for p in pairs:
        pe, se = [], []
        for e in range(2):
            logits = s[p][:, e * 2 * blk:(e + 1) * 2 * blk]
            sink2 = sink_ref[2 * p + e] * LOG2E
            m = jnp.maximum(jnp.max(logits, axis=-1, keepdims=True), sink2)
            pe.append(jnp.exp2(logits - m).astype(BF16))
            se.append(jnp.exp2(sink2 - m))
        probs.append(jnp.concatenate(pe, axis=1))
        sink.append(jnp.where(lo, se[0], se[1]))
    o = [_dot(probs[p], vb_ref[p // pairs_per_kv]) for p in pairs]
    for p in pairs:
        o_ref[:, cols[p]] = (o[p][:, 0:LANES] / (o[p][:, LANES:] + sink[p])).astype(BF16)


def _attention(sinks, freq8, pos3, proj, kv, bsz, seq, q_col_block):
    n = proj.shape[0]
    blk = WINDOW
    nb = seq // blk
    expand, shift = _attn_constants()
    row = lambda b, i: b * nb + i
    full = lambda a: pl.BlockSpec(a.shape, lambda b, i: (0,) * a.ndim)
    est = (2 * (blk * ATT_WIDTH * 2 * 2 + blk * 2 * KV_WIDTH * 2)
           + 2 * ATT_KV_HEADS * 4 * blk * 2 * LANES * 2 + 64 * blk * 4 * blk * 4)
    return pl.pallas_call(
        _attn_kernel,
        grid=(bsz, nb),
        in_specs=[pl.BlockSpec(memory_space=pltpu.SMEM),
                  full(freq8), full(expand), full(shift),
                  pl.BlockSpec((None, 1, blk), lambda b, i: (row(b, i), 0, 0)),
                  pl.BlockSpec((blk, ATT_WIDTH), lambda b, i: (row(b, i), q_col_block)),
                  pl.BlockSpec((blk, 2 * KV_WIDTH), lambda b, i: (row(b, i), 0))],
        out_specs=pl.BlockSpec((blk, ATT_WIDTH), lambda b, i: (row(b, i), 0)),
        out_shape=jax.ShapeDtypeStruct((n, ATT_WIDTH), BF16),
        scratch_shapes=[pltpu.VMEM((ATT_KV_HEADS, 4 * blk, 2 * LANES), BF16),
                        pltpu.VMEM((ATT_KV_HEADS, 4 * blk, 2 * LANES), BF16)],
        compiler_params=pltpu.CompilerParams(
            dimension_semantics=("parallel", "arbitrary"), vmem_limit_bytes=_vmem_limit(est)),
        name="swa_attn",
    )(sinks, freq8, expand, shift, pos3, proj, kv)


def _hgrn_constants():
    t = np.arange(CHUNK)[:, None]
    j = np.arange(CHUNK)[None, :]
    tri = (j <= t).astype(np.float32)
    cumsum = np.concatenate([tri, tri], axis=1)
    reduce = np.zeros((DIAG, HG_DIM, LANES), np.float32)
    for d in range(DIAG):
        reduce[d, :, DIAG - 1 - d] = 1.0
    return jnp.asarray(cumsum, BF16), jnp.asarray(reduce.reshape(DIAG * HG_DIM, LANES), BF16)


def _hgrn_kernel(cm_ref, red_ref, lbp_ref, gn_ref, q_ref, f_ref, i_ref, g_ref, o_ref, st_ref, *,
                 tokens):
    @pl.when(pl.program_id(1) == 0)
    def _():
        st_ref[...] = jnp.zeros_like(st_ref)

    lbp = lbp_ref[...]
    e = jnp.exp(lbp - jnp.max(lbp, axis=0, keepdims=True))
    lb = e[0:1, :] / jnp.sum(e, axis=0, keepdims=True)
    half_key_scale = 0.5 * (1.0 - lb)

    row = lax.broadcasted_iota(jnp.int32, (CHUNK, CHUNK), 0)
    col = lax.broadcasted_iota(jnp.int32, (CHUNK, CHUNK), 1)
    same = {w: row // w == col // w for w in (DIAG, 2 * DIAG, 4 * DIAG)}
    groups = CHUNK // DIAG

    heads = [slice(h * HG_DIM, (h + 1) * HG_DIM) for h in range(HG_HEADS)]
    chunks = [slice(c * CHUNK, (c + 1) * CHUNK) for c in range(tokens // CHUNK)]
    per_chunk = []

    for rs in chunks:
        half_q = 0.5 * q_ref[rs, :].astype(F32)
        qs = half_q + half_q * jnp.tanh(half_q)
        key = half_key_scale - half_key_scale * jnp.tanh(0.5 * f_ref[rs, :].astype(F32))
        f = 1.0 - key
        val = i_ref[rs, :]

        b = _dot(cm_ref[...], jnp.concatenate(_split_bf16(jnp.log2(f), 2), axis=0).astype(BF16))
        eb = jnp.exp2(b)
        q_state = (qs * eb).astype(BF16)
        k_state = (key * jnp.exp2(b[CHUNK - 1:CHUNK, :] - b)).astype(BF16)
        decay = eb[CHUNK - 1:CHUNK, :]

        q_lv, k_lv = [], []
        for lv in LEVELS:
            qp, kp = [], []
            for piece in range(CHUNK // lv):
                ps = slice(piece * lv, (piece + 1) * lv)
                zero = jnp.zeros((lv, HG_WIDTH), F32)
                if piece % 2:
                    ref = b[piece * lv - 1:piece * lv, :]
                    qp.append(qs[ps] * jnp.exp2(b[ps] - ref)); kp.append(zero)
                else:
                    ref = b[(piece + 1) * lv - 1:(piece + 1) * lv, :]
                    qp.append(zero); kp.append(key[ps] * jnp.exp2(ref - b[ps]))
            q_lv.append(jnp.concatenate(qp, axis=0).astype(BF16))
            k_lv.append(jnp.concatenate(kp, axis=0).astype(BF16))

        carry = key.reshape(groups, DIAG, HG_WIDTH)
        f3 = f.reshape(groups, DIAG, HG_WIDTH)
        qs3 = qs.reshape(groups, DIAG, HG_WIDTH)
        diag = [(qs3 * carry).reshape(CHUNK, HG_WIDTH).astype(BF16)]
        for d in range(1, DIAG):
            carry = f3 * pltpu.roll(carry, 1, axis=1)
            diag.append((qs3 * carry).reshape(CHUNK, HG_WIDTH).astype(BF16))

        parts = [[_dot_nt(q_lv[n][:, hs], k_lv[n][:, hs]) for n in range(len(LEVELS))]
                 for hs in heads]
        r = _dot(jnp.concatenate([jnp.concatenate([dg[:, hs] for dg in diag], axis=1)
                                  for hs in heads], axis=0), red_ref[...])
        update = [_dot_tn(val[:, hs], k_state[:, hs]) for hs in heads]
        per_chunk.append((parts, r, update, q_state, val, decay))

    g_all = []
    for parts, r, _, _, _, _ in per_chunk:
        g_chunk = []
        for h in range(HG_HEADS):
            g_diag = pltpu.roll(r[h * CHUNK:(h + 1) * CHUNK], LANES - (DIAG - 1), axis=1,
                                stride=1, stride_axis=0)[:, 0:CHUNK]
            g_mat = jnp.where(same[DIAG], g_diag,
                              jnp.where(same[2 * DIAG], parts[h][2],
                                        jnp.where(same[4 * DIAG], parts[h][1], parts[h][0])))
            g_chunk.append(g_mat.astype(BF16))
        g_all.append(g_chunk)

    state = [st_ref[h] for h in range(HG_HEADS)]
    for rs, g_chunk, (_, _, update, q_state, val, decay) in zip(chunks, g_all, per_chunk):
        for h, hs in enumerate(heads):
            o = _dot_nt(q_state[:, hs], state[h].astype(BF16)) + _dot(g_chunk[h], val[:, hs])
            state[h] = state[h] * decay[:, hs] + update[h]
            y = _rms(o) * gn_ref[...]
            o_ref[rs, hs] = (y * _sigmoid(g_ref[rs, hs].astype(F32))).astype(BF16)
    for h in range(HG_HEADS):
        st_ref[h] = state[h]


def _hgrn(lb_params, gnorm, proj, bsz, seq, col_blocks):
    n = proj.shape[0]
    tokens = 4 * CHUNK
    nt = seq // tokens
    cmat, rmat = _hgrn_constants()
    row = lambda b, i: b * nt + i
    spec = lambda cb: pl.BlockSpec((tokens, HG_WIDTH), lambda b, i: (row(b, i), cb))
    full = lambda a: pl.BlockSpec(a.shape, lambda b, i: (0,) * a.ndim)
    est = 2 * 5 * tokens * HG_WIDTH * 2 + HG_HEADS * HG_DIM * HG_DIM * 4 + 40 * CHUNK * HG_WIDTH * 4
    return pl.pallas_call(
        functools.partial(_hgrn_kernel, tokens=tokens),
        grid=(bsz, nt),
        in_specs=[full(cmat), full(rmat), full(lb_params), full(gnorm),
                  spec(col_blocks[0]), spec(col_blocks[1]), spec(col_blocks[2]), spec(col_blocks[3])],
        out_specs=pl.BlockSpec((tokens, HG_WIDTH), lambda b, i: (row(b, i), 0)),
        out_shape=jax.ShapeDtypeStruct((n, HG_WIDTH), BF16),
        scratch_shapes=[pltpu.VMEM((HG_HEADS, HG_DIM, HG_DIM), F32)],
        compiler_params=pltpu.CompilerParams(
            dimension_semantics=("parallel", "arbitrary"), vmem_limit_bytes=_vmem_limit(est)),
        name="hgrn2",
    )(cmat, rmat, lb_params, gnorm, proj, proj, proj, proj)


def _mix_kernel(x_ref, mod_ref, g_ref, a_ref, hg_ref, ga_ref, gh_ref, wa_ref, wh_ref, wo_ref,
                o_ref, m_ref, *, tc):
    d = x_ref.shape[1]
    for c in range(d // tc):
        cs = slice(c * tc, (c + 1) * tc)
        ya = _dot(a_ref[...], wa_ref[:, cs])
        yh = _dot(hg_ref[...], wh_ref[:, cs])
        merged = (_sigmoid(ga_ref[:, cs].astype(F32)) * ya
                  + _sigmoid(gh_ref[:, cs].astype(F32)) * yh)
        m_ref[:, cs] = merged.astype(BF16)
    y = _dot(m_ref[...], wo_ref[...])
    o_ref[...] = x_ref[...] + mod_ref[2:3, :] * (_rms(y) * g_ref[...])


def _mix_out(x2, mod3, gain, attn, hg, proj, wa, wh, wo, seq, ga_block, gh_block):
    n, d = x2.shape
    tm, tc = 256, 512
    per_batch = seq // tm
    const = lambda shape: pl.BlockSpec(shape, lambda i: (0, 0), pipeline_mode=pl.Buffered(1))
    est = ((wa.size + wh.size + wo.size) * 2 + 2 * 2 * tm * d * 4 + 2 * 2 * tm * d * 2
           + 2 * 2 * tm * ATT_WIDTH * 2 + tm * d * 2 + 3 * tm * d * 4)
    return pl.pallas_call(
        functools.partial(_mix_kernel, tc=tc),
        grid=(n // tm,),
        in_specs=[pl.BlockSpec((tm, d), lambda i: (i, 0)),
                  pl.BlockSpec((None, N_MOD, d), lambda i: (i // per_batch, 0, 0)),
                  pl.BlockSpec((1, d), lambda i: (0, 0)),
                  pl.BlockSpec((tm, ATT_WIDTH), lambda i: (i, 0)),
                  pl.BlockSpec((tm, HG_WIDTH), lambda i: (i, 0)),
                  pl.BlockSpec((tm, d), lambda i: (i, ga_block)),
                  pl.BlockSpec((tm, d), lambda i: (i, gh_block)),
                  const(wa.shape), const(wh.shape), const(wo.shape)],
        out_specs=pl.BlockSpec((tm, d), lambda i: (i, 0)),
        out_shape=jax.ShapeDtypeStruct((n, d), F32),
        scratch_shapes=[pltpu.VMEM((tm, d), BF16)],
        compiler_params=pltpu.CompilerParams(
            dimension_semantics=("parallel",), vmem_limit_bytes=_vmem_limit(est)),
        name="mix_out",
    )(x2, mod3, gain, attn, hg, proj, proj, wa, wh, wo)


def _ffn_up_kernel(x_ref, mod_ref, gpre_ref, wg_ref, wu_ref, o_ref, h_ref):
    @pl.when(pl.program_id(1) == 0)
    def _():
        y = _rms(x_ref[...]) * gpre_ref[...]
        h_ref[...] = (y * (1.0 + mod_ref[4:5, :]) + mod_ref[3:4, :]).astype(BF16)

    h = h_ref[...]
    g = _dot(h, wg_ref[...].astype(BF16))
    u = _dot(h, wu_ref[...].astype(BF16))
    o_ref[...] = (g * _sigmoid(g) * u).astype(BF16)


def _ffn_down_kernel(a_ref, x_ref, mod_ref, gpost_ref, wd_ref, o_ref):
    y = _dot(a_ref[...], wd_ref[...])
    o_ref[...] = x_ref[...] + mod_ref[5:6, :] * (_rms(y) * gpost_ref[...])


def _ffn(x1, mod3, gpre, gpost, w_in, w_down, seq):
    n, d = x1.shape
    hidden = w_down.shape[0]

    tm, th = 1024, 512
    nh = hidden // th
    per_batch = seq // tm
    est = (2 * tm * d * 4 + tm * d * 2 + 2 * 2 * d * th * 4 + 2 * d * th * 2 + 2 * tm * th * 2
           + 4 * tm * th * 4 + tm * d * 4)
    act = pl.pallas_call(
        _ffn_up_kernel,
        grid=(n // tm, nh),
        in_specs=[pl.BlockSpec((tm, d), lambda i, j: (i, 0)),
                  pl.BlockSpec((None, N_MOD, d), lambda i, j: (i // per_batch, 0, 0)),
                  pl.BlockSpec((1, d), lambda i, j: (0, 0)),
                  pl.BlockSpec((d, th), lambda i, j: (0, j)),
                  pl.BlockSpec((d, th), lambda i, j: (0, nh + j))],
        out_specs=pl.BlockSpec((tm, th), lambda i, j: (i, j)),
        out_shape=jax.ShapeDtypeStruct((n, hidden), BF16),
        scratch_shapes=[pltpu.VMEM((tm, d), BF16)],
        compiler_params=pltpu.CompilerParams(
            dimension_semantics=("parallel", "arbitrary"), vmem_limit_bytes=_vmem_limit(est)),
        name="ffn_up",
    )(x1, mod3, gpre, w_in, w_in)

    tm = 256
    per_batch = seq // tm
    est = hidden * d * 2 + 2 * tm * hidden * 2 + 2 * 2 * tm * d * 4 + 2 * tm * d * 4
    return pl.pallas_call(
        _ffn_down_kernel,
        grid=(n // tm,),
        in_specs=[pl.BlockSpec((tm, hidden), lambda i: (i, 0)),
                  pl.BlockSpec((tm, d), lambda i: (i, 0)),
                  pl.BlockSpec((None, N_MOD, d), lambda i: (i // per_batch, 0, 0)),
                  pl.BlockSpec((1, d), lambda i: (0, 0)),
                  pl.BlockSpec((hidden, d), lambda i: (0, 0), pipeline_mode=pl.Buffered(1))],
        out_specs=pl.BlockSpec((tm, d), lambda i: (i, 0)),
        out_shape=jax.ShapeDtypeStruct((n, d), F32),
        compiler_params=pltpu.CompilerParams(
            dimension_semantics=("parallel",), vmem_limit_bytes=_vmem_limit(est)),
        name="ffn_down",
    )(act, x1, mod3, gpost, w_down)


def kernel(x, c, positions, w_ada, b_ada, g_pre_mix, g_post_mix, g_pre_ffn, g_post_ffn, w_in,
           attn_sinks, w_attn_proj, hg_lower_bounds, hg_norm, w_hgrn_proj, w_out, w_ffn_in,
           w_ffn_out):
    bsz, seq, d = x.shape
    n = bsz * seq
    assert d == D_MODEL and w_ada.shape[0] == 1 and seq % (2 * CHUNK) == 0 and seq % 1024 == 0

    ga_block, gh_block = 0, 1
    q_block = 2 * D_MODEL // ATT_WIDTH
    hg_blocks = tuple((2 * D_MODEL + ATT_WIDTH) // HG_WIDTH + k for k in range(4))

    x2 = x.reshape(n, d)
    mod3 = _adaln(c, w_ada[0], b_ada[0]).reshape(bsz, N_MOD, d)

    proj, kv = _in_proj(x2, mod3, g_pre_mix, w_in[0].astype(BF16), seq)

    inv_freq = ROPE_THETA ** (-jnp.arange(0, ROT_DIM, 2, dtype=F32) / ROT_DIM)
    freq8 = jnp.broadcast_to(inv_freq[:, None], (ROT_HALF, WINDOW))
    pos3 = positions.astype(F32).reshape(n // WINDOW, 1, WINDOW)
    attn = _attention(attn_sinks[0], freq8, pos3, proj, kv, bsz, seq, q_block)

    hg = _hgrn(hg_lower_bounds, hg_norm, proj, bsz, seq, hg_blocks)

    x1 = _mix_out(x2, mod3, g_post_mix, attn, hg, proj, w_attn_proj[0].astype(BF16),
                  w_hgrn_proj[0].astype(BF16), w_out[0].astype(BF16), seq, ga_block, gh_block)

    out = _ffn(x1, mod3, g_pre_ffn, g_post_ffn, w_ffn_in[0], w_ffn_out[0].astype(BF16), seq)
    return out.reshape(bsz, seq, d)
```

```python
import functools
import math

import numpy as np
import jax
import jax.numpy as jnp
from jax import lax
from jax.experimental import pallas as pl
from jax.experimental.pallas import tpu as pltpu

F32 = jnp.float32
BF16 = jnp.bfloat16

D_MODEL = 2048
ATT_HEADS = 16
ATT_KV_HEADS = 2
HEAD_DIM = 64
ATT_WIDTH = ATT_HEADS * HEAD_DIM
KV_WIDTH = ATT_KV_HEADS * HEAD_DIM
WINDOW = 128
ROT_DIM = HEAD_DIM // 4
ROT_HALF = ROT_DIM // 2
ROPE_THETA = 500000.0
HG_HEADS = 8
HG_DIM = 128
HG_WIDTH = HG_HEADS * HG_DIM
N_MOD = 6
EPS = 1e-6
LOG2E = math.log2(math.e)
MASK_BIAS = -1e30

LANES = 128
SUBLANES = 8
V7X_VMEM_BYTES = 64 * 1024 * 1024

CHUNK = 64
DIAG = SUBLANES
LEVELS = (32, 16, 8)


def _vmem_limit(estimate_bytes):
    return int(min(estimate_bytes * 3 // 2, V7X_VMEM_BYTES - 8 * 1024 * 1024))


def _dot(a, b):
    return jnp.dot(a, b, preferred_element_type=F32)


def _dot_nt(a, b):
    return lax.dot_general(a, b, (((1,), (1,)), ((), ())), preferred_element_type=F32)


def _dot_tn(a, b):
    return lax.dot_general(a, b, (((0,), (0,)), ((), ())), preferred_element_type=F32)


def _rms(t):
    return t * lax.rsqrt(jnp.mean(t * t, axis=-1, keepdims=True) + EPS)


def _sigmoid(t):
    return 0.5 + 0.5 * jnp.tanh(0.5 * t)


def _split_bf16(x, parts):
    out = []
    for _ in range(parts - 1):
        p = x.astype(BF16).astype(F32)
        out.append(p)
        x = x - p
    out.append(x.astype(BF16).astype(F32))
    return out


def _adaln_kernel(c_ref, w_ref, b_ref, o_ref):
    bsz = c_ref.shape[0]
    c_hi, c_lo = _split_bf16(c_ref[...], 2)
    w_hi, w_lo = _split_bf16(w_ref[...], 2)
    by_hi = _dot(jnp.concatenate([c_hi, c_lo], axis=0).astype(BF16), w_hi.astype(BF16))
    o_ref[...] = (by_hi[0:bsz] + by_hi[bsz:] + _dot(c_hi.astype(BF16), w_lo.astype(BF16))
                  + b_ref[...])


def _adaln(c, w, b):
    bsz, d = c.shape
    n = w.shape[1]
    tn = 1024
    est = 2 * (d * tn * 4) + 2 * bsz * d * 4 + 4 * bsz * tn * 4 + 3 * d * tn * 4
    return pl.pallas_call(
        _adaln_kernel,
        grid=(n // tn,),
        in_specs=[pl.BlockSpec((bsz, d), lambda j: (0, 0)),
                  pl.BlockSpec((d, tn), lambda j: (0, j)),
                  pl.BlockSpec((1, tn), lambda j: (0, j))],
        out_specs=pl.BlockSpec((bsz, tn), lambda j: (0, j)),
        out_shape=jax.ShapeDtypeStruct((bsz, n), F32),
        compiler_params=pltpu.CompilerParams(
            dimension_semantics=("arbitrary",), vmem_limit_bytes=_vmem_limit(est)),
        name="adaln_mod",
    )(c, w, b.reshape(1, n))


def _in_proj_kernel(x_ref, mod_ref, g_ref, w_ref, wkv_ref, o_ref, okv_ref, h_ref):
    @pl.when(pl.program_id(1) == 0)
    def _():
        y = _rms(x_ref[...]) * g_ref[...]
        h = y * (1.0 + mod_ref[1:2, :]) + mod_ref[0:1, :]
        h_ref[...] = h.astype(BF16)
        okv_ref[...] = _dot(h_ref[...], wkv_ref[...]).astype(BF16)

    o_ref[...] = _dot(h_ref[...], w_ref[...]).astype(BF16)


def _in_proj(x2, mod3, gain, w_in, seq):
    n, d = x2.shape
    tm, tn = 1024, 1024
    per_batch = seq // tm
    o_kv = ATT_WIDTH
    o_qh = o_kv + 2 * KV_WIDTH
    o_ga = o_qh + 4 * HG_WIDTH
    cols = w_in.shape[1] - 2 * KV_WIDTH
    n_gate = 2 * D_MODEL // tn
    n_q = ATT_WIDTH // tn

    def src_col(j):
        unit = tn // LANES
        lane_tile = jnp.where(j < n_gate, o_ga // LANES + unit * j,
                              jnp.where(j < n_gate + n_q, unit * (j - n_gate),
                                        o_qh // LANES + unit * (j - n_gate - n_q)))
        return lane_tile * LANES

    est = (2 * tm * d * 4 + tm * d * 2 + 2 * d * tn * 2 + 2 * tm * tn * 2
           + 2 * d * KV_WIDTH * 2 * 2 + 2 * tm * 2 * KV_WIDTH * 2 + tm * tn * 4 + tm * d * 4)
    return pl.pallas_call(
        _in_proj_kernel,
        grid=(n // tm, cols // tn),
        in_specs=[pl.BlockSpec((tm, d), lambda i, j: (i, 0)),
                  pl.BlockSpec((None, N_MOD, d), lambda i, j: (i // per_batch, 0, 0)),
                  pl.BlockSpec((1, d), lambda i, j: (0, 0)),
                  pl.BlockSpec((pl.Element(d), pl.Element(tn)), lambda i, j: (0, src_col(j))),
                  pl.BlockSpec((d, 2 * KV_WIDTH), lambda i, j: (0, o_kv // (2 * KV_WIDTH)))],
        out_specs=[pl.BlockSpec((tm, tn), lambda i, j: (i, j)),
                   pl.BlockSpec((tm, 2 * KV_WIDTH), lambda i, j: (i, 0))],
        out_shape=[jax.ShapeDtypeStruct((n, cols), BF16),
                   jax.ShapeDtypeStruct((n, 2 * KV_WIDTH), BF16)],
        scratch_shapes=[pltpu.VMEM((tm, d), BF16)],
        compiler_params=pltpu.CompilerParams(
            dimension_semantics=("parallel", "arbitrary"), vmem_limit_bytes=_vmem_limit(est)),
        name="in_proj",
    )(x2, mod3, gain, w_in, w_in)


def _attn_constants():
    lane = np.arange(LANES)
    dim = lane % HEAD_DIM
    expand = np.zeros((2, 3, ROT_HALF, 3, LANES), np.float32)
    for j in range(ROT_HALF):
        expand[0, :, j, 0, (dim < ROT_DIM) & (dim % ROT_HALF == j)] = 1.0
        expand[1, :, j, 1, (dim < ROT_HALF) & (dim == j)] = -1.0
        expand[1, :, j, 2, (dim >= ROT_HALF) & (dim < ROT_DIM) & (dim - ROT_HALF == j)] = 1.0
    expand = expand.reshape(6 * ROT_HALF, 3 * LANES)
    shift = np.zeros((LANES, 2 * LANES), np.float32)
    for l in range(LANES):
        if l + ROT_HALF < LANES:
            shift[l + ROT_HALF, l] = 1.0
        if l - ROT_HALF >= 0:
            shift[l - ROT_HALF, LANES + l] = 1.0
    return jnp.asarray(expand, BF16), jnp.asarray(shift, BF16)


def _attn_kernel(sink_ref, freq_ref, expand_ref, shift_ref, pos_ref, q_ref, kv_ref, o_ref,
                 kb_ref, vb_ref):
    nblk = pl.program_id(1)
    parity = nblk % 2
    blk = WINDOW
    lane = lax.broadcasted_iota(jnp.int32, (blk, LANES), 1)
    lo = lane < HEAD_DIM

    @pl.when(nblk == 0)
    def _():
        kb_ref[...] = jnp.zeros_like(kb_ref)
        vb_ref[...] = jnp.zeros_like(vb_ref)

    ang = freq_ref[...] * pos_ref[...]
    parts = _split_bf16(jnp.cos(ang), 3) + _split_bf16(jnp.sin(ang), 3)
    tabs = _dot_tn(jnp.concatenate(parts, axis=0).astype(BF16), expand_ref[...])
    dim = lane % HEAD_DIM
    k_tabs = (tabs[:, 0:LANES] + jnp.where(dim < ROT_DIM, 0.0, 1.0),
              tabs[:, LANES:2 * LANES], tabs[:, 2 * LANES:3 * LANES])
    q_tabs = tuple(t * (HEAD_DIM ** -0.5 * LOG2E) for t in k_tabs)

    def rope(x_bf16, tables):
        t_cos, t_lo, t_hi = tables
        ud = _dot(x_bf16, shift_ref[...])
        return x_bf16.astype(F32) * t_cos + ud[:, 0:LANES] * t_lo + ud[:, LANES:] * t_hi

    k = rope(kv_ref[:, 0:KV_WIDTH], k_tabs)
    v = kv_ref[:, KV_WIDTH:].astype(F32)
    k_sw = pltpu.roll(k, HEAD_DIM, axis=1)
    v_sw = pltpu.roll(v, HEAD_DIM, axis=1)
    for kvh in range(ATT_KV_HEADS):
        for slot in range(2):
            ks = (k, k_sw)[kvh ^ slot]
            vs = (v, v_sw)[kvh ^ slot]
            keep = lo if slot == 0 else ~lo
            rows = pl.ds(pl.multiple_of((2 * slot + parity) * blk, blk), blk)
            kb_ref[kvh, rows, 0:LANES] = jnp.where(keep, ks, 0.0).astype(BF16)
            vb_ref[kvh, rows, 0:LANES] = jnp.where(keep, vs, 0.0).astype(BF16)

    kj = lax.broadcasted_iota(jnp.int32, (blk, blk), 0)
    qi = lax.broadcasted_iota(jnp.int32, (blk, blk), 1)
    ones_lo = jnp.where(lo, 1.0, 0.0).astype(BF16)
    ones_hi = jnp.where(lo, 0.0, 1.0).astype(BF16)
    for half in range(2):
        is_cur = parity == half
        valid = (is_cur & (kj <= qi)) | ((~is_cur) & (nblk > 0) & (kj > qi))
        bias = jnp.where(valid, 0.0, MASK_BIAS).astype(BF16)
        for kvh in range(ATT_KV_HEADS):
            for slot in range(2):
                rows = slice((2 * slot + half) * blk, (2 * slot + half + 1) * blk)
                kb_ref[kvh, rows, LANES:] = bias
                vb_ref[kvh, rows, LANES:] = ones_hi if slot else ones_lo

    eye = jnp.where(kj == qi, 1.0, 0.0).astype(BF16)
    pairs = range(ATT_HEADS // 2)
    pairs_per_kv = ATT_HEADS // ATT_KV_HEADS // 2
    cols = [slice(p * LANES, (p + 1) * LANES) for p in pairs]
    q = [rope(q_ref[:, cs], q_tabs).astype(BF16) for cs in cols]
    s = [_dot_nt(jnp.concatenate([q[p], eye], axis=1), kb_ref[p // pairs_per_kv])
         for p in pairs]
    probs, sink = [], []
    for p in pairs:
        pe, se = [], []
        for e in range(2):
            logits = s[p][:, e * 2 * blk:(e + 1) * 2 * blk]
            sink2 = sink_ref[2 * p + e] * LOG2E
            m = jnp.maximum(jnp.max(logits, axis=-1, keepdims=True), sink2)
            pe.append(jnp.exp2(logits - m).astype(BF16))
            se.append(jnp.exp2(sink2 - m))
        probs.append(jnp.concatenate(pe, axis=1))
        sink.append(jnp.where(lo, se[0], se[1]))
    o = [_dot(probs[p], vb_ref[p // pairs_per_kv]) for p in pairs]
    for p in pairs:
        o_ref[:, cols[p]] = (o[p][:, 0:LANES] / (o[p][:, LANES:] + sink[p])).astype(BF16)


def _attention(sinks, freq8, pos3, proj, kv, bsz, seq, q_col_block):
    n = proj.shape[0]
    blk = WINDOW
    nb = seq // blk
    expand, shift = _attn_constants()
    row = lambda b, i: b * nb + i
    full = lambda a: pl.BlockSpec(a.shape, lambda b, i: (0,) * a.ndim)
    est = (2 * (blk * ATT_WIDTH * 2 * 2 + blk * 2 * KV_WIDTH * 2)
           + 2 * ATT_KV_HEADS * 4 * blk * 2 * LANES * 2 + 64 * blk * 4 * blk * 4)
    return pl.pallas_call(
        _attn_kernel,
        grid=(bsz, nb),
        in_specs=[pl.BlockSpec(memory_space=pltpu.SMEM),
                  full(freq8), full(expand), full(shift),
                  pl.BlockSpec((None, 1, blk), lambda b, i: (row(b, i), 0, 0)),
                  pl.BlockSpec((blk, ATT_WIDTH), lambda b, i: (row(b, i), q_col_block)),
                  pl.BlockSpec((blk, 2 * KV_WIDTH), lambda b, i: (row(b, i), 0))],
        out_specs=pl.BlockSpec((blk, ATT_WIDTH), lambda b, i: (row(b, i), 0)),
        out_shape=jax.ShapeDtypeStruct((n, ATT_WIDTH), BF16),
        scratch_shapes=[pltpu.VMEM((ATT_KV_HEADS, 4 * blk, 2 * LANES), BF16),
                        pltpu.VMEM((ATT_KV_HEADS, 4 * blk, 2 * LANES), BF16)],
        compiler_params=pltpu.CompilerParams(
            dimension_semantics=("parallel", "arbitrary"), vmem_limit_bytes=_vmem_limit(est)),
        name="swa_attn",
    )(sinks, freq8, expand, shift, pos3, proj, kv)


def _hgrn_constants():
    t = np.arange(CHUNK)[:, None]
    j = np.arange(CHUNK)[None, :]
    tri = (j <= t).astype(np.float32)
    cumsum = np.concatenate([tri, tri], axis=1)
    reduce = np.zeros((DIAG, HG_DIM, LANES), np.float32)
    for d in range(DIAG):
        reduce[d, :, DIAG - 1 - d] = 1.0
    return jnp.asarray(cumsum, BF16), jnp.asarray(reduce.reshape(DIAG * HG_DIM, LANES), BF16)


def _hgrn_kernel(cm_ref, red_ref, lbp_ref, gn_ref, q_ref, f_ref, i_ref, g_ref, o_ref, st_ref, *,
                 tokens):
    @pl.when(pl.program_id(1) == 0)
    def _():
        st_ref[...] = jnp.zeros_like(st_ref)

    lbp = lbp_ref[...]
    e = jnp.exp(lbp - jnp.max(lbp, axis=0, keepdims=True))
    lb = e[0:1, :] / jnp.sum(e, axis=0, keepdims=True)
    half_key_scale = 0.5 * (1.0 - lb)

    row = lax.broadcasted_iota(jnp.int32, (CHUNK, CHUNK), 0)
    col = lax.broadcasted_iota(jnp.int32, (CHUNK, CHUNK), 1)
    same = {w: row // w == col // w for w in (DIAG, 2 * DIAG, 4 * DIAG)}
    groups = CHUNK // DIAG

    heads = [slice(h * HG_DIM, (h + 1) * HG_DIM) for h in range(HG_HEADS)]
    chunks = [slice(c * CHUNK, (c + 1) * CHUNK) for c in range(tokens // CHUNK)]
    per_chunk = []

    for rs in chunks:
        half_q = 0.5 * q_ref[rs, :].astype(F32)
        qs = half_q + half_q * jnp.tanh(half_q)
        key = half_key_scale - half_key_scale * jnp.tanh(0.5 * f_ref[rs, :].astype(F32))
        f = 1.0 - key
        val = i_ref[rs, :]

        b = _dot(cm_ref[...], jnp.concatenate(_split_bf16(jnp.log2(f), 2), axis=0).astype(BF16))
        eb = jnp.exp2(b)
        q_state = (qs * eb).astype(BF16)
        k_state = (key * jnp.exp2(b[CHUNK - 1:CHUNK, :] - b)).astype(BF16)
        decay = eb[CHUNK - 1:CHUNK, :]

        q_lv, k_lv = [], []
        for lv in LEVELS:
            qp, kp = [], []
            for piece in range(CHUNK // lv):
                ps = slice(piece * lv, (piece + 1) * lv)
                zero = jnp.zeros((lv, HG_WIDTH), F32)
                if piece % 2:
                    ref = b[piece * lv - 1:piece * lv, :]
                    qp.append(qs[ps] * jnp.exp2(b[ps] - ref)); kp.append(zero)
                else:
                    ref = b[(piece + 1) * lv - 1:(piece + 1) * lv, :]
                    qp.append(zero); kp.append(key[ps] * jnp.exp2(ref - b[ps]))
            q_lv.append(jnp.concatenate(qp, axis=0).astype(BF16))
            k_lv.append(jnp.concatenate(kp, axis=0).astype(BF16))

        carry = key.reshape(groups, DIAG, HG_WIDTH)
        f3 = f.reshape(groups, DIAG, HG_WIDTH)
        qs_bf16 = qs.astype(BF16)
        diag = [qs_bf16 * key.astype(BF16)]
        for d in range(1, DIAG):
            carry = f3 * pltpu.roll(carry, 1, axis=1)
            diag.append(qs_bf16 * carry.reshape(CHUNK, HG_WIDTH).astype(BF16))

        parts = [[_dot_nt(q_lv[n][:, hs], k_lv[n][:, hs]) for n in range(len(LEVELS))]
                 for hs in heads]
        r = _dot(jnp.concatenate([jnp.concatenate([dg[:, hs] for dg in diag], axis=1)
                                  for hs in heads], axis=0), red_ref[...])
        update = [_dot_tn(val[:, hs], k_state[:, hs]) for hs in heads]
        per_chunk.append((parts, r, update, q_state, val, decay))

    g_all = []
    for parts, r, _, _, _, _ in per_chunk:
        g_chunk = []
        for h in range(HG_HEADS):
            g_diag = pltpu.roll(r[h * CHUNK:(h + 1) * CHUNK], LANES - (DIAG - 1), axis=1,
                                stride=1, stride_axis=0)[:, 0:CHUNK]
            g_mat = jnp.where(same[DIAG], g_diag,
                              jnp.where(same[2 * DIAG], parts[h][2],
                                        jnp.where(same[4 * DIAG], parts[h][1], parts[h][0])))
            g_chunk.append(g_mat.astype(BF16))
        g_all.append(g_chunk)

    state = [st_ref[h] for h in range(HG_HEADS)]
    for rs, g_chunk, (_, _, update, q_state, val, decay) in zip(chunks, g_all, per_chunk):
        for h, hs in enumerate(heads):
            o = _dot_nt(q_state[:, hs], state[h].astype(BF16)) + _dot(g_chunk[h], val[:, hs])
            state[h] = state[h] * decay[:, hs] + update[h]
            y = _rms(o) * gn_ref[...]
            o_ref[rs, hs] = (y * _sigmoid(g_ref[rs, hs].astype(F32))).astype(BF16)
    for h in range(HG_HEADS):
        st_ref[h] = state[h]


def _hgrn(lb_params, gnorm, proj, bsz, seq, col_blocks):
    n = proj.shape[0]
    tokens = 4 * CHUNK
    nt = seq // tokens
    cmat, rmat = _hgrn_constants()
    row = lambda b, i: b * nt + i
    spec = lambda cb: pl.BlockSpec((tokens, HG_WIDTH), lambda b, i: (row(b, i), cb))
    full = lambda a: pl.BlockSpec(a.shape, lambda b, i: (0,) * a.ndim)
    est = 2 * 5 * tokens * HG_WIDTH * 2 + HG_HEADS * HG_DIM * HG_DIM * 4 + 40 * CHUNK * HG_WIDTH * 4
    return pl.pallas_call(
        functools.partial(_hgrn_kernel, tokens=tokens),
        grid=(bsz, nt),
        in_specs=[full(cmat), full(rmat), full(lb_params), full(gnorm),
                  spec(col_blocks[0]), spec(col_blocks[1]), spec(col_blocks[2]), spec(col_blocks[3])],
        out_specs=pl.BlockSpec((tokens, HG_WIDTH), lambda b, i: (row(b, i), 0)),
        out_shape=jax.ShapeDtypeStruct((n, HG_WIDTH), BF16),
        scratch_shapes=[pltpu.VMEM((HG_HEADS, HG_DIM, HG_DIM), F32)],
        compiler_params=pltpu.CompilerParams(
            dimension_semantics=("parallel", "arbitrary"), vmem_limit_bytes=_vmem_limit(est)),
        name="hgrn2",
    )(cmat, rmat, lb_params, gnorm, proj, proj, proj, proj)


def _mix_kernel(x_ref, mod_ref, g_ref, a_ref, hg_ref, ga_ref, gh_ref, wa_ref, wh_ref, wo_ref,
                o_ref, m_ref, *, tc):
    d = x_ref.shape[1]
    for c in range(d // tc):
        cs = slice(c * tc, (c + 1) * tc)
        ya = _dot(a_ref[...], wa_ref[:, cs])
        yh = _dot(hg_ref[...], wh_ref[:, cs])
        merged = (_sigmoid(ga_ref[:, cs].astype(F32)) * ya
                  + _sigmoid(gh_ref[:, cs].astype(F32)) * yh)
        m_ref[:, cs] = merged.astype(BF16)
    y = _dot(m_ref[...], wo_ref[...])
    o_ref[...] = x_ref[...] + mod_ref[2:3, :] * (_rms(y) * g_ref[...])


def _mix_out(x2, mod3, gain, attn, hg, proj, wa, wh, wo, seq, ga_block, gh_block):
    n, d = x2.shape
    tm, tc = 512, 512
    per_batch = seq // tm
    const = lambda shape: pl.BlockSpec(shape, lambda i: (0, 0), pipeline_mode=pl.Buffered(1))
    est = ((wa.size + wh.size + wo.size) * 2 + 2 * 2 * tm * d * 4 + 2 * 2 * tm * d * 2
           + 2 * 2 * tm * ATT_WIDTH * 2 + tm * d * 2 + 3 * tm * d * 4)
    return pl.pallas_call(
        functools.partial(_mix_kernel, tc=tc),
        grid=(n // tm,),
        in_specs=[pl.BlockSpec((tm, d), lambda i: (i, 0)),
                  pl.BlockSpec((None, N_MOD, d), lambda i: (i // per_batch, 0, 0)),
                  pl.BlockSpec((1, d), lambda i: (0, 0)),
                  pl.BlockSpec((tm, ATT_WIDTH), lambda i: (i, 0)),
                  pl.BlockSpec((tm, HG_WIDTH), lambda i: (i, 0)),
                  pl.BlockSpec((tm, d), lambda i: (i, ga_block)),
                  pl.BlockSpec((tm, d), lambda i: (i, gh_block)),
                  const(wa.shape), const(wh.shape), const(wo.shape)],
        out_specs=pl.BlockSpec((tm, d), lambda i: (i, 0)),
        out_shape=jax.ShapeDtypeStruct((n, d), F32),
        scratch_shapes=[pltpu.VMEM((tm, d), BF16)],
        compiler_params=pltpu.CompilerParams(
            dimension_semantics=("parallel",), vmem_limit_bytes=_vmem_limit(est)),
        name="mix_out",
    )(x2, mod3, gain, attn, hg, proj, proj, wa, wh, wo)


def _ffn_up_kernel(x_ref, mod_ref, gpre_ref, wg_ref, wu_ref, o_ref, h_even_ref, h_odd_ref, *,
                   slices):
    i, j = pl.program_id(0), pl.program_id(1)

    def modulated_norm(rows):
        y = _rms(x_ref[rows, :]) * gpre_ref[...]
        return (y * (1.0 + mod_ref[4:5, :]) + mod_ref[3:4, :]).astype(BF16)

    @pl.when((i == 0) & (j == 0))
    def _():
        h_even_ref[...] = modulated_norm(slice(None))

    def step(h_ref, h_next_ref):
        rows_per = x_ref.shape[0] // slices
        first = pl.multiple_of(jnp.clip(j - 1, 0, slices - 1) * rows_per, rows_per)
        rows = pl.ds(first, rows_per)
        h_next_ref[rows, :] = modulated_norm(rows)

        h = h_ref[...]
        g = _dot(h, wg_ref[...].astype(BF16))
        u = _dot(h, wu_ref[...].astype(BF16))
        o_ref[...] = (g * _sigmoid(g) * u).astype(BF16)

    pl.when(i % 2 == 0)(lambda: step(h_even_ref, h_odd_ref))
    pl.when(i % 2 == 1)(lambda: step(h_odd_ref, h_even_ref))


def _ffn_down_kernel(a_ref, x_ref, mod_ref, gpost_ref, wd_ref, o_ref):
    y = _dot(a_ref[...], wd_ref[...])
    o_ref[...] = x_ref[...] + mod_ref[5:6, :] * (_rms(y) * gpost_ref[...])


def _ffn(x1, mod3, gpre, gpost, w_in, w_down, seq):
    n, d = x1.shape
    hidden = w_down.shape[0]

    tm, th = 1024, 512
    nh = hidden // th
    per_batch = seq // tm
    last = n // tm - 1
    x_tile = lambda i, j: jnp.minimum(i + jnp.where(j > 0, 1, 0), last)
    est = (2 * tm * d * 4 + 2 * tm * d * 2 + 2 * 2 * d * th * 4 + 2 * d * th * 2 + 2 * tm * th * 2
           + 4 * tm * th * 4)
    act = pl.pallas_call(
        functools.partial(_ffn_up_kernel, slices=8),
        grid=(n // tm, nh),
        in_specs=[pl.BlockSpec((tm, d), lambda i, j: (x_tile(i, j), 0)),
                  pl.BlockSpec((None, N_MOD, d), lambda i, j: (x_tile(i, j) // per_batch, 0, 0)),
                  pl.BlockSpec((1, d), lambda i, j: (0, 0)),
                  pl.BlockSpec((d, th), lambda i, j: (0, j)),
                  pl.BlockSpec((d, th), lambda i, j: (0, nh + j))],
        out_specs=pl.BlockSpec((tm, th), lambda i, j: (i, j)),
        out_shape=jax.ShapeDtypeStruct((n, hidden), BF16),
        scratch_shapes=[pltpu.VMEM((tm, d), BF16), pltpu.VMEM((tm, d), BF16)],
        compiler_params=pltpu.CompilerParams(
            dimension_semantics=("arbitrary", "arbitrary"), vmem_limit_bytes=_vmem_limit(est)),
        name="ffn_up",
    )(x1, mod3, gpre, w_in, w_in)

    tm = 256
    per_batch = seq // tm
    est = hidden * d * 2 + 2 * tm * hidden * 2 + 2 * 2 * tm * d * 4 + 2 * tm * d * 4
    return pl.pallas_call(
        _ffn_down_kernel,
        grid=(n // tm,),
        in_specs=[pl.BlockSpec((tm, hidden), lambda i: (i, 0)),
                  pl.BlockSpec((tm, d), lambda i: (i, 0)),
                  pl.BlockSpec((None, N_MOD, d), lambda i: (i // per_batch, 0, 0)),
                  pl.BlockSpec((1, d), lambda i: (0, 0)),
                  pl.BlockSpec((hidden, d), lambda i: (0, 0), pipeline_mode=pl.Buffered(1))],
        out_specs=pl.BlockSpec((tm, d), lambda i: (i, 0)),
        out_shape=jax.ShapeDtypeStruct((n, d), F32),
        compiler_params=pltpu.CompilerParams(
            dimension_semantics=("parallel",), vmem_limit_bytes=_vmem_limit(est)),
        name="ffn_down",
    )(act, x1, mod3, gpost, w_down)


def kernel(x, c, positions, w_ada, b_ada, g_pre_mix, g_post_mix, g_pre_ffn, g_post_ffn, w_in,
           attn_sinks, w_attn_proj, hg_lower_bounds, hg_norm, w_hgrn_proj, w_out, w_ffn_in,
           w_ffn_out):
    bsz, seq, d = x.shape
    n = bsz * seq
    assert d == D_MODEL and w_ada.shape[0] == 1 and seq % (2 * CHUNK) == 0 and seq % 1024 == 0

    ga_block, gh_block = 0, 1
    q_block = 2 * D_MODEL // ATT_WIDTH
    hg_blocks = tuple((2 * D_MODEL + ATT_WIDTH) // HG_WIDTH + k for k in range(4))

    x2 = x.reshape(n, d)
    mod3 = _adaln(c, w_ada[0], b_ada[0]).reshape(bsz, N_MOD, d)

    proj, kv = _in_proj(x2, mod3, g_pre_mix, w_in[0].astype(BF16), seq)

    inv_freq = ROPE_THETA ** (-jnp.arange(0, ROT_DIM, 2, dtype=F32) / ROT_DIM)
    freq8 = jnp.broadcast_to(inv_freq[:, None], (ROT_HALF, WINDOW))
    pos3 = positions.astype(F32).reshape(n // WINDOW, 1, WINDOW)
    attn = _attention(attn_sinks[0], freq8, pos3, proj, kv, bsz, seq, q_block)

    hg = _hgrn(hg_lower_bounds, hg_norm, proj, bsz, seq, hg_blocks)

    x1 = _mix_out(x2, mod3, g_post_mix, attn, hg, proj, w_attn_proj[0].astype(BF16),
                  w_hgrn_proj[0].astype(BF16), w_out[0].astype(BF16), seq, ga_block, gh_block)

    out = _ffn(x1, mod3, g_pre_ffn, g_post_ffn, w_ffn_in[0], w_ffn_out[0].astype(BF16), seq)
    return out.reshape(bsz, seq, d)
```

```python
import functools
import math

import numpy as np
import jax
import jax.numpy as jnp
from jax import lax
from jax.experimental import pallas as pl
from jax.experimental.pallas import tpu as pltpu

F32 = jnp.float32
BF16 = jnp.bfloat16

D_MODEL = 2048
ATT_HEADS = 16
ATT_KV_HEADS = 2
HEAD_DIM = 64
ATT_WIDTH = ATT_HEADS * HEAD_DIM
KV_WIDTH = ATT_KV_HEADS * HEAD_DIM
WINDOW = 128
ROT_DIM = HEAD_DIM // 4
ROT_HALF = ROT_DIM // 2
ROPE_THETA = 500000.0
HG_HEADS = 8
HG_DIM = 128
HG_WIDTH = HG_HEADS * HG_DIM
N_MOD = 6
EPS = 1e-6
LOG2E = math.log2(math.e)
MASK_BIAS = -1e30

LANES = 128
SUBLANES = 8
V7X_VMEM_BYTES = 64 * 1024 * 1024

CHUNK = 64
DIAG = SUBLANES
LEVELS = (32, 16, 8)


def _vmem_limit(estimate_bytes):
    return int(min(estimate_bytes * 3 // 2, V7X_VMEM_BYTES - 8 * 1024 * 1024))


def _dot(a, b):
    return jnp.dot(a, b, preferred_element_type=F32)


def _dot_nt(a, b):
    return lax.dot_general(a, b, (((1,), (1,)), ((), ())), preferred_element_type=F32)


def _dot_tn(a, b):
    return lax.dot_general(a, b, (((0,), (0,)), ((), ())), preferred_element_type=F32)


def _rms(t):
    return t * lax.rsqrt(jnp.mean(t * t, axis=-1, keepdims=True) + EPS)


def _sigmoid(t):
    return 0.5 + 0.5 * jnp.tanh(0.5 * t)


def _split_bf16(x, parts):
    out = []
    for _ in range(parts - 1):
        p = x.astype(BF16).astype(F32)
        out.append(p)
        x = x - p
    out.append(x.astype(BF16).astype(F32))
    return out


def _adaln_kernel(c_ref, w_ref, b_ref, o_ref):
    bsz = c_ref.shape[0]
    c_hi, c_lo = _split_bf16(c_ref[...], 2)
    w_hi, w_lo = _split_bf16(w_ref[...], 2)
    by_hi = _dot(jnp.concatenate([c_hi, c_lo], axis=0).astype(BF16), w_hi.astype(BF16))
    o_ref[...] = (by_hi[0:bsz] + by_hi[bsz:] + _dot(c_hi.astype(BF16), w_lo.astype(BF16))
                  + b_ref[...])


def _adaln(c, w, b):
    bsz, d = c.shape
    n = w.shape[1]
    tn = 1024
    est = 2 * (d * tn * 4) + 2 * bsz * d * 4 + 4 * bsz * tn * 4 + 3 * d * tn * 4
    return pl.pallas_call(
        _adaln_kernel,
        grid=(n // tn,),
        in_specs=[pl.BlockSpec((bsz, d), lambda j: (0, 0)),
                  pl.BlockSpec((d, tn), lambda j: (0, j)),
                  pl.BlockSpec((1, tn), lambda j: (0, j))],
        out_specs=pl.BlockSpec((bsz, tn), lambda j: (0, j)),
        out_shape=jax.ShapeDtypeStruct((bsz, n), F32),
        compiler_params=pltpu.CompilerParams(
            dimension_semantics=("arbitrary",), vmem_limit_bytes=_vmem_limit(est)),
        name="adaln_mod",
    )(c, w, b.reshape(1, n))


def _in_proj_kernel(x_ref, mod_ref, g_ref, w_ref, wkv_ref, o_ref, okv_ref, h_ref):
    @pl.when(pl.program_id(1) == 0)
    def _():
        y = _rms(x_ref[...]) * g_ref[...]
        h = y * (1.0 + mod_ref[1:2, :]) + mod_ref[0:1, :]
        h_ref[...] = h.astype(BF16)
        okv_ref[...] = _dot(h_ref[...], wkv_ref[...]).astype(BF16)

    o_ref[...] = _dot(h_ref[...], w_ref[...]).astype(BF16)


def _in_proj(x2, mod3, gain, w_in, seq):
    n, d = x2.shape
    tm, tn = 1024, 1024
    per_batch = seq // tm
    o_kv = ATT_WIDTH
    o_qh = o_kv + 2 * KV_WIDTH
    o_ga = o_qh + 4 * HG_WIDTH
    cols = w_in.shape[1] - 2 * KV_WIDTH
    n_gate = 2 * D_MODEL // tn
    n_q = ATT_WIDTH // tn

    def src_col(j):
        unit = tn // LANES
        lane_tile = jnp.where(j < n_gate, o_ga // LANES + unit * j,
                              jnp.where(j < n_gate + n_q, unit * (j - n_gate),
                                        o_qh // LANES + unit * (j - n_gate - n_q)))
        return lane_tile * LANES

    est = (2 * tm * d * 4 + tm * d * 2 + 2 * d * tn * 2 + 2 * tm * tn * 2
           + 2 * d * KV_WIDTH * 2 * 2 + 2 * tm * 2 * KV_WIDTH * 2 + tm * tn * 4 + tm * d * 4)
    return pl.pallas_call(
        _in_proj_kernel,
        grid=(n // tm, cols // tn),
        in_specs=[pl.BlockSpec((tm, d), lambda i, j: (i, 0)),
                  pl.BlockSpec((None, N_MOD, d), lambda i, j: (i // per_batch, 0, 0)),
                  pl.BlockSpec((1, d), lambda i, j: (0, 0)),
                  pl.BlockSpec((pl.Element(d), pl.Element(tn)), lambda i, j: (0, src_col(j))),
                  pl.BlockSpec((d, 2 * KV_WIDTH), lambda i, j: (0, o_kv // (2 * KV_WIDTH)))],
        out_specs=[pl.BlockSpec((tm, tn), lambda i, j: (i, j)),
                   pl.BlockSpec((tm, 2 * KV_WIDTH), lambda i, j: (i, 0))],
        out_shape=[jax.ShapeDtypeStruct((n, cols), BF16),
                   jax.ShapeDtypeStruct((n, 2 * KV_WIDTH), BF16)],
        scratch_shapes=[pltpu.VMEM((tm, d), BF16)],
        compiler_params=pltpu.CompilerParams(
            dimension_semantics=("parallel", "arbitrary"), vmem_limit_bytes=_vmem_limit(est)),
        name="in_proj",
    )(x2, mod3, gain, w_in, w_in)


def _attn_constants():
    lane = np.arange(LANES)
    dim = lane % HEAD_DIM
    expand = np.zeros((2, 3, ROT_HALF, 3, LANES), np.float32)
    for j in range(ROT_HALF):
        expand[0, :, j, 0, (dim < ROT_DIM) & (dim % ROT_HALF == j)] = 1.0
        expand[1, :, j, 1, (dim < ROT_HALF) & (dim == j)] = -1.0
        expand[1, :, j, 2, (dim >= ROT_HALF) & (dim < ROT_DIM) & (dim - ROT_HALF == j)] = 1.0
    expand = expand.reshape(6 * ROT_HALF, 3 * LANES)
    shift = np.zeros((LANES, 2 * LANES), np.float32)
    for l in range(LANES):
        if l + ROT_HALF < LANES:
            shift[l + ROT_HALF, l] = 1.0
        if l - ROT_HALF >= 0:
            shift[l - ROT_HALF, LANES + l] = 1.0
    return jnp.asarray(expand, BF16), jnp.asarray(shift, BF16)


def _attn_kernel(sink_ref, freq_ref, expand_ref, shift_ref, pos_ref, q_ref, kv_ref, o_ref,
                 kb_ref, vb_ref):
    members = q_ref.shape[0]
    nblk = pl.program_id(1)
    parity = nblk % 2
    blk = WINDOW
    lane = lax.broadcasted_iota(jnp.int32, (blk, LANES), 1)
    lo = lane < HEAD_DIM
    dim = lane % HEAD_DIM
    kj = lax.broadcasted_iota(jnp.int32, (blk, blk), 0)
    qi = lax.broadcasted_iota(jnp.int32, (blk, blk), 1)

    @pl.when(nblk == 0)
    def _():
        kb_ref[...] = jnp.zeros_like(kb_ref)
        vb_ref[...] = jnp.zeros_like(vb_ref)

    def rope(x_bf16, tables):
        t_cos, t_lo, t_hi = tables
        ud = _dot(x_bf16, shift_ref[...])
        return x_bf16.astype(F32) * t_cos + ud[:, 0:LANES] * t_lo + ud[:, LANES:] * t_hi

    k_tabs, q_tabs = [], []
    for g in range(members):
        ang = freq_ref[...] * pos_ref[g]
        parts = _split_bf16(jnp.cos(ang), 3) + _split_bf16(jnp.sin(ang), 3)
        tabs = _dot_tn(jnp.concatenate(parts, axis=0).astype(BF16), expand_ref[...])
        kt = (tabs[:, 0:LANES] + jnp.where(dim < ROT_DIM, 0.0, 1.0),
              tabs[:, LANES:2 * LANES], tabs[:, 2 * LANES:3 * LANES])
        k_tabs.append(kt)
        q_tabs.append(tuple(t * (HEAD_DIM ** -0.5 * LOG2E) for t in kt))

    ones_lo = jnp.where(lo, 1.0, 0.0).astype(BF16)
    ones_hi = jnp.where(lo, 0.0, 1.0).astype(BF16)
    bias = []
    for half in range(2):
        is_cur = parity == half
        valid = (is_cur & (kj <= qi)) | ((~is_cur) & (nblk > 0) & (kj > qi))
        bias.append(jnp.where(valid, 0.0, MASK_BIAS).astype(BF16))
    for g in range(members):
        k = rope(kv_ref[g, :, 0:KV_WIDTH], k_tabs[g])
        v = kv_ref[g, :, KV_WIDTH:].astype(F32)
        k_sw = pltpu.roll(k, HEAD_DIM, axis=1)
        v_sw = pltpu.roll(v, HEAD_DIM, axis=1)
        for kvh in range(ATT_KV_HEADS):
            for slot in range(2):
                ks = (k, k_sw)[kvh ^ slot]
                vs = (v, v_sw)[kvh ^ slot]
                keep = lo if slot == 0 else ~lo
                rows = pl.ds(pl.multiple_of((2 * slot + parity) * blk, blk), blk)
                kb_ref[g, kvh, rows, 0:LANES] = jnp.where(keep, ks, 0.0).astype(BF16)
                vb_ref[g, kvh, rows, 0:LANES] = jnp.where(keep, vs, 0.0).astype(BF16)
                for half in range(2):
                    rows = slice((2 * slot + half) * blk, (2 * slot + half + 1) * blk)
                    kb_ref[g, kvh, rows, LANES:] = bias[half]
                    vb_ref[g, kvh, rows, LANES:] = ones_hi if slot else ones_lo

    eye = jnp.where(kj == qi, 1.0, 0.0).astype(BF16)
    pairs_per_kv = ATT_HEADS // ATT_KV_HEADS // 2
    work = [(g, p) for g in range(members) for p in range(ATT_HEADS // 2)]
    cols = [slice(p * LANES, (p + 1) * LANES) for p in range(ATT_HEADS // 2)]
    q = [rope(q_ref[g, :, cols[p]], q_tabs[g]).astype(BF16) for g, p in work]
    s = [_dot_nt(jnp.concatenate([q[w], eye], axis=1), kb_ref[g, p // pairs_per_kv])
         for w, (g, p) in enumerate(work)]
    probs, sink = [], []
    for w, (g, p) in enumerate(work):
        pe, se = [], []
        for e in range(2):
            logits = s[w][:, e * 2 * blk:(e + 1) * 2 * blk]
            sink2 = sink_ref[2 * p + e] * LOG2E
            m = jnp.maximum(jnp.max(logits, axis=-1, keepdims=True), sink2)
            pe.append(jnp.exp2(logits - m).astype(BF16))
            se.append(jnp.exp2(sink2 - m))
        probs.append(jnp.concatenate(pe, axis=1))
        sink.append(jnp.where(lo, se[0], se[1]))
    o = [_dot(probs[w], vb_ref[g, p // pairs_per_kv])
         for w, (g, p) in enumerate(work)]
    for w, (g, p) in enumerate(work):
        o_ref[g, :, cols[p]] = (o[w][:, 0:LANES] / (o[w][:, LANES:] + sink[w])).astype(BF16)


def _attention(sinks, freq8, pos4, proj3, kv3, q_col_block):
    bsz, seq = proj3.shape[0], proj3.shape[1]
    blk = WINDOW
    members = 4
    expand, shift = _attn_constants()
    full = lambda a: pl.BlockSpec(a.shape, lambda b, i: (0,) * a.ndim)
    est = (2 * members * (blk * ATT_WIDTH * 2 * 2 + blk * 2 * KV_WIDTH * 2)
           + 2 * members * ATT_KV_HEADS * 4 * blk * 2 * LANES * 2 + members * 64 * blk * 4 * blk * 4)
    return pl.pallas_call(
        _attn_kernel,
        grid=(bsz // members, seq // blk),
        in_specs=[pl.BlockSpec(memory_space=pltpu.SMEM),
                  full(freq8), full(expand), full(shift),
                  pl.BlockSpec((members, None, 1, blk), lambda b, i: (b, i, 0, 0)),
                  pl.BlockSpec((members, blk, ATT_WIDTH), lambda b, i: (b, i, q_col_block)),
                  pl.BlockSpec((members, blk, 2 * KV_WIDTH), lambda b, i: (b, i, 0))],
        out_specs=pl.BlockSpec((members, blk, ATT_WIDTH), lambda b, i: (b, i, 0)),
        out_shape=jax.ShapeDtypeStruct((bsz, seq, ATT_WIDTH), BF16),
        scratch_shapes=[pltpu.VMEM((members, ATT_KV_HEADS, 4 * blk, 2 * LANES), BF16),
                        pltpu.VMEM((members, ATT_KV_HEADS, 4 * blk, 2 * LANES), BF16)],
        compiler_params=pltpu.CompilerParams(
            dimension_semantics=("parallel", "arbitrary"), vmem_limit_bytes=_vmem_limit(est)),
        name="swa_attn",
    )(sinks, freq8, expand, shift, pos4, proj3, kv3)


def _hgrn_constants():
    t = np.arange(CHUNK)[:, None]
    j = np.arange(CHUNK)[None, :]
    tri = (j <= t).astype(np.float32)
    cumsum = np.concatenate([tri, tri], axis=1)
    reduce = np.zeros((DIAG, HG_DIM, LANES), np.float32)
    for d in range(DIAG):
        reduce[d, :, DIAG - 1 - d] = 1.0
    return jnp.asarray(cumsum, BF16), jnp.asarray(reduce.reshape(DIAG * HG_DIM, LANES), BF16)


def _hgrn_kernel(cm_ref, red_ref, lbp_ref, gn_ref, q_ref, f_ref, i_ref, g_ref, o_ref, st_ref, *,
                 tokens):
    @pl.when(pl.program_id(1) == 0)
    def _():
        st_ref[...] = jnp.zeros_like(st_ref)

    lbp = lbp_ref[...]
    e = jnp.exp(lbp - jnp.max(lbp, axis=0, keepdims=True))
    lb = e[0:1, :] / jnp.sum(e, axis=0, keepdims=True)
    half_key_scale = 0.5 * (1.0 - lb)

    row = lax.broadcasted_iota(jnp.int32, (CHUNK, CHUNK), 0)
    col = lax.broadcasted_iota(jnp.int32, (CHUNK, CHUNK), 1)
    same = {w: row // w == col // w for w in (DIAG, 2 * DIAG, 4 * DIAG)}
    groups = CHUNK // DIAG

    heads = [slice(h * HG_DIM, (h + 1) * HG_DIM) for h in range(HG_HEADS)]
    chunks = [slice(c * CHUNK, (c + 1) * CHUNK) for c in range(tokens // CHUNK)]
    per_chunk = []

    for rs in chunks:
        half_q = 0.5 * q_ref[rs, :].astype(F32)
        qs = half_q + half_q * jnp.tanh(half_q)
        key = half_key_scale - half_key_scale * jnp.tanh(0.5 * f_ref[rs, :].astype(F32))
        f = 1.0 - key
        val = i_ref[rs, :]

        b = _dot(cm_ref[...], jnp.concatenate(_split_bf16(jnp.log2(f), 2), axis=0).astype(BF16))
        eb = jnp.exp2(b)
        q_state = (qs * eb).astype(BF16)
        k_state = (key * jnp.exp2(b[CHUNK - 1:CHUNK, :] - b)).astype(BF16)
        decay = eb[CHUNK - 1:CHUNK, :]

        q_lv, k_lv = [], []
        for lv in LEVELS:
            qp, kp = [], []
            for piece in range(CHUNK // lv):
                ps = slice(piece * lv, (piece + 1) * lv)
                zero = jnp.zeros((lv, HG_WIDTH), F32)
                if piece % 2:
                    ref = b[piece * lv - 1:piece * lv, :]
                    qp.append(qs[ps] * jnp.exp2(b[ps] - ref)); kp.append(zero)
                else:
                    ref = b[(piece + 1) * lv - 1:(piece + 1) * lv, :]
                    qp.append(zero); kp.append(key[ps] * jnp.exp2(ref - b[ps]))
            q_lv.append(jnp.concatenate(qp, axis=0).astype(BF16))
            k_lv.append(jnp.concatenate(kp, axis=0).astype(BF16))

        carry = key.reshape(groups, DIAG, HG_WIDTH)
        f3 = f.reshape(groups, DIAG, HG_WIDTH)
        qs_bf16 = qs.astype(BF16)
        diag = [qs_bf16 * key.astype(BF16)]
        for d in range(1, DIAG):
            carry = f3 * pltpu.roll(carry, 1, axis=1)
            diag.append(qs_bf16 * carry.reshape(CHUNK, HG_WIDTH).astype(BF16))

        parts = [[_dot_nt(q_lv[n][:, hs], k_lv[n][:, hs]) for n in range(len(LEVELS))]
                 for hs in heads]
        r = _dot(jnp.concatenate([jnp.concatenate([dg[:, hs] for dg in diag], axis=1)
                                  for hs in heads], axis=0), red_ref[...])
        update = [_dot_tn(val[:, hs], k_state[:, hs]) for hs in heads]
        per_chunk.append((parts, r, update, q_state, val, decay))

    g_all = []
    for parts, r, _, _, _, _ in per_chunk:
        g_chunk = []
        for h in range(HG_HEADS):
            g_diag = pltpu.roll(r[h * CHUNK:(h + 1) * CHUNK], LANES - (DIAG - 1), axis=1,
                                stride=1, stride_axis=0)[:, 0:CHUNK]
            g_mat = jnp.where(same[DIAG], g_diag,
                              jnp.where(same[2 * DIAG], parts[h][2],
                                        jnp.where(same[4 * DIAG], parts[h][1], parts[h][0])))
            g_chunk.append(g_mat.astype(BF16))
        g_all.append(g_chunk)

    state = [st_ref[h] for h in range(HG_HEADS)]
    for rs, g_chunk, (_, _, update, q_state, val, decay) in zip(chunks, g_all, per_chunk):
        for h, hs in enumerate(heads):
            o = _dot_nt(q_state[:, hs], state[h].astype(BF16)) + _dot(g_chunk[h], val[:, hs])
            state[h] = state[h] * decay[:, hs] + update[h]
            y = _rms(o) * gn_ref[...]
            o_ref[rs, hs] = (y * _sigmoid(g_ref[rs, hs].astype(F32))).astype(BF16)
    for h in range(HG_HEADS):
        st_ref[h] = state[h]


def _hgrn(lb_params, gnorm, proj, bsz, seq, col_blocks):
    n = proj.shape[0]
    tokens = 8 * CHUNK
    nt = seq // tokens
    cmat, rmat = _hgrn_constants()
    row = lambda b, i: b * nt + i
    spec = lambda cb: pl.BlockSpec((tokens, HG_WIDTH), lambda b, i: (row(b, i), cb))
    full = lambda a: pl.BlockSpec(a.shape, lambda b, i: (0,) * a.ndim)
    est = 2 * 5 * tokens * HG_WIDTH * 2 + HG_HEADS * HG_DIM * HG_DIM * 4 + 40 * CHUNK * HG_WIDTH * 4
    return pl.pallas_call(
        functools.partial(_hgrn_kernel, tokens=tokens),
        grid=(bsz, nt),
        in_specs=[full(cmat), full(rmat), full(lb_params), full(gnorm),
                  spec(col_blocks[0]), spec(col_blocks[1]), spec(col_blocks[2]), spec(col_blocks[3])],
        out_specs=pl.BlockSpec((tokens, HG_WIDTH), lambda b, i: (row(b, i), 0)),
        out_shape=jax.ShapeDtypeStruct((n, HG_WIDTH), BF16),
        scratch_shapes=[pltpu.VMEM((HG_HEADS, HG_DIM, HG_DIM), F32)],
        compiler_params=pltpu.CompilerParams(
            dimension_semantics=("parallel", "arbitrary"), vmem_limit_bytes=_vmem_limit(est)),
        name="hgrn2",
    )(cmat, rmat, lb_params, gnorm, proj, proj, proj, proj)


def _mix_kernel(x_ref, mod_ref, g_ref, a_ref, hg_ref, ga_ref, gh_ref, wa_ref, wh_ref, wo_ref,
                o_ref, m_ref, *, tc):
    d = x_ref.shape[1]
    for c in range(d // tc):
        cs = slice(c * tc, (c + 1) * tc)
        ya = _dot(a_ref[...], wa_ref[:, cs])
        yh = _dot(hg_ref[...], wh_ref[:, cs])
        merged = (_sigmoid(ga_ref[:, cs].astype(F32)) * ya
                  + _sigmoid(gh_ref[:, cs].astype(F32)) * yh)
        m_ref[:, cs] = merged.astype(BF16)
    y = _dot(m_ref[...], wo_ref[...])
    o_ref[...] = x_ref[...] + mod_ref[2:3, :] * (_rms(y) * g_ref[...])


def _mix_out(x2, mod3, gain, attn, hg, proj, wa, wh, wo, seq, ga_block, gh_block):
    n, d = x2.shape
    tm, tc = 512, 512
    per_batch = seq // tm
    const = lambda shape: pl.BlockSpec(shape, lambda i: (0, 0), pipeline_mode=pl.Buffered(1))
    est = ((wa.size + wh.size + wo.size) * 2 + 2 * 2 * tm * d * 4 + 2 * 2 * tm * d * 2
           + 2 * 2 * tm * ATT_WIDTH * 2 + tm * d * 2 + 3 * tm * d * 4)
    return pl.pallas_call(
        functools.partial(_mix_kernel, tc=tc),
        grid=(n // tm,),
        in_specs=[pl.BlockSpec((tm, d), lambda i: (i, 0)),
                  pl.BlockSpec((None, N_MOD, d), lambda i: (i // per_batch, 0, 0)),
                  pl.BlockSpec((1, d), lambda i: (0, 0)),
                  pl.BlockSpec((tm, ATT_WIDTH), lambda i: (i, 0)),
                  pl.BlockSpec((tm, HG_WIDTH), lambda i: (i, 0)),
                  pl.BlockSpec((tm, d), lambda i: (i, ga_block)),
                  pl.BlockSpec((tm, d), lambda i: (i, gh_block)),
                  const(wa.shape), const(wh.shape), const(wo.shape)],
        out_specs=pl.BlockSpec((tm, d), lambda i: (i, 0)),
        out_shape=jax.ShapeDtypeStruct((n, d), F32),
        scratch_shapes=[pltpu.VMEM((tm, d), BF16)],
        compiler_params=pltpu.CompilerParams(
            dimension_semantics=("parallel",), vmem_limit_bytes=_vmem_limit(est)),
        name="mix_out",
    )(x2, mod3, gain, attn, hg, proj, proj, wa, wh, wo)


def _ffn_up_kernel(x_ref, mod_ref, gpre_ref, wg_ref, wu_ref, o_ref, h_even_ref, h_odd_ref, *,
                   slices):
    i, j = pl.program_id(0), pl.program_id(1)

    def modulated_norm(rows):
        y = _rms(x_ref[rows, :]) * gpre_ref[...]
        return (y * (1.0 + mod_ref[4:5, :]) + mod_ref[3:4, :]).astype(BF16)

    @pl.when((i == 0) & (j == 0))
    def _():
        h_even_ref[...] = modulated_norm(slice(None))

    def step(h_ref, h_next_ref):
        rows_per = x_ref.shape[0] // slices
        first = pl.multiple_of(jnp.clip(j - 1, 0, slices - 1) * rows_per, rows_per)
        rows = pl.ds(first, rows_per)
        h_next_ref[rows, :] = modulated_norm(rows)

        h = h_ref[...]
        g = _dot(h, wg_ref[...].astype(BF16))
        u = _dot(h, wu_ref[...].astype(BF16))
        o_ref[...] = (g * _sigmoid(g) * u).astype(BF16)

    pl.when(i % 2 == 0)(lambda: step(h_even_ref, h_odd_ref))
    pl.when(i % 2 == 1)(lambda: step(h_odd_ref, h_even_ref))


def _ffn_down_kernel(a_ref, x_ref, mod_ref, gpost_ref, wd_ref, o_ref):
    y = _dot(a_ref[...], wd_ref[...])
    o_ref[...] = x_ref[...] + mod_ref[5:6, :] * (_rms(y) * gpost_ref[...])


def _ffn(x1, mod3, gpre, gpost, w_in, w_down, seq):
    n, d = x1.shape
    hidden = w_down.shape[0]

    tm, th = 1024, 512
    nh = hidden // th
    per_batch = seq // tm
    last = n // tm - 1
    x_tile = lambda i, j: jnp.minimum(i + jnp.where(j > 0, 1, 0), last)
    est = (2 * tm * d * 4 + 2 * tm * d * 2 + 2 * 2 * d * th * 4 + 2 * d * th * 2 + 2 * tm * th * 2
           + 4 * tm * th * 4)
    act = pl.pallas_call(
        functools.partial(_ffn_up_kernel, slices=8),
        grid=(n // tm, nh),
        in_specs=[pl.BlockSpec((tm, d), lambda i, j: (x_tile(i, j), 0)),
                  pl.BlockSpec((None, N_MOD, d), lambda i, j: (x_tile(i, j) // per_batch, 0, 0)),
                  pl.BlockSpec((1, d), lambda i, j: (0, 0)),
                  pl.BlockSpec((d, th), lambda i, j: (0, j)),
                  pl.BlockSpec((d, th), lambda i, j: (0, nh + j))],
        out_specs=pl.BlockSpec((tm, th), lambda i, j: (i, j)),
        out_shape=jax.ShapeDtypeStruct((n, hidden), BF16),
        scratch_shapes=[pltpu.VMEM((tm, d), BF16), pltpu.VMEM((tm, d), BF16)],
        compiler_params=pltpu.CompilerParams(
            dimension_semantics=("arbitrary", "arbitrary"), vmem_limit_bytes=_vmem_limit(est)),
        name="ffn_up",
    )(x1, mod3, gpre, w_in, w_in)

    tm = 256
    per_batch = seq // tm
    est = hidden * d * 2 + 2 * tm * hidden * 2 + 2 * 2 * tm * d * 4 + 2 * tm * d * 4
    return pl.pallas_call(
        _ffn_down_kernel,
        grid=(n // tm,),
        in_specs=[pl.BlockSpec((tm, hidden), lambda i: (i, 0)),
                  pl.BlockSpec((tm, d), lambda i: (i, 0)),
                  pl.BlockSpec((None, N_MOD, d), lambda i: (i // per_batch, 0, 0)),
                  pl.BlockSpec((1, d), lambda i: (0, 0)),
                  pl.BlockSpec((hidden, d), lambda i: (0, 0), pipeline_mode=pl.Buffered(1))],
        out_specs=pl.BlockSpec((tm, d), lambda i: (i, 0)),
        out_shape=jax.ShapeDtypeStruct((n, d), F32),
        compiler_params=pltpu.CompilerParams(
            dimension_semantics=("parallel",), vmem_limit_bytes=_vmem_limit(est)),
        name="ffn_down",
    )(act, x1, mod3, gpost, w_down)


def kernel(x, c, positions, w_ada, b_ada, g_pre_mix, g_post_mix, g_pre_ffn, g_post_ffn, w_in,
           attn_sinks, w_attn_proj, hg_lower_bounds, hg_norm, w_hgrn_proj, w_out, w_ffn_in,
           w_ffn_out):
    bsz, seq, d = x.shape
    n = bsz * seq
    assert d == D_MODEL and w_ada.shape[0] == 1 and seq % 1024 == 0 and bsz % 4 == 0

    ga_block, gh_block = 0, 1
    q_block = 2 * D_MODEL // ATT_WIDTH
    hg_blocks = tuple((2 * D_MODEL + ATT_WIDTH) // HG_WIDTH + k for k in range(4))

    x2 = x.reshape(n, d)
    mod3 = _adaln(c, w_ada[0], b_ada[0]).reshape(bsz, N_MOD, d)

    proj, kv = _in_proj(x2, mod3, g_pre_mix, w_in[0].astype(BF16), seq)

    inv_freq = ROPE_THETA ** (-jnp.arange(0, ROT_DIM, 2, dtype=F32) / ROT_DIM)
    freq8 = jnp.broadcast_to(inv_freq[:, None], (ROT_HALF, WINDOW))
    pos4 = positions.astype(F32).reshape(bsz, seq // WINDOW, 1, WINDOW)
    attn = _attention(attn_sinks[0], freq8, pos4, proj.reshape(bsz, seq, -1),
                      kv.reshape(bsz, seq, -1), q_block).reshape(n, ATT_WIDTH)

    hg = _hgrn(hg_lower_bounds, hg_norm, proj, bsz, seq, hg_blocks)

    x1 = _mix_out(x2, mod3, g_post_mix, attn, hg, proj, w_attn_proj[0].astype(BF16),
                  w_hgrn_proj[0].astype(BF16), w_out[0].astype(BF16), seq, ga_block, gh_block)

    out = _ffn(x1, mod3, g_pre_ffn, g_post_ffn, w_ffn_in[0], w_ffn_out[0].astype(BF16), seq)
    return out.reshape(bsz, seq, d)
```

```python
import functools
import math

import numpy as np
import jax
import jax.numpy as jnp
from jax import lax
from jax.experimental import pallas as pl
from jax.experimental.pallas import tpu as pltpu

F32 = jnp.float32
BF16 = jnp.bfloat16

D_MODEL = 2048
ATT_HEADS = 16
ATT_KV_HEADS = 2
HEAD_DIM = 64
ATT_WIDTH = ATT_HEADS * HEAD_DIM
KV_WIDTH = ATT_KV_HEADS * HEAD_DIM
WINDOW = 128
ROT_DIM = HEAD_DIM // 4
ROT_HALF = ROT_DIM // 2
ROPE_THETA = 500000.0
HG_HEADS = 8
HG_DIM = 128
HG_WIDTH = HG_HEADS * HG_DIM
N_MOD = 6
EPS = 1e-6
LOG2E = math.log2(math.e)
MASK_BIAS = -1e30

LANES = 128
SUBLANES = 8
V7X_VMEM_BYTES = 64 * 1024 * 1024

CHUNK = 64
DIAG = SUBLANES
LEVELS = (32, 16, 8)


def _vmem_limit(estimate_bytes):
    return int(min(estimate_bytes * 3 // 2, V7X_VMEM_BYTES - 8 * 1024 * 1024))


def _dot(a, b):
    return jnp.dot(a, b, preferred_element_type=F32)


def _dot_nt(a, b):
    return lax.dot_general(a, b, (((1,), (1,)), ((), ())), preferred_element_type=F32)


def _dot_tn(a, b):
    return lax.dot_general(a, b, (((0,), (0,)), ((), ())), preferred_element_type=F32)


def _rms(t):
    return t * lax.rsqrt(jnp.mean(t * t, axis=-1, keepdims=True) + EPS)


def _sigmoid(t):
    return 0.5 + 0.5 * jnp.tanh(0.5 * t)


def _split_bf16(x, parts):
    out = []
    for _ in range(parts - 1):
        p = x.astype(BF16).astype(F32)
        out.append(p)
        x = x - p
    out.append(x.astype(BF16).astype(F32))
    return out


def _adaln_kernel(c_ref, w_ref, b_ref, o_ref):
    bsz = c_ref.shape[0]
    c_hi, c_lo = _split_bf16(c_ref[...], 2)
    w_hi, w_lo = _split_bf16(w_ref[...], 2)
    by_hi = _dot(jnp.concatenate([c_hi, c_lo], axis=0).astype(BF16), w_hi.astype(BF16))
    o_ref[...] = (by_hi[0:bsz] + by_hi[bsz:] + _dot(c_hi.astype(BF16), w_lo.astype(BF16))
                  + b_ref[...])


def _adaln(c, w, b):
    bsz, d = c.shape
    n = w.shape[1]
    tn = 1024
    est = 2 * (d * tn * 4) + 2 * bsz * d * 4 + 4 * bsz * tn * 4 + 3 * d * tn * 4
    return pl.pallas_call(
        _adaln_kernel,
        grid=(n // tn,),
        in_specs=[pl.BlockSpec((bsz, d), lambda j: (0, 0)),
                  pl.BlockSpec((d, tn), lambda j: (0, j)),
                  pl.BlockSpec((1, tn), lambda j: (0, j))],
        out_specs=pl.BlockSpec((bsz, tn), lambda j: (0, j)),
        out_shape=jax.ShapeDtypeStruct((bsz, n), F32),
        compiler_params=pltpu.CompilerParams(
            dimension_semantics=("arbitrary",), vmem_limit_bytes=_vmem_limit(est)),
        name="adaln_mod",
    )(c, w, b.reshape(1, n))


def _in_proj_kernel(x_ref, mod_ref, g_ref, w_ref, wkv_ref, o_ref, okv_ref, h_ref):
    @pl.when(pl.program_id(1) == 0)
    def _():
        y = _rms(x_ref[...]) * g_ref[...]
        h = y * (1.0 + mod_ref[1:2, :]) + mod_ref[0:1, :]
        h_ref[...] = h.astype(BF16)
        okv_ref[...] = _dot(h_ref[...], wkv_ref[...]).astype(BF16)

    o_ref[...] = _dot(h_ref[...], w_ref[...]).astype(BF16)


def _in_proj(x2, mod3, gain, w_in, seq):
    n, d = x2.shape
    tm, tn = 1024, 1024
    per_batch = seq // tm
    o_kv = ATT_WIDTH
    o_qh = o_kv + 2 * KV_WIDTH
    o_ga = o_qh + 4 * HG_WIDTH
    cols = w_in.shape[1] - 2 * KV_WIDTH
    n_gate = 2 * D_MODEL // tn
    n_q = ATT_WIDTH // tn

    def src_col(j):
        unit = tn // LANES
        lane_tile = jnp.where(j < n_gate, o_ga // LANES + unit * j,
                              jnp.where(j < n_gate + n_q, unit * (j - n_gate),
                                        o_qh // LANES + unit * (j - n_gate - n_q)))
        return lane_tile * LANES

    est = (2 * tm * d * 4 + tm * d * 2 + 2 * d * tn * 2 + 2 * tm * tn * 2
           + 2 * d * KV_WIDTH * 2 * 2 + 2 * tm * 2 * KV_WIDTH * 2 + tm * tn * 4 + tm * d * 4)
    return pl.pallas_call(
        _in_proj_kernel,
        grid=(n // tm, cols // tn),
        in_specs=[pl.BlockSpec((tm, d), lambda i, j: (i, 0)),
                  pl.BlockSpec((None, N_MOD, d), lambda i, j: (i // per_batch, 0, 0)),
                  pl.BlockSpec((1, d), lambda i, j: (0, 0)),
                  pl.BlockSpec((pl.Element(d), pl.Element(tn)), lambda i, j: (0, src_col(j))),
                  pl.BlockSpec((d, 2 * KV_WIDTH), lambda i, j: (0, o_kv // (2 * KV_WIDTH)))],
        out_specs=[pl.BlockSpec((tm, tn), lambda i, j: (i, j)),
                   pl.BlockSpec((tm, 2 * KV_WIDTH), lambda i, j: (i, 0))],
        out_shape=[jax.ShapeDtypeStruct((n, cols), BF16),
                   jax.ShapeDtypeStruct((n, 2 * KV_WIDTH), BF16)],
        scratch_shapes=[pltpu.VMEM((tm, d), BF16)],
        compiler_params=pltpu.CompilerParams(
            dimension_semantics=("parallel", "arbitrary"), vmem_limit_bytes=_vmem_limit(est)),
        name="in_proj",
    )(x2, mod3, gain, w_in, w_in)


def _attn_constants():
    lane = np.arange(LANES)
    dim = lane % HEAD_DIM
    expand = np.zeros((2, 3, ROT_HALF, 3, LANES), np.float32)
    for j in range(ROT_HALF):
        expand[0, :, j, 0, (dim < ROT_DIM) & (dim % ROT_HALF == j)] = 1.0
        expand[1, :, j, 1, (dim < ROT_HALF) & (dim == j)] = -1.0
        expand[1, :, j, 2, (dim >= ROT_HALF) & (dim < ROT_DIM) & (dim - ROT_HALF == j)] = 1.0
    expand = expand.reshape(6 * ROT_HALF, 3 * LANES)
    shift = np.zeros((LANES, 2 * LANES), np.float32)
    for l in range(LANES):
        if l + ROT_HALF < LANES:
            shift[l + ROT_HALF, l] = 1.0
        if l - ROT_HALF >= 0:
            shift[l - ROT_HALF, LANES + l] = 1.0
    return jnp.asarray(expand, BF16), jnp.asarray(shift, BF16)


def _attn_kernel(sink_ref, freq_ref, expand_ref, shift_ref, pos_ref, q_ref, kv_ref, o_ref,
                 qb_ref, kb_ref, vb_ref):
    members = q_ref.shape[0]
    nblk = pl.program_id(1)
    parity = nblk % 2
    blk = WINDOW
    lane = lax.broadcasted_iota(jnp.int32, (blk, LANES), 1)
    lo = lane < HEAD_DIM
    dim = lane % HEAD_DIM
    kj = lax.broadcasted_iota(jnp.int32, (blk, blk), 0)
    qi = lax.broadcasted_iota(jnp.int32, (blk, blk), 1)

    @pl.when(nblk == 0)
    def _():
        kb_ref[...] = jnp.zeros_like(kb_ref)
        vb_ref[...] = jnp.zeros_like(vb_ref)
        eye = jnp.where(kj == qi, 1.0, 0.0).astype(BF16)
        ones_lo = jnp.where(lo, 1.0, 0.0).astype(BF16)
        ones_hi = jnp.where(lo, 0.0, 1.0).astype(BF16)
        for g in range(members):
            for p in range(ATT_HEADS // 2):
                qb_ref[g, p * blk:(p + 1) * blk, LANES:] = eye
            for kvh in range(ATT_KV_HEADS):
                for quarter in range(4):
                    rows = slice(quarter * blk, (quarter + 1) * blk)
                    vb_ref[g, kvh, rows, LANES:] = ones_hi if quarter >= 2 else ones_lo

    def rope(x_bf16, tables):
        t_cos, t_lo, t_hi = tables
        ud = _dot(x_bf16, shift_ref[...])
        return x_bf16.astype(F32) * t_cos + ud[:, 0:LANES] * t_lo + ud[:, LANES:] * t_hi

    k_tabs, q_tabs = [], []
    for g in range(members):
        ang = freq_ref[...] * pos_ref[g]
        parts = _split_bf16(jnp.cos(ang), 3) + _split_bf16(jnp.sin(ang), 3)
        tabs = _dot_tn(jnp.concatenate(parts, axis=0).astype(BF16), expand_ref[...])
        kt = (tabs[:, 0:LANES] + jnp.where(dim < ROT_DIM, 0.0, 1.0),
              tabs[:, LANES:2 * LANES], tabs[:, 2 * LANES:3 * LANES])
        k_tabs.append(kt)
        q_tabs.append(tuple(t * (HEAD_DIM ** -0.5 * LOG2E) for t in kt))

    bias = []
    for half in range(2):
        is_cur = parity == half
        valid = (is_cur & (kj <= qi)) | ((~is_cur) & (nblk > 0) & (kj > qi))
        bias.append(jnp.where(valid, 0.0, MASK_BIAS).astype(BF16))
    for g in range(members):
        k = rope(kv_ref[g, :, 0:KV_WIDTH], k_tabs[g])
        v = kv_ref[g, :, KV_WIDTH:].astype(F32)
        k_sw = pltpu.roll(k, HEAD_DIM, axis=1)
        v_sw = pltpu.roll(v, HEAD_DIM, axis=1)
        for kvh in range(ATT_KV_HEADS):
            for slot in range(2):
                ks = (k, k_sw)[kvh ^ slot]
                vs = (v, v_sw)[kvh ^ slot]
                keep = lo if slot == 0 else ~lo
                rows = pl.ds(pl.multiple_of((2 * slot + parity) * blk, blk), blk)
                kb_ref[g, kvh, rows, 0:LANES] = jnp.where(keep, ks, 0.0).astype(BF16)
                vb_ref[g, kvh, rows, 0:LANES] = jnp.where(keep, vs, 0.0).astype(BF16)
                for half in range(2):
                    rows = slice((2 * slot + half) * blk, (2 * slot + half + 1) * blk)
                    kb_ref[g, kvh, rows, LANES:] = bias[half]

    pairs_per_kv = ATT_HEADS // ATT_KV_HEADS // 2
    cols = [slice(p * LANES, (p + 1) * LANES) for p in range(ATT_HEADS // 2)]
    groups = [(g, kvh) for g in range(members) for kvh in range(ATT_KV_HEADS)]
    pairs_of = lambda kvh: range(kvh * pairs_per_kv, (kvh + 1) * pairs_per_kv)
    for g in range(members):
        stacked = jnp.concatenate([q_ref[g, :, cs] for cs in cols], axis=0)
        tables = tuple(jnp.concatenate([t] * len(cols), axis=0) for t in q_tabs[g])
        qb_ref[g, :, 0:LANES] = rope(stacked, tables).astype(BF16)
    rows_of = lambda kvh: slice(kvh * pairs_per_kv * blk, (kvh + 1) * pairs_per_kv * blk)
    s = [_dot_nt(qb_ref[g, rows_of(kvh), :], kb_ref[g, kvh]) for g, kvh in groups]
    probs, sink = [], []
    for w, (g, kvh) in enumerate(groups):
        pp, ss = [], []
        for i, p in enumerate(pairs_of(kvh)):
            pe, se = [], []
            for e in range(2):
                logits = s[w][i * blk:(i + 1) * blk, e * 2 * blk:(e + 1) * 2 * blk]
                sink2 = sink_ref[2 * p + e] * LOG2E
                m = jnp.maximum(jnp.max(logits, axis=-1, keepdims=True), sink2)
                pe.append(jnp.exp2(logits - m).astype(BF16))
                se.append(jnp.exp2(sink2 - m))
            pp.append(jnp.concatenate(pe, axis=1))
            ss.append(jnp.where(lo, se[0], se[1]))
        probs.append(jnp.concatenate(pp, axis=0))
        sink.append(ss)
    o = [_dot(probs[w], vb_ref[g, kvh]) for w, (g, kvh) in enumerate(groups)]
    for w, (g, kvh) in enumerate(groups):
        for i, p in enumerate(pairs_of(kvh)):
            op = o[w][i * blk:(i + 1) * blk]
            o_ref[g, :, cols[p]] = (op[:, 0:LANES] / (op[:, LANES:] + sink[w][i])).astype(BF16)


def _attention(sinks, freq8, pos4, proj3, kv3, q_col_block):
    bsz, seq = proj3.shape[0], proj3.shape[1]
    blk = WINDOW
    members = 4
    expand, shift = _attn_constants()
    full = lambda a: pl.BlockSpec(a.shape, lambda b, i: (0,) * a.ndim)
    est = (2 * members * (blk * ATT_WIDTH * 2 * 2 + blk * 2 * KV_WIDTH * 2)
           + 2 * members * ATT_KV_HEADS * 4 * blk * 2 * LANES * 2 + members * ATT_WIDTH * blk * 2
           + members * 64 * blk * 4 * blk * 4)
    return pl.pallas_call(
        _attn_kernel,
        grid=(bsz // members, seq // blk),
        in_specs=[pl.BlockSpec(memory_space=pltpu.SMEM),
                  full(freq8), full(expand), full(shift),
                  pl.BlockSpec((members, None, 1, blk), lambda b, i: (b, i, 0, 0)),
                  pl.BlockSpec((members, blk, ATT_WIDTH), lambda b, i: (b, i, q_col_block)),
                  pl.BlockSpec((members, blk, 2 * KV_WIDTH), lambda b, i: (b, i, 0))],
        out_specs=pl.BlockSpec((members, blk, ATT_WIDTH), lambda b, i: (b, i, 0)),
        out_shape=jax.ShapeDtypeStruct((bsz, seq, ATT_WIDTH), BF16),
        scratch_shapes=[pltpu.VMEM((members, ATT_HEADS // 2 * blk, 2 * LANES), BF16),
                        pltpu.VMEM((members, ATT_KV_HEADS, 4 * blk, 2 * LANES), BF16),
                        pltpu.VMEM((members, ATT_KV_HEADS, 4 * blk, 2 * LANES), BF16)],
        compiler_params=pltpu.CompilerParams(
            dimension_semantics=("parallel", "arbitrary"), vmem_limit_bytes=_vmem_limit(est)),
        name="swa_attn",
    )(sinks, freq8, expand, shift, pos4, proj3, kv3)


def _hgrn_constants():
    t = np.arange(CHUNK)[:, None]
    j = np.arange(CHUNK)[None, :]
    tri = (j <= t).astype(np.float32)
    cumsum = np.concatenate([tri, tri], axis=1)
    reduce = np.zeros((DIAG, HG_DIM, LANES), np.float32)
    for d in range(DIAG):
        reduce[d, :, DIAG - 1 - d] = 1.0
    return jnp.asarray(cumsum, BF16), jnp.asarray(reduce.reshape(DIAG * HG_DIM, LANES), BF16)


def _hgrn_kernel(cm_ref, red_ref, lbp_ref, gn_ref, q_ref, f_ref, i_ref, g_ref, o_ref, st_ref, *,
                 tokens):
    @pl.when(pl.program_id(1) == 0)
    def _():
        st_ref[...] = jnp.zeros_like(st_ref)

    lbp = lbp_ref[...]
    e = jnp.exp(lbp - jnp.max(lbp, axis=0, keepdims=True))
    lb = e[0:1, :] / jnp.sum(e, axis=0, keepdims=True)
    half_key_scale = 0.5 * (1.0 - lb)

    row = lax.broadcasted_iota(jnp.int32, (CHUNK, CHUNK), 0)
    col = lax.broadcasted_iota(jnp.int32, (CHUNK, CHUNK), 1)
    same = {w: row // w == col // w for w in (DIAG, 2 * DIAG, 4 * DIAG)}
    groups = CHUNK // DIAG

    heads = [slice(h * HG_DIM, (h + 1) * HG_DIM) for h in range(HG_HEADS)]
    chunks = [slice(c * CHUNK, (c + 1) * CHUNK) for c in range(tokens // CHUNK)]
    per_chunk = []

    for rs in chunks:
        half_q = 0.5 * q_ref[rs, :].astype(F32)
        qs = half_q + half_q * jnp.tanh(half_q)
        key = half_key_scale - half_key_scale * jnp.tanh(0.5 * f_ref[rs, :].astype(F32))
        f = 1.0 - key
        val = i_ref[rs, :]

        b = _dot(cm_ref[...], jnp.concatenate(_split_bf16(jnp.log2(f), 2), axis=0).astype(BF16))
        eb = jnp.exp2(b)
        q_state = (qs * eb).astype(BF16)
        k_state = (key * jnp.exp2(b[CHUNK - 1:CHUNK, :] - b)).astype(BF16)
        decay = eb[CHUNK - 1:CHUNK, :]

        q_lv, k_lv = [], []
        for lv in LEVELS:
            qp, kp = [], []
            for piece in range(CHUNK // lv):
                ps = slice(piece * lv, (piece + 1) * lv)
                zero = jnp.zeros((lv, HG_WIDTH), F32)
                if piece % 2:
                    ref = b[piece * lv - 1:piece * lv, :]
                    qp.append(qs[ps] * jnp.exp2(b[ps] - ref)); kp.append(zero)
                else:
                    ref = b[(piece + 1) * lv - 1:(piece + 1) * lv, :]
                    qp.append(zero); kp.append(key[ps] * jnp.exp2(ref - b[ps]))
            q_lv.append(jnp.concatenate(qp, axis=0).astype(BF16))
            k_lv.append(jnp.concatenate(kp, axis=0).astype(BF16))

        carry = key.reshape(groups, DIAG, HG_WIDTH)
        f3 = f.reshape(groups, DIAG, HG_WIDTH)
        qs_bf16 = qs.astype(BF16)
        diag = [qs_bf16 * key.astype(BF16)]
        for d in range(1, DIAG):
            carry = f3 * pltpu.roll(carry, 1, axis=1)
            diag.append(qs_bf16 * carry.reshape(CHUNK, HG_WIDTH).astype(BF16))

        parts = [[_dot_nt(q_lv[n][:, hs], k_lv[n][:, hs]) for n in range(len(LEVELS))]
                 for hs in heads]
        r = _dot(jnp.concatenate([jnp.concatenate([dg[:, hs] for dg in diag], axis=1)
                                  for hs in heads], axis=0), red_ref[...])
        update = [_dot_tn(val[:, hs], k_state[:, hs]) for hs in heads]
        per_chunk.append((parts, r, update, q_state, val, decay))

    g_all = []
    for parts, r, _, _, _, _ in per_chunk:
        g_chunk = []
        for h in range(HG_HEADS):
            g_diag = pltpu.roll(r[h * CHUNK:(h + 1) * CHUNK], LANES - (DIAG - 1), axis=1,
                                stride=1, stride_axis=0)[:, 0:CHUNK]
            g_mat = jnp.where(same[DIAG], g_diag,
                              jnp.where(same[2 * DIAG], parts[h][2],
                                        jnp.where(same[4 * DIAG], parts[h][1], parts[h][0])))
            g_chunk.append(g_mat.astype(BF16))
        g_all.append(g_chunk)

    state = [st_ref[h] for h in range(HG_HEADS)]
    for rs, g_chunk, (_, _, update, q_state, val, decay) in zip(chunks, g_all, per_chunk):
        for h, hs in enumerate(heads):
            o = _dot_nt(q_state[:, hs], state[h].astype(BF16)) + _dot(g_chunk[h], val[:, hs])
            state[h] = state[h] * decay[:, hs] + update[h]
            y = _rms(o) * gn_ref[...]
            o_ref[rs, hs] = (y * _sigmoid(g_ref[rs, hs].astype(F32))).astype(BF16)
    for h in range(HG_HEADS):
        st_ref[h] = state[h]


def _hgrn(lb_params, gnorm, proj, bsz, seq, col_blocks):
    n = proj.shape[0]
    tokens = 8 * CHUNK
    nt = seq // tokens
    cmat, rmat = _hgrn_constants()
    row = lambda b, i: b * nt + i
    spec = lambda cb: pl.BlockSpec((tokens, HG_WIDTH), lambda b, i: (row(b, i), cb))
    full = lambda a: pl.BlockSpec(a.shape, lambda b, i: (0,) * a.ndim)
    est = 2 * 5 * tokens * HG_WIDTH * 2 + HG_HEADS * HG_DIM * HG_DIM * 4 + 40 * CHUNK * HG_WIDTH * 4
    return pl.pallas_call(
        functools.partial(_hgrn_kernel, tokens=tokens),
        grid=(bsz, nt),
        in_specs=[full(cmat), full(rmat), full(lb_params), full(gnorm),
                  spec(col_blocks[0]), spec(col_blocks[1]), spec(col_blocks[2]), spec(col_blocks[3])],
        out_specs=pl.BlockSpec((tokens, HG_WIDTH), lambda b, i: (row(b, i), 0)),
        out_shape=jax.ShapeDtypeStruct((n, HG_WIDTH), BF16),
        scratch_shapes=[pltpu.VMEM((HG_HEADS, HG_DIM, HG_DIM), F32)],
        compiler_params=pltpu.CompilerParams(
            dimension_semantics=("parallel", "arbitrary"), vmem_limit_bytes=_vmem_limit(est)),
        name="hgrn2",
    )(cmat, rmat, lb_params, gnorm, proj, proj, proj, proj)


def _mix_kernel(x_ref, mod_ref, g_ref, a_ref, hg_ref, ga_ref, gh_ref, wa_ref, wh_ref, wo_ref,
                o_ref, m_ref, *, tc):
    d = x_ref.shape[1]
    for c in range(d // tc):
        cs = slice(c * tc, (c + 1) * tc)
        ya = _dot(a_ref[...], wa_ref[:, cs])
        yh = _dot(hg_ref[...], wh_ref[:, cs])
        merged = (_sigmoid(ga_ref[:, cs].astype(F32)) * ya
                  + _sigmoid(gh_ref[:, cs].astype(F32)) * yh)
        m_ref[:, cs] = merged.astype(BF16)
    y = _dot(m_ref[...], wo_ref[...])
    o_ref[...] = x_ref[...] + mod_ref[2:3, :] * (_rms(y) * g_ref[...])


def _mix_out(x2, mod3, gain, attn, hg, proj, wa, wh, wo, seq, ga_block, gh_block):
    n, d = x2.shape
    tm, tc = 512, 512
    per_batch = seq // tm
    const = lambda shape: pl.BlockSpec(shape, lambda i: (0, 0), pipeline_mode=pl.Buffered(1))
    est = ((wa.size + wh.size + wo.size) * 2 + 2 * 2 * tm * d * 4 + 2 * 2 * tm * d * 2
           + 2 * 2 * tm * ATT_WIDTH * 2 + tm * d * 2 + 3 * tm * d * 4)
    return pl.pallas_call(
        functools.partial(_mix_kernel, tc=tc),
        grid=(n // tm,),
        in_specs=[pl.BlockSpec((tm, d), lambda i: (i, 0)),
                  pl.BlockSpec((None, N_MOD, d), lambda i: (i // per_batch, 0, 0)),
                  pl.BlockSpec((1, d), lambda i: (0, 0)),
                  pl.BlockSpec((tm, ATT_WIDTH), lambda i: (i, 0)),
                  pl.BlockSpec((tm, HG_WIDTH), lambda i: (i, 0)),
                  pl.BlockSpec((tm, d), lambda i: (i, ga_block)),
                  pl.BlockSpec((tm, d), lambda i: (i, gh_block)),
                  const(wa.shape), const(wh.shape), const(wo.shape)],
        out_specs=pl.BlockSpec((tm, d), lambda i: (i, 0)),
        out_shape=jax.ShapeDtypeStruct((n, d), F32),
        scratch_shapes=[pltpu.VMEM((tm, d), BF16)],
        compiler_params=pltpu.CompilerParams(
            dimension_semantics=("parallel",), vmem_limit_bytes=_vmem_limit(est)),
        name="mix_out",
    )(x2, mod3, gain, attn, hg, proj, proj, wa, wh, wo)


def _ffn_up_kernel(x_ref, mod_ref, gpre_ref, wg_ref, wu_ref, o_ref, h_even_ref, h_odd_ref, *,
                   slices):
    i, j = pl.program_id(0), pl.program_id(1)

    def modulated_norm(rows):
        y = _rms(x_ref[rows, :]) * gpre_ref[...]
        return (y * (1.0 + mod_ref[4:5, :]) + mod_ref[3:4, :]).astype(BF16)

    @pl.when((i == 0) & (j == 0))
    def _():
        h_even_ref[...] = modulated_norm(slice(None))

    def step(h_ref, h_next_ref):
        rows_per = x_ref.shape[0] // slices
        first = pl.multiple_of(jnp.clip(j - 1, 0, slices - 1) * rows_per, rows_per)
        rows = pl.ds(first, rows_per)
        h_next_ref[rows, :] = modulated_norm(rows)

        h = h_ref[...]
        g = _dot(h, wg_ref[...].astype(BF16))
        u = _dot(h, wu_ref[...].astype(BF16))
        o_ref[...] = (g * _sigmoid(g) * u).astype(BF16)

    pl.when(i % 2 == 0)(lambda: step(h_even_ref, h_odd_ref))
    pl.when(i % 2 == 1)(lambda: step(h_odd_ref, h_even_ref))


def _ffn_down_kernel(a_ref, x_ref, mod_ref, gpost_ref, wd_ref, o_ref):
    y = _dot(a_ref[...], wd_ref[...])
    o_ref[...] = x_ref[...] + mod_ref[5:6, :] * (_rms(y) * gpost_ref[...])


def _ffn(x1, mod3, gpre, gpost, w_in, w_down, seq):
    n, d = x1.shape
    hidden = w_down.shape[0]

    tm, th = 1024, 512
    nh = hidden // th
    per_batch = seq // tm
    last = n // tm - 1
    x_tile = lambda i, j: jnp.minimum(i + jnp.where(j > 0, 1, 0), last)
    est = (2 * tm * d * 4 + 2 * tm * d * 2 + 2 * 2 * d * th * 4 + 2 * d * th * 2 + 2 * tm * th * 2
           + 4 * tm * th * 4)
    act = pl.pallas_call(
        functools.partial(_ffn_up_kernel, slices=8),
        grid=(n // tm, nh),
        in_specs=[pl.BlockSpec((tm, d), lambda i, j: (x_tile(i, j), 0)),
                  pl.BlockSpec((None, N_MOD, d), lambda i, j: (x_tile(i, j) // per_batch, 0, 0)),
                  pl.BlockSpec((1, d), lambda i, j: (0, 0)),
                  pl.BlockSpec((d, th), lambda i, j: (0, j)),
                  pl.BlockSpec((d, th), lambda i, j: (0, nh + j))],
        out_specs=pl.BlockSpec((tm, th), lambda i, j: (i, j)),
        out_shape=jax.ShapeDtypeStruct((n, hidden), BF16),
        scratch_shapes=[pltpu.VMEM((tm, d), BF16), pltpu.VMEM((tm, d), BF16)],
        compiler_params=pltpu.CompilerParams(
            dimension_semantics=("arbitrary", "arbitrary"), vmem_limit_bytes=_vmem_limit(est)),
        name="ffn_up",
    )(x1, mod3, gpre, w_in, w_in)

    tm = 256
    per_batch = seq // tm
    est = hidden * d * 2 + 2 * tm * hidden * 2 + 2 * 2 * tm * d * 4 + 2 * tm * d * 4
    return pl.pallas_call(
        _ffn_down_kernel,
        grid=(n // tm,),
        in_specs=[pl.BlockSpec((tm, hidden), lambda i: (i, 0)),
                  pl.BlockSpec((tm, d), lambda i: (i, 0)),
                  pl.BlockSpec((None, N_MOD, d), lambda i: (i // per_batch, 0, 0)),
                  pl.BlockSpec((1, d), lambda i: (0, 0)),
                  pl.BlockSpec((hidden, d), lambda i: (0, 0), pipeline_mode=pl.Buffered(1))],
        out_specs=pl.BlockSpec((tm, d), lambda i: (i, 0)),
        out_shape=jax.ShapeDtypeStruct((n, d), F32),
        compiler_params=pltpu.CompilerParams(
            dimension_semantics=("parallel",), vmem_limit_bytes=_vmem_limit(est)),
        name="ffn_down",
    )(act, x1, mod3, gpost, w_down)


def kernel(x, c, positions, w_ada, b_ada, g_pre_mix, g_post_mix, g_pre_ffn, g_post_ffn, w_in,
           attn_sinks, w_attn_proj, hg_lower_bounds, hg_norm, w_hgrn_proj, w_out, w_ffn_in,
           w_ffn_out):
    bsz, seq, d = x.shape
    n = bsz * seq
    assert d == D_MODEL and w_ada.shape[0] == 1 and seq % 1024 == 0 and bsz % 4 == 0

    ga_block, gh_block = 0, 1
    q_block = 2 * D_MODEL // ATT_WIDTH
    hg_blocks = tuple((2 * D_MODEL + ATT_WIDTH) // HG_WIDTH + k for k in range(4))

    x2 = x.reshape(n, d)
    mod3 = _adaln(c, w_ada[0], b_ada[0]).reshape(bsz, N_MOD, d)

    proj, kv = _in_proj(x2, mod3, g_pre_mix, w_in[0].astype(BF16), seq)

    inv_freq = ROPE_THETA ** (-jnp.arange(0, ROT_DIM, 2, dtype=F32) / ROT_DIM)
    freq8 = jnp.broadcast_to(inv_freq[:, None], (ROT_HALF, WINDOW))
    pos4 = positions.astype(F32).reshape(bsz, seq // WINDOW, 1, WINDOW)
    attn = _attention(attn_sinks[0], freq8, pos4, proj.reshape(bsz, seq, -1),
                      kv.reshape(bsz, seq, -1), q_block).reshape(n, ATT_WIDTH)

    hg = _hgrn(hg_lower_bounds, hg_norm, proj, bsz, seq, hg_blocks)

    x1 = _mix_out(x2, mod3, g_post_mix, attn, hg, proj, w_attn_proj[0].astype(BF16),
                  w_hgrn_proj[0].astype(BF16), w_out[0].astype(BF16), seq, ga_block, gh_block)

    out = _ffn(x1, mod3, g_pre_ffn, g_post_ffn, w_ffn_in[0], w_ffn_out[0].astype(BF16), seq)
    return out.reshape(bsz, seq, d)
```

```python
import functools
import math

import numpy as np
import jax
import jax.numpy as jnp
from jax import lax
from jax.experimental import pallas as pl
from jax.experimental.pallas import tpu as pltpu

F32 = jnp.float32
BF16 = jnp.bfloat16

D_MODEL = 2048
ATT_HEADS = 16
ATT_KV_HEADS = 2
HEAD_DIM = 64
ATT_WIDTH = ATT_HEADS * HEAD_DIM
KV_WIDTH = ATT_KV_HEADS * HEAD_DIM
WINDOW = 128
ROT_DIM = HEAD_DIM // 4
ROT_HALF = ROT_DIM // 2
ROPE_THETA = 500000.0
HG_HEADS = 8
HG_DIM = 128
HG_WIDTH = HG_HEADS * HG_DIM
N_MOD = 6
EPS = 1e-6
LOG2E = math.log2(math.e)
MASK_BIAS = -1e30

LANES = 128
SUBLANES = 8
V7X_VMEM_BYTES = 64 * 1024 * 1024

CHUNK = 64
DIAG = SUBLANES
LEVELS = (32, 16, 8)


def _vmem_limit(estimate_bytes):
    return int(min(estimate_bytes * 3 // 2, V7X_VMEM_BYTES - 8 * 1024 * 1024))


def _dot(a, b):
    return jnp.dot(a, b, preferred_element_type=F32)


def _dot_nt(a, b):
    return lax.dot_general(a, b, (((1,), (1,)), ((), ())), preferred_element_type=F32)


def _dot_tn(a, b):
    return lax.dot_general(a, b, (((0,), (0,)), ((), ())), preferred_element_type=F32)


def _rms(t):
    return t * lax.rsqrt(jnp.mean(t * t, axis=-1, keepdims=True) + EPS)


def _sigmoid(t):
    return 0.5 + 0.5 * jnp.tanh(0.5 * t)


def _split_bf16(x, parts):
    out = []
    for _ in range(parts - 1):
        p = x.astype(BF16).astype(F32)
        out.append(p)
        x = x - p
    out.append(x.astype(BF16).astype(F32))
    return out


def _adaln_kernel(c_ref, w_ref, b_ref, o_ref):
    bsz = c_ref.shape[0]
    c_hi, c_lo = _split_bf16(c_ref[...], 2)
    w_hi, w_lo = _split_bf16(w_ref[...], 2)
    by_hi = _dot(jnp.concatenate([c_hi, c_lo], axis=0).astype(BF16), w_hi.astype(BF16))
    o_ref[...] = (by_hi[0:bsz] + by_hi[bsz:] + _dot(c_hi.astype(BF16), w_lo.astype(BF16))
                  + b_ref[...])


def _adaln(c, w, b):
    bsz, d = c.shape
    n = w.shape[1]
    tn = 1024
    est = 2 * (d * tn * 4) + 2 * bsz * d * 4 + 4 * bsz * tn * 4 + 3 * d * tn * 4
    return pl.pallas_call(
        _adaln_kernel,
        grid=(n // tn,),
        in_specs=[pl.BlockSpec((bsz, d), lambda j: (0, 0)),
                  pl.BlockSpec((d, tn), lambda j: (0, j)),
                  pl.BlockSpec((1, tn), lambda j: (0, j))],
        out_specs=pl.BlockSpec((bsz, tn), lambda j: (0, j)),
        out_shape=jax.ShapeDtypeStruct((bsz, n), F32),
        compiler_params=pltpu.CompilerParams(
            dimension_semantics=("arbitrary",), vmem_limit_bytes=_vmem_limit(est)),
        name="adaln_mod",
    )(c, w, b.reshape(1, n))


def _in_proj_kernel(x_ref, mod_ref, g_ref, w_ref, wkv_ref, o_ref, okv_ref, h_even_ref, h_odd_ref, *,
                    slices):
    i, j = pl.program_id(0), pl.program_id(1)

    def modulated_norm(rows):
        y = _rms(x_ref[rows, :]) * g_ref[...]
        return (y * (1.0 + mod_ref[1:2, :]) + mod_ref[0:1, :]).astype(BF16)

    @pl.when((i == 0) & (j == 0))
    def _():
        h_even_ref[...] = modulated_norm(slice(None))

    def step(h_ref, h_next_ref):
        @pl.when(j == 0)
        def _():
            okv_ref[...] = _dot(h_ref[...], wkv_ref[...]).astype(BF16)

        rows_per = x_ref.shape[0] // slices
        first = pl.multiple_of(jnp.clip(j - 1, 0, slices - 1) * rows_per, rows_per)
        rows = pl.ds(first, rows_per)
        h_next_ref[rows, :] = modulated_norm(rows)
        o_ref[...] = _dot(h_ref[...], w_ref[...]).astype(BF16)

    pl.when(i % 2 == 0)(lambda: step(h_even_ref, h_odd_ref))
    pl.when(i % 2 == 1)(lambda: step(h_odd_ref, h_even_ref))


def _in_proj(x2, mod3, gain, w_in, seq):
    n, d = x2.shape
    tm, tn = 1024, 1024
    per_batch = seq // tm
    o_kv = ATT_WIDTH
    o_qh = o_kv + 2 * KV_WIDTH
    o_ga = o_qh + 4 * HG_WIDTH
    cols = w_in.shape[1] - 2 * KV_WIDTH
    n_gate = 2 * D_MODEL // tn
    n_q = ATT_WIDTH // tn

    def src_col(j):
        unit = tn // LANES
        lane_tile = jnp.where(j < n_gate, o_ga // LANES + unit * j,
                              jnp.where(j < n_gate + n_q, unit * (j - n_gate),
                                        o_qh // LANES + unit * (j - n_gate - n_q)))
        return lane_tile * LANES

    last = n // tm - 1
    x_tile = lambda i, j: jnp.minimum(i + jnp.where(j > 0, 1, 0), last)
    est = (2 * tm * d * 4 + 2 * tm * d * 2 + 2 * d * tn * 2 + 2 * tm * tn * 2
           + 2 * d * KV_WIDTH * 2 * 2 + 2 * tm * 2 * KV_WIDTH * 2 + tm * tn * 4 + tm * d * 4 // 8)
    return pl.pallas_call(
        functools.partial(_in_proj_kernel, slices=8),
        grid=(n // tm, cols // tn),
        in_specs=[pl.BlockSpec((tm, d), lambda i, j: (x_tile(i, j), 0)),
                  pl.BlockSpec((None, N_MOD, d), lambda i, j: (x_tile(i, j) // per_batch, 0, 0)),
                  pl.BlockSpec((1, d), lambda i, j: (0, 0)),
                  pl.BlockSpec((pl.Element(d), pl.Element(tn)), lambda i, j: (0, src_col(j))),
                  pl.BlockSpec((d, 2 * KV_WIDTH), lambda i, j: (0, o_kv // (2 * KV_WIDTH)))],
        out_specs=[pl.BlockSpec((tm, tn), lambda i, j: (i, j)),
                   pl.BlockSpec((tm, 2 * KV_WIDTH), lambda i, j: (i, 0))],
        out_shape=[jax.ShapeDtypeStruct((n, cols), BF16),
                   jax.ShapeDtypeStruct((n, 2 * KV_WIDTH), BF16)],
        scratch_shapes=[pltpu.VMEM((tm, d), BF16), pltpu.VMEM((tm, d), BF16)],
        compiler_params=pltpu.CompilerParams(
            dimension_semantics=("arbitrary", "arbitrary"), vmem_limit_bytes=_vmem_limit(est)),
        name="in_proj",
    )(x2, mod3, gain, w_in, w_in)


def _attn_constants():
    lane = np.arange(LANES)
    dim = lane % HEAD_DIM
    expand = np.zeros((2, 3, ROT_HALF, 3, LANES), np.float32)
    for j in range(ROT_HALF):
        expand[0, :, j, 0, (dim < ROT_DIM) & (dim % ROT_HALF == j)] = 1.0
        expand[1, :, j, 1, (dim < ROT_HALF) & (dim == j)] = -1.0
        expand[1, :, j, 2, (dim >= ROT_HALF) & (dim < ROT_DIM) & (dim - ROT_HALF == j)] = 1.0
    expand = expand.reshape(6 * ROT_HALF, 3 * LANES)
    shift = np.zeros((LANES, 2 * LANES), np.float32)
    for l in range(LANES):
        if l + ROT_HALF < LANES:
            shift[l + ROT_HALF, l] = 1.0
        if l - ROT_HALF >= 0:
            shift[l - ROT_HALF, LANES + l] = 1.0
    return jnp.asarray(expand, BF16), jnp.asarray(shift, BF16)


def _attn_kernel(sink_ref, freq_ref, expand_ref, shift_ref, pos_ref, q_ref, kv_ref, o_ref,
                 qb_ref, kb_ref, vb_ref):
    members = q_ref.shape[0]
    nblk = pl.program_id(1)
    parity = nblk % 2
    blk = WINDOW
    lane = lax.broadcasted_iota(jnp.int32, (blk, LANES), 1)
    lo = lane < HEAD_DIM
    dim = lane % HEAD_DIM
    kj = lax.broadcasted_iota(jnp.int32, (blk, blk), 0)
    qi = lax.broadcasted_iota(jnp.int32, (blk, blk), 1)

    @pl.when(nblk == 0)
    def _():
        kb_ref[...] = jnp.zeros_like(kb_ref)
        vb_ref[...] = jnp.zeros_like(vb_ref)
        eye = jnp.where(kj == qi, 1.0, 0.0).astype(BF16)
        ones_lo = jnp.where(lo, 1.0, 0.0).astype(BF16)
        ones_hi = jnp.where(lo, 0.0, 1.0).astype(BF16)
        for g in range(members):
            for p in range(ATT_HEADS // 2):
                qb_ref[g, p * blk:(p + 1) * blk, LANES:] = eye
            for kvh in range(ATT_KV_HEADS):
                for quarter in range(4):
                    rows = slice(quarter * blk, (quarter + 1) * blk)
                    vb_ref[g, kvh, rows, LANES:] = ones_hi if quarter >= 2 else ones_lo

    def rope(x_bf16, tables):
        t_cos, t_lo, t_hi = tables
        ud = _dot(x_bf16, shift_ref[...])
        return x_bf16.astype(F32) * t_cos + ud[:, 0:LANES] * t_lo + ud[:, LANES:] * t_hi

    k_tabs, q_tabs = [], []
    for g in range(members):
        ang = freq_ref[...] * pos_ref[g]
        parts = _split_bf16(jnp.cos(ang), 3) + _split_bf16(jnp.sin(ang), 3)
        tabs = _dot_tn(jnp.concatenate(parts, axis=0).astype(BF16), expand_ref[...])
        kt = (tabs[:, 0:LANES] + jnp.where(dim < ROT_DIM, 0.0, 1.0),
              tabs[:, LANES:2 * LANES], tabs[:, 2 * LANES:3 * LANES])
        k_tabs.append(kt)
        q_tabs.append(tuple(t * (HEAD_DIM ** -0.5 * LOG2E) for t in kt))

    bias = []
    for half in range(2):
        is_cur = parity == half
        valid = (is_cur & (kj <= qi)) | ((~is_cur) & (nblk > 0) & (kj > qi))
        bias.append(jnp.where(valid, 0.0, MASK_BIAS).astype(BF16))
    for g in range(members):
        k = rope(kv_ref[g, :, 0:KV_WIDTH], k_tabs[g])
        v = kv_ref[g, :, KV_WIDTH:].astype(F32)
        k_sw = pltpu.roll(k, HEAD_DIM, axis=1)
        v_sw = pltpu.roll(v, HEAD_DIM, axis=1)
        for kvh in range(ATT_KV_HEADS):
            for slot in range(2):
                ks = (k, k_sw)[kvh ^ slot]
                vs = (v, v_sw)[kvh ^ slot]
                keep = lo if slot == 0 else ~lo
                rows = pl.ds(pl.multiple_of((2 * slot + parity) * blk, blk), blk)
                kb_ref[g, kvh, rows, 0:LANES] = jnp.where(keep, ks, 0.0).astype(BF16)
                vb_ref[g, kvh, rows, 0:LANES] = jnp.where(keep, vs, 0.0).astype(BF16)
                for half in range(2):
                    rows = slice((2 * slot + half) * blk, (2 * slot + half + 1) * blk)
                    kb_ref[g, kvh, rows, LANES:] = bias[half]

    pairs_per_kv = ATT_HEADS // ATT_KV_HEADS // 2
    cols = [slice(p * LANES, (p + 1) * LANES) for p in range(ATT_HEADS // 2)]
    groups = [(g, kvh) for g in range(members) for kvh in range(ATT_KV_HEADS)]
    pairs_of = lambda kvh: range(kvh * pairs_per_kv, (kvh + 1) * pairs_per_kv)
    for g in range(members):
        stacked = jnp.concatenate([q_ref[g, :, cs] for cs in cols], axis=0)
        tables = tuple(jnp.concatenate([t] * len(cols), axis=0) for t in q_tabs[g])
        qb_ref[g, :, 0:LANES] = rope(stacked, tables).astype(BF16)
    rows_of = lambda kvh: slice(kvh * pairs_per_kv * blk, (kvh + 1) * pairs_per_kv * blk)
    s = [_dot_nt(qb_ref[g, rows_of(kvh), :], kb_ref[g, kvh]) for g, kvh in groups]
    probs, sink = [], []
    for w, (g, kvh) in enumerate(groups):
        pp, ss = [], []
        for i, p in enumerate(pairs_of(kvh)):
            pe, se = [], []
            for e in range(2):
                logits = s[w][i * blk:(i + 1) * blk, e * 2 * blk:(e + 1) * 2 * blk]
                sink2 = sink_ref[2 * p + e] * LOG2E
                m = jnp.maximum(jnp.max(logits, axis=-1, keepdims=True), sink2)
                pe.append(jnp.exp2(logits - m).astype(BF16))
                se.append(jnp.exp2(sink2 - m))
            pp.append(jnp.concatenate(pe, axis=1))
            ss.append(jnp.where(lo, se[0], se[1]))
        probs.append(jnp.concatenate(pp, axis=0))
        sink.append(ss)
    o = [_dot(probs[w], vb_ref[g, kvh]) for w, (g, kvh) in enumerate(groups)]
    for w, (g, kvh) in enumerate(groups):
        for i, p in enumerate(pairs_of(kvh)):
            op = o[w][i * blk:(i + 1) * blk]
            o_ref[g, :, cols[p]] = (op[:, 0:LANES] / (op[:, LANES:] + sink[w][i])).astype(BF16)


def _attention(sinks, freq8, pos4, proj3, kv3, q_col_block):
    bsz, seq = proj3.shape[0], proj3.shape[1]
    blk = WINDOW
    members = 4
    expand, shift = _attn_constants()
    full = lambda a: pl.BlockSpec(a.shape, lambda b, i: (0,) * a.ndim)
    est = (2 * members * (blk * ATT_WIDTH * 2 * 2 + blk * 2 * KV_WIDTH * 2)
           + 2 * members * ATT_KV_HEADS * 4 * blk * 2 * LANES * 2 + members * ATT_WIDTH * blk * 2
           + members * 64 * blk * 4 * blk * 4)
    return pl.pallas_call(
        _attn_kernel,
        grid=(bsz // members, seq // blk),
        in_specs=[pl.BlockSpec(memory_space=pltpu.SMEM),
                  full(freq8), full(expand), full(shift),
                  pl.BlockSpec((members, None, 1, blk), lambda b, i: (b, i, 0, 0)),
                  pl.BlockSpec((members, blk, ATT_WIDTH), lambda b, i: (b, i, q_col_block)),
                  pl.BlockSpec((members, blk, 2 * KV_WIDTH), lambda b, i: (b, i, 0))],
        out_specs=pl.BlockSpec((members, blk, ATT_WIDTH), lambda b, i: (b, i, 0)),
        out_shape=jax.ShapeDtypeStruct((bsz, seq, ATT_WIDTH), BF16),
        scratch_shapes=[pltpu.VMEM((members, ATT_HEADS // 2 * blk, 2 * LANES), BF16),
                        pltpu.VMEM((members, ATT_KV_HEADS, 4 * blk, 2 * LANES), BF16),
                        pltpu.VMEM((members, ATT_KV_HEADS, 4 * blk, 2 * LANES), BF16)],
        compiler_params=pltpu.CompilerParams(
            dimension_semantics=("parallel", "arbitrary"), vmem_limit_bytes=_vmem_limit(est)),
        name="swa_attn",
    )(sinks, freq8, expand, shift, pos4, proj3, kv3)


def _hgrn_constants():
    t = np.arange(CHUNK)[:, None]
    j = np.arange(CHUNK)[None, :]
    tri = (j <= t).astype(np.float32)
    cumsum = np.concatenate([tri, tri], axis=1)
    reduce = np.zeros((DIAG, HG_DIM, LANES), np.float32)
    for d in range(DIAG):
        reduce[d, :, DIAG - 1 - d] = 1.0
    return jnp.asarray(cumsum, BF16), jnp.asarray(reduce.reshape(DIAG * HG_DIM, LANES), BF16)


def _hgrn_kernel(cm_ref, red_ref, lbp_ref, gn_ref, q_ref, f_ref, i_ref, g_ref, o_ref, st_ref, *,
                 tokens):
    @pl.when(pl.program_id(1) == 0)
    def _():
        st_ref[...] = jnp.zeros_like(st_ref)

    lbp = lbp_ref[...]
    e = jnp.exp(lbp - jnp.max(lbp, axis=0, keepdims=True))
    lb = e[0:1, :] / jnp.sum(e, axis=0, keepdims=True)
    half_key_scale = 0.5 * (1.0 - lb)

    row = lax.broadcasted_iota(jnp.int32, (CHUNK, CHUNK), 0)
    col = lax.broadcasted_iota(jnp.int32, (CHUNK, CHUNK), 1)
    same = {w: row // w == col // w for w in (DIAG, 2 * DIAG, 4 * DIAG)}
    groups = CHUNK // DIAG

    heads = [slice(h * HG_DIM, (h + 1) * HG_DIM) for h in range(HG_HEADS)]
    chunks = [slice(c * CHUNK, (c + 1) * CHUNK) for c in range(tokens // CHUNK)]
    per_chunk = []

    for rs in chunks:
        half_q = 0.5 * q_ref[rs, :].astype(F32)
        qs = half_q + half_q * jnp.tanh(half_q)
        key = half_key_scale - half_key_scale * jnp.tanh(0.5 * f_ref[rs, :].astype(F32))
        f = 1.0 - key
        val = i_ref[rs, :]

        b = _dot(cm_ref[...], jnp.concatenate(_split_bf16(jnp.log2(f), 2), axis=0).astype(BF16))
        eb = jnp.exp2(b)
        q_state = (qs * eb).astype(BF16)
        k_state = (key * jnp.exp2(b[CHUNK - 1:CHUNK, :] - b)).astype(BF16)
        decay = eb[CHUNK - 1:CHUNK, :]

        q_lv, k_lv = [], []
        for lv in LEVELS:
            qp, kp = [], []
            for piece in range(CHUNK // lv):
                ps = slice(piece * lv, (piece + 1) * lv)
                zero = jnp.zeros((lv, HG_WIDTH), F32)
                if piece % 2:
                    ref = b[piece * lv - 1:piece * lv, :]
                    qp.append(qs[ps] * jnp.exp2(b[ps] - ref)); kp.append(zero)
                else:
                    ref = b[(piece + 1) * lv - 1:(piece + 1) * lv, :]
                    qp.append(zero); kp.append(key[ps] * jnp.exp2(ref - b[ps]))
            q_lv.append(jnp.concatenate(qp, axis=0).astype(BF16))
            k_lv.append(jnp.concatenate(kp, axis=0).astype(BF16))

        carry = key.reshape(groups, DIAG, HG_WIDTH)
        f3 = f.reshape(groups, DIAG, HG_WIDTH)
        qs_bf16 = qs.astype(BF16)
        diag = [qs_bf16 * key.astype(BF16)]
        for d in range(1, DIAG):
            carry = f3 * pltpu.roll(carry, 1, axis=1)
            diag.append(qs_bf16 * carry.reshape(CHUNK, HG_WIDTH).astype(BF16))

        parts = [[_dot_nt(q_lv[n][:, hs], k_lv[n][:, hs]) for n in range(len(LEVELS))]
                 for hs in heads]
        r = _dot(jnp.concatenate([jnp.concatenate([dg[:, hs] for dg in diag], axis=1)
                                  for hs in heads], axis=0), red_ref[...])
        update = [_dot_tn(val[:, hs], k_state[:, hs]) for hs in heads]
        per_chunk.append((parts, r, update, q_state, val, decay))

    g_all = []
    for parts, r, _, _, _, _ in per_chunk:
        g_chunk = []
        for h in range(HG_HEADS):
            g_diag = pltpu.roll(r[h * CHUNK:(h + 1) * CHUNK], LANES - (DIAG - 1), axis=1,
                                stride=1, stride_axis=0)[:, 0:CHUNK]
            g_mat = jnp.where(same[DIAG], g_diag,
                              jnp.where(same[2 * DIAG], parts[h][2],
                                        jnp.where(same[4 * DIAG], parts[h][1], parts[h][0])))
            g_chunk.append(g_mat.astype(BF16))
        g_all.append(g_chunk)

    state = [st_ref[h] for h in range(HG_HEADS)]
    for rs, g_chunk, (_, _, update, q_state, val, decay) in zip(chunks, g_all, per_chunk):
        for h, hs in enumerate(heads):
            o = _dot_nt(q_state[:, hs], state[h].astype(BF16)) + _dot(g_chunk[h], val[:, hs])
            state[h] = state[h] * decay[:, hs] + update[h]
            y = _rms(o) * gn_ref[...]
            o_ref[rs, hs] = (y * _sigmoid(g_ref[rs, hs].astype(F32))).astype(BF16)
    for h in range(HG_HEADS):
        st_ref[h] = state[h]


def _hgrn(lb_params, gnorm, proj, bsz, seq, col_blocks):
    n = proj.shape[0]
    tokens = 8 * CHUNK
    nt = seq // tokens
    cmat, rmat = _hgrn_constants()
    row = lambda b, i: b * nt + i
    spec = lambda cb: pl.BlockSpec((tokens, HG_WIDTH), lambda b, i: (row(b, i), cb))
    full = lambda a: pl.BlockSpec(a.shape, lambda b, i: (0,) * a.ndim)
    est = 2 * 5 * tokens * HG_WIDTH * 2 + HG_HEADS * HG_DIM * HG_DIM * 4 + 40 * CHUNK * HG_WIDTH * 4
    return pl.pallas_call(
        functools.partial(_hgrn_kernel, tokens=tokens),
        grid=(bsz, nt),
        in_specs=[full(cmat), full(rmat), full(lb_params), full(gnorm),
                  spec(col_blocks[0]), spec(col_blocks[1]), spec(col_blocks[2]), spec(col_blocks[3])],
        out_specs=pl.BlockSpec((tokens, HG_WIDTH), lambda b, i: (row(b, i), 0)),
        out_shape=jax.ShapeDtypeStruct((n, HG_WIDTH), BF16),
        scratch_shapes=[pltpu.VMEM((HG_HEADS, HG_DIM, HG_DIM), F32)],
        compiler_params=pltpu.CompilerParams(
            dimension_semantics=("parallel", "arbitrary"), vmem_limit_bytes=_vmem_limit(est)),
        name="hgrn2",
    )(cmat, rmat, lb_params, gnorm, proj, proj, proj, proj)


def _mix_kernel(x_ref, mod_ref, g_ref, a_ref, hg_ref, ga_ref, gh_ref, wa_ref, wh_ref, wo_ref,
                o_ref, m_ref, *, tc):
    d = x_ref.shape[1]
    for c in range(d // tc):
        cs = slice(c * tc, (c + 1) * tc)
        ya = _dot(a_ref[...], wa_ref[:, cs])
        yh = _dot(hg_ref[...], wh_ref[:, cs])
        merged = (_sigmoid(ga_ref[:, cs].astype(F32)) * ya
                  + _sigmoid(gh_ref[:, cs].astype(F32)) * yh)
        m_ref[:, cs] = merged.astype(BF16)
    y = _dot(m_ref[...], wo_ref[...])
    o_ref[...] = x_ref[...] + mod_ref[2:3, :] * (_rms(y) * g_ref[...])


def _mix_out(x2, mod3, gain, attn, hg, proj, wa, wh, wo, seq, ga_block, gh_block):
    n, d = x2.shape
    tm, tc = 512, 512
    per_batch = seq // tm
    const = lambda shape: pl.BlockSpec(shape, lambda i: (0, 0), pipeline_mode=pl.Buffered(1))
    est = ((wa.size + wh.size + wo.size) * 2 + 2 * 2 * tm * d * 4 + 2 * 2 * tm * d * 2
           + 2 * 2 * tm * ATT_WIDTH * 2 + tm * d * 2 + 3 * tm * d * 4)
    return pl.pallas_call(
        functools.partial(_mix_kernel, tc=tc),
        grid=(n // tm,),
        in_specs=[pl.BlockSpec((tm, d), lambda i: (i, 0)),
                  pl.BlockSpec((None, N_MOD, d), lambda i: (i // per_batch, 0, 0)),
                  pl.BlockSpec((1, d), lambda i: (0, 0)),
                  pl.BlockSpec((tm, ATT_WIDTH), lambda i: (i, 0)),
                  pl.BlockSpec((tm, HG_WIDTH), lambda i: (i, 0)),
                  pl.BlockSpec((tm, d), lambda i: (i, ga_block)),
                  pl.BlockSpec((tm, d), lambda i: (i, gh_block)),
                  const(wa.shape), const(wh.shape), const(wo.shape)],
        out_specs=pl.BlockSpec((tm, d), lambda i: (i, 0)),
        out_shape=jax.ShapeDtypeStruct((n, d), F32),
        scratch_shapes=[pltpu.VMEM((tm, d), BF16)],
        compiler_params=pltpu.CompilerParams(
            dimension_semantics=("parallel",), vmem_limit_bytes=_vmem_limit(est)),
        name="mix_out",
    )(x2, mod3, gain, attn, hg, proj, proj, wa, wh, wo)


def _ffn_up_kernel(x_ref, mod_ref, gpre_ref, wg_ref, wu_ref, wd_ref, o_ref, wd_bf16_ref,
                   h_even_ref, h_odd_ref, *, slices):
    i, j = pl.program_id(0), pl.program_id(1)

    def modulated_norm(rows):
        y = _rms(x_ref[rows, :]) * gpre_ref[...]
        return (y * (1.0 + mod_ref[4:5, :]) + mod_ref[3:4, :]).astype(BF16)

    @pl.when((i == 0) & (j == 0))
    def _():
        h_even_ref[...] = modulated_norm(slice(None))

    def step(h_ref, h_next_ref):
        rows_per = x_ref.shape[0] // slices
        first = pl.multiple_of(jnp.clip(j - 1, 0, slices - 1) * rows_per, rows_per)
        rows = pl.ds(first, rows_per)
        h_next_ref[rows, :] = modulated_norm(rows)
        wd_bf16_ref[...] = wd_ref[...].astype(BF16)

        h = h_ref[...]
        g = _dot(h, wg_ref[...].astype(BF16))
        u = _dot(h, wu_ref[...].astype(BF16))
        o_ref[...] = (g * _sigmoid(g) * u).astype(BF16)

    pl.when(i % 2 == 0)(lambda: step(h_even_ref, h_odd_ref))
    pl.when(i % 2 == 1)(lambda: step(h_odd_ref, h_even_ref))


def _ffn_down_kernel(a_ref, x_ref, mod_ref, gpost_ref, wd_ref, o_ref):
    y = _dot(a_ref[...], wd_ref[...])
    o_ref[...] = x_ref[...] + mod_ref[5:6, :] * (_rms(y) * gpost_ref[...])


def _ffn(x1, mod3, gpre, gpost, w_in, w_down, seq):
    n, d = x1.shape
    hidden = w_down.shape[0]

    tm, th = 1024, 512
    nh = hidden // th
    per_batch = seq // tm
    last = n // tm - 1
    x_tile = lambda i, j: jnp.minimum(i + jnp.where(j > 0, 1, 0), last)
    steps = (n // tm) * nh
    wd_rows = hidden // steps
    assert wd_rows * steps == hidden and wd_rows % 16 == 0
    est = (2 * tm * d * 4 + 2 * tm * d * 2 + 2 * 2 * d * th * 4 + 2 * d * th * 2 + 2 * tm * th * 2
           + 4 * tm * th * 4 + 2 * wd_rows * d * 6)
    act, w_down_bf16 = pl.pallas_call(
        functools.partial(_ffn_up_kernel, slices=8),
        grid=(n // tm, nh),
        in_specs=[pl.BlockSpec((tm, d), lambda i, j: (x_tile(i, j), 0)),
                  pl.BlockSpec((None, N_MOD, d), lambda i, j: (x_tile(i, j) // per_batch, 0, 0)),
                  pl.BlockSpec((1, d), lambda i, j: (0, 0)),
                  pl.BlockSpec((d, th), lambda i, j: (0, j)),
                  pl.BlockSpec((d, th), lambda i, j: (0, nh + j)),
                  pl.BlockSpec((wd_rows, d), lambda i, j: (i * nh + j, 0))],
        out_specs=[pl.BlockSpec((tm, th), lambda i, j: (i, j)),
                   pl.BlockSpec((wd_rows, d), lambda i, j: (i * nh + j, 0))],
        out_shape=[jax.ShapeDtypeStruct((n, hidden), BF16),
                   jax.ShapeDtypeStruct((hidden, d), BF16)],
        scratch_shapes=[pltpu.VMEM((tm, d), BF16), pltpu.VMEM((tm, d), BF16)],
        compiler_params=pltpu.CompilerParams(
            dimension_semantics=("arbitrary", "arbitrary"), vmem_limit_bytes=_vmem_limit(est)),
        name="ffn_up",
    )(x1, mod3, gpre, w_in, w_in, w_down)
    w_down = w_down_bf16

    tm = 256
    per_batch = seq // tm
    est = hidden * d * 2 + 2 * tm * hidden * 2 + 2 * 2 * tm * d * 4 + 2 * tm * d * 4
    return pl.pallas_call(
        _ffn_down_kernel,
        grid=(n // tm,),
        in_specs=[pl.BlockSpec((tm, hidden), lambda i: (i, 0)),
                  pl.BlockSpec((tm, d), lambda i: (i, 0)),
                  pl.BlockSpec((None, N_MOD, d), lambda i: (i // per_batch, 0, 0)),
                  pl.BlockSpec((1, d), lambda i: (0, 0)),
                  pl.BlockSpec((hidden, d), lambda i: (0, 0), pipeline_mode=pl.Buffered(1))],
        out_specs=pl.BlockSpec((tm, d), lambda i: (i, 0)),
        out_shape=jax.ShapeDtypeStruct((n, d), F32),
        compiler_params=pltpu.CompilerParams(
            dimension_semantics=("parallel",), vmem_limit_bytes=_vmem_limit(est)),
        name="ffn_down",
    )(act, x1, mod3, gpost, w_down)


def kernel(x, c, positions, w_ada, b_ada, g_pre_mix, g_post_mix, g_pre_ffn, g_post_ffn, w_in,
           attn_sinks, w_attn_proj, hg_lower_bounds, hg_norm, w_hgrn_proj, w_out, w_ffn_in,
           w_ffn_out):
    bsz, seq, d = x.shape
    n = bsz * seq
    assert d == D_MODEL and w_ada.shape[0] == 1 and seq % 1024 == 0 and bsz % 4 == 0

    ga_block, gh_block = 0, 1
    q_block = 2 * D_MODEL // ATT_WIDTH
    hg_blocks = tuple((2 * D_MODEL + ATT_WIDTH) // HG_WIDTH + k for k in range(4))

    x2 = x.reshape(n, d)
    mod3 = _adaln(c, w_ada[0], b_ada[0]).reshape(bsz, N_MOD, d)

    proj, kv = _in_proj(x2, mod3, g_pre_mix, w_in[0].astype(BF16), seq)

    inv_freq = ROPE_THETA ** (-jnp.arange(0, ROT_DIM, 2, dtype=F32) / ROT_DIM)
    freq8 = jnp.broadcast_to(inv_freq[:, None], (ROT_HALF, WINDOW))
    pos4 = positions.astype(F32).reshape(bsz, seq // WINDOW, 1, WINDOW)
    attn = _attention(attn_sinks[0], freq8, pos4, proj.reshape(bsz, seq, -1),
                      kv.reshape(bsz, seq, -1), q_block).reshape(n, ATT_WIDTH)

    hg = _hgrn(hg_lower_bounds, hg_norm, proj, bsz, seq, hg_blocks)

    x1 = _mix_out(x2, mod3, g_post_mix, attn, hg, proj, w_attn_proj[0].astype(BF16),
                  w_hgrn_proj[0].astype(BF16), w_out[0].astype(BF16), seq, ga_block, gh_block)

    out = _ffn(x1, mod3, g_pre_ffn, g_post_ffn, w_ffn_in[0], w_ffn_out[0], seq)
    return out.reshape(bsz, seq, d)
```

```python
import functools
import math

import numpy as np
import jax
import jax.numpy as jnp
from jax import lax
from jax.experimental import pallas as pl
from jax.experimental.pallas import tpu as pltpu

F32 = jnp.float32
BF16 = jnp.bfloat16

D_MODEL = 2048
ATT_HEADS = 16
ATT_KV_HEADS = 2
HEAD_DIM = 64
ATT_WIDTH = ATT_HEADS * HEAD_DIM
KV_WIDTH = ATT_KV_HEADS * HEAD_DIM
WINDOW = 128
ROT_DIM = HEAD_DIM // 4
ROT_HALF = ROT_DIM // 2
ROPE_THETA = 500000.0
HG_HEADS = 8
HG_DIM = 128
HG_WIDTH = HG_HEADS * HG_DIM
N_MOD = 6
EPS = 1e-6
LOG2E = math.log2(math.e)
MASK_BIAS = -1e30

LANES = 128
SUBLANES = 8
V7X_VMEM_BYTES = 64 * 1024 * 1024

CHUNK = 64
DIAG = SUBLANES
LEVELS = (32, 16, 8)


def _vmem_limit(estimate_bytes):
    return int(min(estimate_bytes * 3 // 2, V7X_VMEM_BYTES - 8 * 1024 * 1024))


def _dot(a, b):
    return jnp.dot(a, b, preferred_element_type=F32)


def _dot_nt(a, b):
    return lax.dot_general(a, b, (((1,), (1,)), ((), ())), preferred_element_type=F32)


def _dot_tn(a, b):
    return lax.dot_general(a, b, (((0,), (0,)), ((), ())), preferred_element_type=F32)


def _rms(t):
    return t * lax.rsqrt(jnp.mean(t * t, axis=-1, keepdims=True) + EPS)


def _sigmoid(t):
    return 0.5 + 0.5 * jnp.tanh(0.5 * t)


def _split_bf16(x, parts):
    out = []
    for _ in range(parts - 1):
        p = x.astype(BF16).astype(F32)
        out.append(p)
        x = x - p
    out.append(x.astype(BF16).astype(F32))
    return out


def _adaln_kernel(c_ref, w_ref, b_ref, o_ref):
    bsz = c_ref.shape[0]
    c_hi, c_lo = _split_bf16(c_ref[...], 2)
    w_hi, w_lo = _split_bf16(w_ref[...], 2)
    by_hi = _dot(jnp.concatenate([c_hi, c_lo], axis=0).astype(BF16), w_hi.astype(BF16))
    o_ref[...] = (by_hi[0:bsz] + by_hi[bsz:] + _dot(c_hi.astype(BF16), w_lo.astype(BF16))
                  + b_ref[...])


def _adaln(c, w, b):
    bsz, d = c.shape
    n = w.shape[1]
    tn = 1024
    est = 2 * (d * tn * 4) + 2 * bsz * d * 4 + 4 * bsz * tn * 4 + 3 * d * tn * 4
    return pl.pallas_call(
        _adaln_kernel,
        grid=(n // tn,),
        in_specs=[pl.BlockSpec((bsz, d), lambda j: (0, 0)),
                  pl.BlockSpec((d, tn), lambda j: (0, j)),
                  pl.BlockSpec((1, tn), lambda j: (0, j))],
        out_specs=pl.BlockSpec((bsz, tn), lambda j: (0, j)),
        out_shape=jax.ShapeDtypeStruct((bsz, n), F32),
        compiler_params=pltpu.CompilerParams(
            dimension_semantics=("arbitrary",), vmem_limit_bytes=_vmem_limit(est)),
        name="adaln_mod",
    )(c, w, b.reshape(1, n))


def _in_proj_kernel(x_ref, mod_ref, g_ref, w_ref, wkv_ref, *refs, slices, n_cast):
    cast_src, (o_ref, okv_ref) = refs[:n_cast], refs[n_cast:n_cast + 2]
    cast_dst, (h_even_ref, h_odd_ref) = refs[n_cast + 2:2 * n_cast + 2], refs[2 * n_cast + 2:]
    i, j = pl.program_id(0), pl.program_id(1)

    def modulated_norm(rows):
        y = _rms(x_ref[rows, :]) * g_ref[...]
        return (y * (1.0 + mod_ref[1:2, :]) + mod_ref[0:1, :]).astype(BF16)

    @pl.when((i == 0) & (j == 0))
    def _():
        h_even_ref[...] = modulated_norm(slice(None))

    def step(h_ref, h_next_ref):
        @pl.when(j == 0)
        def _():
            okv_ref[...] = _dot(h_ref[...], wkv_ref[...]).astype(BF16)

        rows_per = x_ref.shape[0] // slices
        first = pl.multiple_of(jnp.clip(j - 1, 0, slices - 1) * rows_per, rows_per)
        rows = pl.ds(first, rows_per)
        h_next_ref[rows, :] = modulated_norm(rows)
        for src_ref, dst_ref in zip(cast_src, cast_dst):
            dst_ref[...] = src_ref[...].astype(BF16)
        o_ref[...] = _dot(h_ref[...], w_ref[...]).astype(BF16)

    pl.when(i % 2 == 0)(lambda: step(h_even_ref, h_odd_ref))
    pl.when(i % 2 == 1)(lambda: step(h_odd_ref, h_even_ref))


def _in_proj(x2, mod3, gain, w_in, seq, later_weights):
    n, d = x2.shape
    tm, tn = 1024, 1024
    per_batch = seq // tm
    o_kv = ATT_WIDTH
    o_qh = o_kv + 2 * KV_WIDTH
    o_ga = o_qh + 4 * HG_WIDTH
    cols = w_in.shape[1] - 2 * KV_WIDTH
    n_gate = 2 * D_MODEL // tn
    n_q = ATT_WIDTH // tn

    def src_col(j):
        unit = tn // LANES
        lane_tile = jnp.where(j < n_gate, o_ga // LANES + unit * j,
                              jnp.where(j < n_gate + n_q, unit * (j - n_gate),
                                        o_qh // LANES + unit * (j - n_gate - n_q)))
        return lane_tile * LANES

    last = n // tm - 1
    x_tile = lambda i, j: jnp.minimum(i + jnp.where(j > 0, 1, 0), last)

    n_col = cols // tn
    cast_specs, cast_bytes = [], 0
    for w, (br, bc) in later_weights:
        rb, cb = w.shape[0] // br, w.shape[1] // bc
        assert rb * br == w.shape[0] and cb * bc == w.shape[1] and rb * cb <= (n // tm) * n_col
        def block(i, j, rb=rb, cb=cb):
            k = jnp.minimum(i * n_col + j, rb * cb - 1)
            return k // cb, k % cb
        cast_specs.append(pl.BlockSpec((br, bc), block))
        cast_bytes += 2 * br * bc * 6

    est = (2 * tm * d * 4 + 2 * tm * d * 2 + 2 * d * tn * 2 + 2 * tm * tn * 2
           + 2 * d * KV_WIDTH * 2 * 2 + 2 * tm * 2 * KV_WIDTH * 2 + tm * tn * 4 + tm * d * 4 // 8
           + cast_bytes)
    return pl.pallas_call(
        functools.partial(_in_proj_kernel, slices=8, n_cast=len(later_weights)),
        grid=(n // tm, cols // tn),
        in_specs=[pl.BlockSpec((tm, d), lambda i, j: (x_tile(i, j), 0)),
                  pl.BlockSpec((None, N_MOD, d), lambda i, j: (x_tile(i, j) // per_batch, 0, 0)),
                  pl.BlockSpec((1, d), lambda i, j: (0, 0)),
                  pl.BlockSpec((pl.Element(d), pl.Element(tn)), lambda i, j: (0, src_col(j))),
                  pl.BlockSpec((d, 2 * KV_WIDTH), lambda i, j: (0, o_kv // (2 * KV_WIDTH)))]
                 + cast_specs,
        out_specs=[pl.BlockSpec((tm, tn), lambda i, j: (i, j)),
                   pl.BlockSpec((tm, 2 * KV_WIDTH), lambda i, j: (i, 0))] + cast_specs,
        out_shape=[jax.ShapeDtypeStruct((n, cols), BF16),
                   jax.ShapeDtypeStruct((n, 2 * KV_WIDTH), BF16)]
                  + [jax.ShapeDtypeStruct(w.shape, BF16) for w, _ in later_weights],
        scratch_shapes=[pltpu.VMEM((tm, d), BF16), pltpu.VMEM((tm, d), BF16)],
        compiler_params=pltpu.CompilerParams(
            dimension_semantics=("arbitrary", "arbitrary"), vmem_limit_bytes=_vmem_limit(est)),
        name="in_proj",
    )(x2, mod3, gain, w_in, w_in, *[w for w, _ in later_weights])


def _attn_constants():
    lane = np.arange(LANES)
    dim = lane % HEAD_DIM
    expand = np.zeros((2, 3, ROT_HALF, 3, LANES), np.float32)
    for j in range(ROT_HALF):
        expand[0, :, j, 0, (dim < ROT_DIM) & (dim % ROT_HALF == j)] = 1.0
        expand[1, :, j, 1, (dim < ROT_HALF) & (dim == j)] = -1.0
        expand[1, :, j, 2, (dim >= ROT_HALF) & (dim < ROT_DIM) & (dim - ROT_HALF == j)] = 1.0
    expand = expand.reshape(6 * ROT_HALF, 3 * LANES)
    shift = np.zeros((LANES, 2 * LANES), np.float32)
    for l in range(LANES):
        if l + ROT_HALF < LANES:
            shift[l + ROT_HALF, l] = 1.0
        if l - ROT_HALF >= 0:
            shift[l - ROT_HALF, LANES + l] = 1.0
    return jnp.asarray(expand, BF16), jnp.asarray(shift, BF16)


def _attn_kernel(sink_ref, freq_ref, expand_ref, shift_ref, pos_ref, q_ref, kv_ref, o_ref,
                 qb_ref, kb_ref, vb_ref):
    members = q_ref.shape[0]
    nblk = pl.program_id(1)
    parity = nblk % 2
    blk = WINDOW
    lane = lax.broadcasted_iota(jnp.int32, (blk, LANES), 1)
    lo = lane < HEAD_DIM
    dim = lane % HEAD_DIM
    kj = lax.broadcasted_iota(jnp.int32, (blk, blk), 0)
    qi = lax.broadcasted_iota(jnp.int32, (blk, blk), 1)

    @pl.when(nblk == 0)
    def _():
        kb_ref[...] = jnp.zeros_like(kb_ref)
        vb_ref[...] = jnp.zeros_like(vb_ref)
        eye = jnp.where(kj == qi, 1.0, 0.0).astype(BF16)
        ones_lo = jnp.where(lo, 1.0, 0.0).astype(BF16)
        ones_hi = jnp.where(lo, 0.0, 1.0).astype(BF16)
        for g in range(members):
            for p in range(ATT_HEADS // 2):
                qb_ref[g, p * blk:(p + 1) * blk, LANES:] = eye
            for kvh in range(ATT_KV_HEADS):
                for quarter in range(4):
                    rows = slice(quarter * blk, (quarter + 1) * blk)
                    vb_ref[g, kvh, rows, LANES:] = ones_hi if quarter >= 2 else ones_lo

    def rope(x_bf16, tables):
        t_cos, t_lo, t_hi = tables
        ud = _dot(x_bf16, shift_ref[...])
        return x_bf16.astype(F32) * t_cos + ud[:, 0:LANES] * t_lo + ud[:, LANES:] * t_hi

    k_tabs, q_tabs = [], []
    for g in range(members):
        ang = freq_ref[...] * pos_ref[g]
        parts = _split_bf16(jnp.cos(ang), 3) + _split_bf16(jnp.sin(ang), 3)
        tabs = _dot_tn(jnp.concatenate(parts, axis=0).astype(BF16), expand_ref[...])
        kt = (tabs[:, 0:LANES] + jnp.where(dim < ROT_DIM, 0.0, 1.0),
              tabs[:, LANES:2 * LANES], tabs[:, 2 * LANES:3 * LANES])
        k_tabs.append(kt)
        q_tabs.append(tuple(t * (HEAD_DIM ** -0.5 * LOG2E) for t in kt))

    bias = []
    for half in range(2):
        is_cur = parity == half
        valid = (is_cur & (kj <= qi)) | ((~is_cur) & (nblk > 0) & (kj > qi))
        bias.append(jnp.where(valid, 0.0, MASK_BIAS).astype(BF16))
    for g in range(members):
        k = rope(kv_ref[g, :, 0:KV_WIDTH], k_tabs[g])
        v = kv_ref[g, :, KV_WIDTH:].astype(F32)
        k_sw = pltpu.roll(k, HEAD_DIM, axis=1)
        v_sw = pltpu.roll(v, HEAD_DIM, axis=1)
        for kvh in range(ATT_KV_HEADS):
            for slot in range(2):
                ks = (k, k_sw)[kvh ^ slot]
                vs = (v, v_sw)[kvh ^ slot]
                keep = lo if slot == 0 else ~lo
                rows = pl.ds(pl.multiple_of((2 * slot + parity) * blk, blk), blk)
                kb_ref[g, kvh, rows, 0:LANES] = jnp.where(keep, ks, 0.0).astype(BF16)
                vb_ref[g, kvh, rows, 0:LANES] = jnp.where(keep, vs, 0.0).astype(BF16)
                for half in range(2):
                    rows = slice((2 * slot + half) * blk, (2 * slot + half + 1) * blk)
                    kb_ref[g, kvh, rows, LANES:] = bias[half]

    pairs_per_kv = ATT_HEADS // ATT_KV_HEADS // 2
    cols = [slice(p * LANES, (p + 1) * LANES) for p in range(ATT_HEADS // 2)]
    groups = [(g, kvh) for g in range(members) for kvh in range(ATT_KV_HEADS)]
    pairs_of = lambda kvh: range(kvh * pairs_per_kv, (kvh + 1) * pairs_per_kv)
    for g in range(members):
        stacked = jnp.concatenate([q_ref[g, :, cs] for cs in cols], axis=0)
        tables = tuple(jnp.concatenate([t] * len(cols), axis=0) for t in q_tabs[g])
        qb_ref[g, :, 0:LANES] = rope(stacked, tables).astype(BF16)
    rows_of = lambda kvh: slice(kvh * pairs_per_kv * blk, (kvh + 1) * pairs_per_kv * blk)
    s = [_dot_nt(qb_ref[g, rows_of(kvh), :], kb_ref[g, kvh]) for g, kvh in groups]
    probs, sink = [], []
    for w, (g, kvh) in enumerate(groups):
        pp, ss = [], []
        for i, p in enumerate(pairs_of(kvh)):
            pe, se = [], []
            for e in range(2):
                logits = s[w][i * blk:(i + 1) * blk, e * 2 * blk:(e + 1) * 2 * blk]
                sink2 = sink_ref[2 * p + e] * LOG2E
                m = jnp.maximum(jnp.max(logits, axis=-1, keepdims=True), sink2)
                pe.append(jnp.exp2(logits - m).astype(BF16))
                se.append(jnp.exp2(sink2 - m))
            pp.append(jnp.concatenate(pe, axis=1))
            ss.append(jnp.where(lo, se[0], se[1]))
        probs.append(jnp.concatenate(pp, axis=0))
        sink.append(ss)
    o = [_dot(probs[w], vb_ref[g, kvh]) for w, (g, kvh) in enumerate(groups)]
    for w, (g, kvh) in enumerate(groups):
        for i, p in enumerate(pairs_of(kvh)):
            op = o[w][i * blk:(i + 1) * blk]
            o_ref[g, :, cols[p]] = (op[:, 0:LANES] / (op[:, LANES:] + sink[w][i])).astype(BF16)


def _attention(sinks, freq8, pos4, proj3, kv3, q_col_block):
    bsz, seq = proj3.shape[0], proj3.shape[1]
    blk = WINDOW
    members = 4
    expand, shift = _attn_constants()
    full = lambda a: pl.BlockSpec(a.shape, lambda b, i: (0,) * a.ndim)
    est = (2 * members * (blk * ATT_WIDTH * 2 * 2 + blk * 2 * KV_WIDTH * 2)
           + 2 * members * ATT_KV_HEADS * 4 * blk * 2 * LANES * 2 + members * ATT_WIDTH * blk * 2
           + members * 64 * blk * 4 * blk * 4)
    return pl.pallas_call(
        _attn_kernel,
        grid=(bsz // members, seq // blk),
        in_specs=[pl.BlockSpec(memory_space=pltpu.SMEM),
                  full(freq8), full(expand), full(shift),
                  pl.BlockSpec((members, None, 1, blk), lambda b, i: (b, i, 0, 0)),
                  pl.BlockSpec((members, blk, ATT_WIDTH), lambda b, i: (b, i, q_col_block)),
                  pl.BlockSpec((members, blk, 2 * KV_WIDTH), lambda b, i: (b, i, 0))],
        out_specs=pl.BlockSpec((members, blk, ATT_WIDTH), lambda b, i: (b, i, 0)),
        out_shape=jax.ShapeDtypeStruct((bsz, seq, ATT_WIDTH), BF16),
        scratch_shapes=[pltpu.VMEM((members, ATT_HEADS // 2 * blk, 2 * LANES), BF16),
                        pltpu.VMEM((members, ATT_KV_HEADS, 4 * blk, 2 * LANES), BF16),
                        pltpu.VMEM((members, ATT_KV_HEADS, 4 * blk, 2 * LANES), BF16)],
        compiler_params=pltpu.CompilerParams(
            dimension_semantics=("parallel", "arbitrary"), vmem_limit_bytes=_vmem_limit(est)),
        name="swa_attn",
    )(sinks, freq8, expand, shift, pos4, proj3, kv3)


def _hgrn_constants():
    t = np.arange(CHUNK)[:, None]
    j = np.arange(CHUNK)[None, :]
    tri = (j <= t).astype(np.float32)
    cumsum = np.concatenate([tri, tri], axis=1)
    reduce = np.zeros((DIAG, HG_DIM, LANES), np.float32)
    for d in range(DIAG):
        reduce[d, :, DIAG - 1 - d] = 1.0
    return jnp.asarray(cumsum, BF16), jnp.asarray(reduce.reshape(DIAG * HG_DIM, LANES), BF16)


def _hgrn_kernel(cm_ref, red_ref, lbp_ref, gn_ref, q_ref, f_ref, i_ref, g_ref, o_ref, st_ref, *,
                 tokens):
    @pl.when(pl.program_id(1) == 0)
    def _():
        st_ref[...] = jnp.zeros_like(st_ref)

    lbp = lbp_ref[...]
    e = jnp.exp(lbp - jnp.max(lbp, axis=0, keepdims=True))
    lb = e[0:1, :] / jnp.sum(e, axis=0, keepdims=True)
    half_key_scale = 0.5 * (1.0 - lb)

    row = lax.broadcasted_iota(jnp.int32, (CHUNK, CHUNK), 0)
    col = lax.broadcasted_iota(jnp.int32, (CHUNK, CHUNK), 1)
    same = {w: row // w == col // w for w in (DIAG, 2 * DIAG, 4 * DIAG)}
    groups = CHUNK // DIAG

    heads = [slice(h * HG_DIM, (h + 1) * HG_DIM) for h in range(HG_HEADS)]
    chunks = [slice(c * CHUNK, (c + 1) * CHUNK) for c in range(tokens // CHUNK)]
    per_chunk = []

    for rs in chunks:
        half_q = 0.5 * q_ref[rs, :].astype(F32)
        qs = half_q + half_q * jnp.tanh(half_q)
        key = half_key_scale - half_key_scale * jnp.tanh(0.5 * f_ref[rs, :].astype(F32))
        f = 1.0 - key
        val = i_ref[rs, :]

        b = _dot(cm_ref[...], jnp.concatenate(_split_bf16(jnp.log2(f), 2), axis=0).astype(BF16))
        eb = jnp.exp2(b)
        q_state = (qs * eb).astype(BF16)
        k_state = (key * jnp.exp2(b[CHUNK - 1:CHUNK, :] - b)).astype(BF16)
        decay = eb[CHUNK - 1:CHUNK, :]

        q_lv, k_lv = [], []
        for lv in LEVELS:
            qp, kp = [], []
            for piece in range(CHUNK // lv):
                ps = slice(piece * lv, (piece + 1) * lv)
                zero = jnp.zeros((lv, HG_WIDTH), F32)
                if piece % 2:
                    ref = b[piece * lv - 1:piece * lv, :]
                    qp.append(qs[ps] * jnp.exp2(b[ps] - ref)); kp.append(zero)
                else:
                    ref = b[(piece + 1) * lv - 1:(piece + 1) * lv, :]
                    qp.append(zero); kp.append(key[ps] * jnp.exp2(ref - b[ps]))
            q_lv.append(jnp.concatenate(qp, axis=0).astype(BF16))
            k_lv.append(jnp.concatenate(kp, axis=0).astype(BF16))

        carry = key.reshape(groups, DIAG, HG_WIDTH)
        f3 = f.reshape(groups, DIAG, HG_WIDTH)
        qs_bf16 = qs.astype(BF16)
        diag = [qs_bf16 * key.astype(BF16)]
        for d in range(1, DIAG):
            carry = f3 * pltpu.roll(carry, 1, axis=1)
            diag.append(qs_bf16 * carry.reshape(CHUNK, HG_WIDTH).astype(BF16))

        parts = [[_dot_nt(q_lv[n][:, hs], k_lv[n][:, hs]) for n in range(len(LEVELS))]
                 for hs in heads]
        r = _dot(jnp.concatenate([jnp.concatenate([dg[:, hs] for dg in diag], axis=1)
                                  for hs in heads], axis=0), red_ref[...])
        update = [_dot_tn(val[:, hs], k_state[:, hs]) for hs in heads]
        per_chunk.append((parts, r, update, q_state, val, decay))

    g_all = []
    for parts, r, _, _, _, _ in per_chunk:
        g_chunk = []
        for h in range(HG_HEADS):
            g_diag = pltpu.roll(r[h * CHUNK:(h + 1) * CHUNK], LANES - (DIAG - 1), axis=1,
                                stride=1, stride_axis=0)[:, 0:CHUNK]
            g_mat = jnp.where(same[DIAG], g_diag,
                              jnp.where(same[2 * DIAG], parts[h][2],
                                        jnp.where(same[4 * DIAG], parts[h][1], parts[h][0])))
            g_chunk.append(g_mat.astype(BF16))
        g_all.append(g_chunk)

    state = [st_ref[h] for h in range(HG_HEADS)]
    for rs, g_chunk, (_, _, update, q_state, val, decay) in zip(chunks, g_all, per_chunk):
        for h, hs in enumerate(heads):
            o = _dot_nt(q_state[:, hs], state[h].astype(BF16)) + _dot(g_chunk[h], val[:, hs])
            state[h] = state[h] * decay[:, hs] + update[h]
            y = _rms(o) * gn_ref[...]
            o_ref[rs, hs] = (y * _sigmoid(g_ref[rs, hs].astype(F32))).astype(BF16)
    for h in range(HG_HEADS):
        st_ref[h] = state[h]


def _hgrn(lb_params, gnorm, proj, bsz, seq, col_blocks):
    n = proj.shape[0]
    tokens = 8 * CHUNK
    nt = seq // tokens
    cmat, rmat = _hgrn_constants()
    row = lambda b, i: b * nt + i
    spec = lambda cb: pl.BlockSpec((tokens, HG_WIDTH), lambda b, i: (row(b, i), cb))
    full = lambda a: pl.BlockSpec(a.shape, lambda b, i: (0,) * a.ndim)
    est = 2 * 5 * tokens * HG_WIDTH * 2 + HG_HEADS * HG_DIM * HG_DIM * 4 + 40 * CHUNK * HG_WIDTH * 4
    return pl.pallas_call(
        functools.partial(_hgrn_kernel, tokens=tokens),
        grid=(bsz, nt),
        in_specs=[full(cmat), full(rmat), full(lb_params), full(gnorm),
                  spec(col_blocks[0]), spec(col_blocks[1]), spec(col_blocks[2]), spec(col_blocks[3])],
        out_specs=pl.BlockSpec((tokens, HG_WIDTH), lambda b, i: (row(b, i), 0)),
        out_shape=jax.ShapeDtypeStruct((n, HG_WIDTH), BF16),
        scratch_shapes=[pltpu.VMEM((HG_HEADS, HG_DIM, HG_DIM), F32)],
        compiler_params=pltpu.CompilerParams(
            dimension_semantics=("parallel", "arbitrary"), vmem_limit_bytes=_vmem_limit(est)),
        name="hgrn2",
    )(cmat, rmat, lb_params, gnorm, proj, proj, proj, proj)


def _mix_kernel(x_ref, mod_ref, g_ref, a_ref, hg_ref, ga_ref, gh_ref, wa_ref, wh_ref, wo_ref,
                o_ref, m_ref, *, tc):
    d = x_ref.shape[1]
    for c in range(d // tc):
        cs = slice(c * tc, (c + 1) * tc)
        ya = _dot(a_ref[...], wa_ref[:, cs])
        yh = _dot(hg_ref[...], wh_ref[:, cs])
        merged = (_sigmoid(ga_ref[:, cs].astype(F32)) * ya
                  + _sigmoid(gh_ref[:, cs].astype(F32)) * yh)
        m_ref[:, cs] = merged.astype(BF16)
    y = _dot(m_ref[...], wo_ref[...])
    o_ref[...] = x_ref[...] + mod_ref[2:3, :] * (_rms(y) * g_ref[...])


def _mix_out(x2, mod3, gain, attn, hg, proj, wa, wh, wo, seq, ga_block, gh_block):
    n, d = x2.shape
    tm, tc = 512, 512
    per_batch = seq // tm
    const = lambda shape: pl.BlockSpec(shape, lambda i: (0, 0), pipeline_mode=pl.Buffered(1))
    est = ((wa.size + wh.size + wo.size) * 2 + 2 * 2 * tm * d * 4 + 2 * 2 * tm * d * 2
           + 2 * 2 * tm * ATT_WIDTH * 2 + tm * d * 2 + 3 * tm * d * 4)
    return pl.pallas_call(
        functools.partial(_mix_kernel, tc=tc),
        grid=(n // tm,),
        in_specs=[pl.BlockSpec((tm, d), lambda i: (i, 0)),
                  pl.BlockSpec((None, N_MOD, d), lambda i: (i // per_batch, 0, 0)),
                  pl.BlockSpec((1, d), lambda i: (0, 0)),
                  pl.BlockSpec((tm, ATT_WIDTH), lambda i: (i, 0)),
                  pl.BlockSpec((tm, HG_WIDTH), lambda i: (i, 0)),
                  pl.BlockSpec((tm, d), lambda i: (i, ga_block)),
                  pl.BlockSpec((tm, d), lambda i: (i, gh_block)),
                  const(wa.shape), const(wh.shape), const(wo.shape)],
        out_specs=pl.BlockSpec((tm, d), lambda i: (i, 0)),
        out_shape=jax.ShapeDtypeStruct((n, d), F32),
        scratch_shapes=[pltpu.VMEM((tm, d), BF16)],
        compiler_params=pltpu.CompilerParams(
            dimension_semantics=("parallel",), vmem_limit_bytes=_vmem_limit(est)),
        name="mix_out",
    )(x2, mod3, gain, attn, hg, proj, proj, wa, wh, wo)


def _ffn_up_kernel(x_ref, mod_ref, gpre_ref, wg_ref, wu_ref, wd_ref, o_ref, wd_bf16_ref,
                   h_even_ref, h_odd_ref, *, slices):
    i, j = pl.program_id(0), pl.program_id(1)

    def modulated_norm(rows):
        y = _rms(x_ref[rows, :]) * gpre_ref[...]
        return (y * (1.0 + mod_ref[4:5, :]) + mod_ref[3:4, :]).astype(BF16)

    @pl.when((i == 0) & (j == 0))
    def _():
        h_even_ref[...] = modulated_norm(slice(None))

    def step(h_ref, h_next_ref):
        rows_per = x_ref.shape[0] // slices
        first = pl.multiple_of(jnp.clip(j - 1, 0, slices - 1) * rows_per, rows_per)
        rows = pl.ds(first, rows_per)
        h_next_ref[rows, :] = modulated_norm(rows)
        wd_bf16_ref[...] = wd_ref[...].astype(BF16)

        h = h_ref[...]
        g = _dot(h, wg_ref[...])
        u = _dot(h, wu_ref[...])
        o_ref[...] = (g * _sigmoid(g) * u).astype(BF16)

    pl.when(i % 2 == 0)(lambda: step(h_even_ref, h_odd_ref))
    pl.when(i % 2 == 1)(lambda: step(h_odd_ref, h_even_ref))


def _ffn_down_kernel(a_ref, x_ref, mod_ref, gpost_ref, wd_ref, o_ref):
    y = _dot(a_ref[...], wd_ref[...])
    o_ref[...] = x_ref[...] + mod_ref[5:6, :] * (_rms(y) * gpost_ref[...])


def _ffn(x1, mod3, gpre, gpost, w_in, w_down, seq):
    n, d = x1.shape
    hidden = w_down.shape[0]

    tm, th = 1024, 512
    nh = hidden // th
    per_batch = seq // tm
    last = n // tm - 1
    x_tile = lambda i, j: jnp.minimum(i + jnp.where(j > 0, 1, 0), last)
    steps = (n // tm) * nh
    wd_rows = hidden // steps
    assert wd_rows * steps == hidden and wd_rows % 16 == 0
    est = (2 * tm * d * 4 + 2 * tm * d * 2 + 2 * 2 * d * th * 2 + 2 * tm * th * 2
           + 4 * tm * th * 4 + 2 * wd_rows * d * 6)
    act, w_down_bf16 = pl.pallas_call(
        functools.partial(_ffn_up_kernel, slices=8),
        grid=(n // tm, nh),
        in_specs=[pl.BlockSpec((tm, d), lambda i, j: (x_tile(i, j), 0)),
                  pl.BlockSpec((None, N_MOD, d), lambda i, j: (x_tile(i, j) // per_batch, 0, 0)),
                  pl.BlockSpec((1, d), lambda i, j: (0, 0)),
                  pl.BlockSpec((d, th), lambda i, j: (0, j)),
                  pl.BlockSpec((d, th), lambda i, j: (0, nh + j)),
                  pl.BlockSpec((wd_rows, d), lambda i, j: (i * nh + j, 0))],
        out_specs=[pl.BlockSpec((tm, th), lambda i, j: (i, j)),
                   pl.BlockSpec((wd_rows, d), lambda i, j: (i * nh + j, 0))],
        out_shape=[jax.ShapeDtypeStruct((n, hidden), BF16),
                   jax.ShapeDtypeStruct((hidden, d), BF16)],
        scratch_shapes=[pltpu.VMEM((tm, d), BF16), pltpu.VMEM((tm, d), BF16)],
        compiler_params=pltpu.CompilerParams(
            dimension_semantics=("arbitrary", "arbitrary"), vmem_limit_bytes=_vmem_limit(est)),
        name="ffn_up",
    )(x1, mod3, gpre, w_in, w_in, w_down)
    w_down = w_down_bf16

    tm = 256
    per_batch = seq // tm
    est = hidden * d * 2 + 2 * tm * hidden * 2 + 2 * 2 * tm * d * 4 + 2 * tm * d * 4
    return pl.pallas_call(
        _ffn_down_kernel,
        grid=(n // tm,),
        in_specs=[pl.BlockSpec((tm, hidden), lambda i: (i, 0)),
                  pl.BlockSpec((tm, d), lambda i: (i, 0)),
                  pl.BlockSpec((None, N_MOD, d), lambda i: (i // per_batch, 0, 0)),
                  pl.BlockSpec((1, d), lambda i: (0, 0)),
                  pl.BlockSpec((hidden, d), lambda i: (0, 0), pipeline_mode=pl.Buffered(1))],
        out_specs=pl.BlockSpec((tm, d), lambda i: (i, 0)),
        out_shape=jax.ShapeDtypeStruct((n, d), F32),
        compiler_params=pltpu.CompilerParams(
            dimension_semantics=("parallel",), vmem_limit_bytes=_vmem_limit(est)),
        name="ffn_down",
    )(act, x1, mod3, gpost, w_down)


def kernel(x, c, positions, w_ada, b_ada, g_pre_mix, g_post_mix, g_pre_ffn, g_post_ffn, w_in,
           attn_sinks, w_attn_proj, hg_lower_bounds, hg_norm, w_hgrn_proj, w_out, w_ffn_in,
           w_ffn_out):
    bsz, seq, d = x.shape
    n = bsz * seq
    assert d == D_MODEL and w_ada.shape[0] == 1 and seq % 1024 == 0 and bsz % 4 == 0

    ga_block, gh_block = 0, 1
    q_block = 2 * D_MODEL // ATT_WIDTH
    hg_blocks = tuple((2 * D_MODEL + ATT_WIDTH) // HG_WIDTH + k for k in range(4))

    x2 = x.reshape(n, d)
    mod3 = _adaln(c, w_ada[0], b_ada[0]).reshape(bsz, N_MOD, d)

    proj, kv, wa, wh, wo, w_up = _in_proj(
        x2, mod3, g_pre_mix, w_in[0].astype(BF16), seq,
        ((w_attn_proj[0], (16, D_MODEL)), (w_hgrn_proj[0], (16, D_MODEL)), (w_out[0], (16, D_MODEL)),
         (w_ffn_in[0], (D_MODEL, LANES))))

    inv_freq = ROPE_THETA ** (-jnp.arange(0, ROT_DIM, 2, dtype=F32) / ROT_DIM)
    freq8 = jnp.broadcast_to(inv_freq[:, None], (ROT_HALF, WINDOW))
    pos4 = positions.astype(F32).reshape(bsz, seq // WINDOW, 1, WINDOW)
    attn = _attention(attn_sinks[0], freq8, pos4, proj.reshape(bsz, seq, -1),
                      kv.reshape(bsz, seq, -1), q_block).reshape(n, ATT_WIDTH)

    hg = _hgrn(hg_lower_bounds, hg_norm, proj, bsz, seq, hg_blocks)

    x1 = _mix_out(x2, mod3, g_post_mix, attn, hg, proj, wa, wh, wo, seq, ga_block, gh_block)

    out = _ffn(x1, mod3, g_pre_ffn, g_post_ffn, w_up, w_ffn_out[0], seq)
    return out.reshape(bsz, seq, d)
```

```python
import functools
import math

import numpy as np
import jax
import jax.numpy as jnp
from jax import lax
from jax.experimental import pallas as pl
from jax.experimental.pallas import tpu as pltpu

F32 = jnp.float32
BF16 = jnp.bfloat16

D_MODEL = 2048
ATT_HEADS = 16
ATT_KV_HEADS = 2
HEAD_DIM = 64
ATT_WIDTH = ATT_HEADS * HEAD_DIM
KV_WIDTH = ATT_KV_HEADS * HEAD_DIM
WINDOW = 128
ROT_DIM = HEAD_DIM // 4
ROT_HALF = ROT_DIM // 2
ROPE_THETA = 500000.0
HG_HEADS = 8
HG_DIM = 128
HG_WIDTH = HG_HEADS * HG_DIM
N_MOD = 6
EPS = 1e-6
LOG2E = math.log2(math.e)
MASK_BIAS = -1e30

LANES = 128
SUBLANES = 8
V7X_VMEM_BYTES = 64 * 1024 * 1024

CHUNK = 64
DIAG = SUBLANES
LEVELS = (32, 16, 8)


def _vmem_limit(estimate_bytes):
    return int(min(estimate_bytes * 3 // 2, V7X_VMEM_BYTES - 8 * 1024 * 1024))


def _dot(a, b):
    return jnp.dot(a, b, preferred_element_type=F32)


def _dot_nt(a, b):
    return lax.dot_general(a, b, (((1,), (1,)), ((), ())), preferred_element_type=F32)


def _dot_tn(a, b):
    return lax.dot_general(a, b, (((0,), (0,)), ((), ())), preferred_element_type=F32)


def _rms(t):
    return t * lax.rsqrt(jnp.mean(t * t, axis=-1, keepdims=True) + EPS)


def _gated_norm_residual(x, y, gate, gain):
    return x + _rms(y) * (gate * gain)


def _sigmoid(t):
    return 0.5 + 0.5 * jnp.tanh(0.5 * t)


def _split_bf16(x, parts):
    out = []
    for _ in range(parts - 1):
        p = x.astype(BF16).astype(F32)
        out.append(p)
        x = x - p
    out.append(x.astype(BF16).astype(F32))
    return out


def _adaln_kernel(c_ref, w_ref, b_ref, o_ref):
    bsz = c_ref.shape[0]
    c_hi, c_lo = _split_bf16(c_ref[...], 2)
    w_hi, w_lo = _split_bf16(w_ref[...], 2)
    by_hi = _dot(jnp.concatenate([c_hi, c_lo], axis=0).astype(BF16), w_hi.astype(BF16))
    o_ref[...] = (by_hi[0:bsz] + by_hi[bsz:] + _dot(c_hi.astype(BF16), w_lo.astype(BF16))
                  + b_ref[...])


def _adaln(c, w, b):
    bsz, d = c.shape
    n = w.shape[1]
    tn = 1024
    est = 2 * (d * tn * 4) + 2 * bsz * d * 4 + 4 * bsz * tn * 4 + 3 * d * tn * 4
    return pl.pallas_call(
        _adaln_kernel,
        grid=(n // tn,),
        in_specs=[pl.BlockSpec((bsz, d), lambda j: (0, 0)),
                  pl.BlockSpec((d, tn), lambda j: (0, j)),
                  pl.BlockSpec((1, tn), lambda j: (0, j))],
        out_specs=pl.BlockSpec((bsz, tn), lambda j: (0, j)),
        out_shape=jax.ShapeDtypeStruct((bsz, n), F32),
        compiler_params=pltpu.CompilerParams(
            dimension_semantics=("arbitrary",), vmem_limit_bytes=_vmem_limit(est)),
        name="adaln_mod",
    )(c, w, b.reshape(1, n))


def _in_proj_kernel(x_ref, mod_ref, g_ref, w_ref, wkv_ref, *refs, slices, n_cast):
    cast_src, (o_ref, okv_ref) = refs[:n_cast], refs[n_cast:n_cast + 2]
    cast_dst, (h_even_ref, h_odd_ref) = refs[n_cast + 2:2 * n_cast + 2], refs[2 * n_cast + 2:]
    i, j = pl.program_id(0), pl.program_id(1)

    def modulated_norm(rows):
        y = _rms(x_ref[rows, :]) * g_ref[...]
        return (y * (1.0 + mod_ref[1:2, :]) + mod_ref[0:1, :]).astype(BF16)

    @pl.when((i == 0) & (j == 0))
    def _():
        h_even_ref[...] = modulated_norm(slice(None))

    def step(h_ref, h_next_ref):
        @pl.when(j == 0)
        def _():
            okv_ref[...] = _dot(h_ref[...], wkv_ref[...]).astype(BF16)

        rows_per = x_ref.shape[0] // slices
        first = pl.multiple_of(jnp.clip(j - 1, 0, slices - 1) * rows_per, rows_per)
        rows = pl.ds(first, rows_per)
        h_next_ref[rows, :] = modulated_norm(rows)
        for src_ref, dst_ref in zip(cast_src, cast_dst):
            dst_ref[...] = src_ref[...].astype(BF16)
        o_ref[...] = _dot(h_ref[...], w_ref[...]).astype(BF16)

    pl.when(i % 2 == 0)(lambda: step(h_even_ref, h_odd_ref))
    pl.when(i % 2 == 1)(lambda: step(h_odd_ref, h_even_ref))


def _in_proj(x2, mod3, gain, w_in, seq, later_weights):
    n, d = x2.shape
    tm, tn = 1024, 1024
    per_batch = seq // tm
    o_kv = ATT_WIDTH
    o_qh = o_kv + 2 * KV_WIDTH
    o_ga = o_qh + 4 * HG_WIDTH
    cols = w_in.shape[1] - 2 * KV_WIDTH
    n_gate = 2 * D_MODEL // tn
    n_q = ATT_WIDTH // tn

    def src_col(j):
        unit = tn // LANES
        lane_tile = jnp.where(j < n_gate, o_ga // LANES + unit * j,
                              jnp.where(j < n_gate + n_q, unit * (j - n_gate),
                                        o_qh // LANES + unit * (j - n_gate - n_q)))
        return lane_tile * LANES

    last = n // tm - 1
    x_tile = lambda i, j: jnp.minimum(i + jnp.where(j > 0, 1, 0), last)

    n_col = cols // tn
    cast_specs, cast_bytes = [], 0
    for w, (br, bc) in later_weights:
        rb, cb = w.shape[0] // br, w.shape[1] // bc
        assert rb * br == w.shape[0] and cb * bc == w.shape[1] and rb * cb <= (n // tm) * n_col
        def block(i, j, rb=rb, cb=cb):
            k = jnp.minimum(i * n_col + j, rb * cb - 1)
            return k // cb, k % cb
        cast_specs.append(pl.BlockSpec((br, bc), block))
        cast_bytes += 2 * br * bc * 6

    est = (2 * tm * d * 4 + 2 * tm * d * 2 + 2 * d * tn * 2 + 2 * tm * tn * 2
           + 2 * d * KV_WIDTH * 2 * 2 + 2 * tm * 2 * KV_WIDTH * 2 + tm * tn * 4 + tm * d * 4 // 8
           + cast_bytes)
    return pl.pallas_call(
        functools.partial(_in_proj_kernel, slices=8, n_cast=len(later_weights)),
        grid=(n // tm, cols // tn),
        in_specs=[pl.BlockSpec((tm, d), lambda i, j: (x_tile(i, j), 0)),
                  pl.BlockSpec((None, N_MOD, d), lambda i, j: (x_tile(i, j) // per_batch, 0, 0)),
                  pl.BlockSpec((1, d), lambda i, j: (0, 0)),
                  pl.BlockSpec((pl.Element(d), pl.Element(tn)), lambda i, j: (0, src_col(j))),
                  pl.BlockSpec((d, 2 * KV_WIDTH), lambda i, j: (0, o_kv // (2 * KV_WIDTH)))]
                 + cast_specs,
        out_specs=[pl.BlockSpec((tm, tn), lambda i, j: (i, j)),
                   pl.BlockSpec((tm, 2 * KV_WIDTH), lambda i, j: (i, 0))] + cast_specs,
        out_shape=[jax.ShapeDtypeStruct((n, cols), BF16),
                   jax.ShapeDtypeStruct((n, 2 * KV_WIDTH), BF16)]
                  + [jax.ShapeDtypeStruct(w.shape, BF16) for w, _ in later_weights],
        scratch_shapes=[pltpu.VMEM((tm, d), BF16), pltpu.VMEM((tm, d), BF16)],
        compiler_params=pltpu.CompilerParams(
            dimension_semantics=("arbitrary", "arbitrary"), vmem_limit_bytes=_vmem_limit(est)),
        name="in_proj",
    )(x2, mod3, gain, w_in, w_in, *[w for w, _ in later_weights])


def _attn_constants():
    lane = np.arange(LANES)
    dim = lane % HEAD_DIM
    expand = np.zeros((2, 3, ROT_HALF, 3, LANES), np.float32)
    for j in range(ROT_HALF):
        expand[0, :, j, 0, (dim < ROT_DIM) & (dim % ROT_HALF == j)] = 1.0
        expand[1, :, j, 1, (dim < ROT_HALF) & (dim == j)] = -1.0
        expand[1, :, j, 2, (dim >= ROT_HALF) & (dim < ROT_DIM) & (dim - ROT_HALF == j)] = 1.0
    expand = expand.reshape(6 * ROT_HALF, 3 * LANES)
    shift = np.zeros((LANES, 2 * LANES), np.float32)
    for l in range(LANES):
        if l + ROT_HALF < LANES:
            shift[l + ROT_HALF, l] = 1.0
        if l - ROT_HALF >= 0:
            shift[l - ROT_HALF, LANES + l] = 1.0
    return jnp.asarray(expand, BF16), jnp.asarray(shift, BF16)


def _attn_kernel(sink_ref, freq_ref, expand_ref, shift_ref, pos_ref, q_ref, kv_ref, o_ref,
                 qb_ref, kb_ref, vb_ref):
    members = q_ref.shape[0]
    nblk = pl.program_id(1)
    parity = nblk % 2
    blk = WINDOW
    lane = lax.broadcasted_iota(jnp.int32, (blk, LANES), 1)
    lo = lane < HEAD_DIM
    dim = lane % HEAD_DIM
    kj = lax.broadcasted_iota(jnp.int32, (blk, blk), 0)
    qi = lax.broadcasted_iota(jnp.int32, (blk, blk), 1)

    @pl.when(nblk == 0)
    def _():
        kb_ref[...] = jnp.zeros_like(kb_ref)
        vb_ref[...] = jnp.zeros_like(vb_ref)
        eye = jnp.where(kj == qi, 1.0, 0.0).astype(BF16)
        ones_lo = jnp.where(lo, 1.0, 0.0).astype(BF16)
        ones_hi = jnp.where(lo, 0.0, 1.0).astype(BF16)
        for g in range(members):
            for p in range(ATT_HEADS // 2):
                qb_ref[g, p * blk:(p + 1) * blk, LANES:] = eye
            for kvh in range(ATT_KV_HEADS):
                for quarter in range(4):
                    rows = slice(quarter * blk, (quarter + 1) * blk)
                    vb_ref[g, kvh, rows, LANES:] = ones_hi if quarter >= 2 else ones_lo

    def rope(x_bf16, tables):
        t_cos, t_lo, t_hi = tables
        ud = _dot(x_bf16, shift_ref[...])
        return x_bf16.astype(F32) * t_cos + ud[:, 0:LANES] * t_lo + ud[:, LANES:] * t_hi

    k_tabs, q_tabs = [], []
    for g in range(members):
        ang = freq_ref[...] * pos_ref[g]
        parts = _split_bf16(jnp.cos(ang), 3) + _split_bf16(jnp.sin(ang), 3)
        tabs = _dot_tn(jnp.concatenate(parts, axis=0).astype(BF16), expand_ref[...])
        kt = (tabs[:, 0:LANES] + jnp.where(dim < ROT_DIM, 0.0, 1.0),
              tabs[:, LANES:2 * LANES], tabs[:, 2 * LANES:3 * LANES])
        k_tabs.append(kt)
        q_tabs.append(tuple(t * (HEAD_DIM ** -0.5 * LOG2E) for t in kt))

    bias = []
    for half in range(2):
        is_cur = parity == half
        valid = (is_cur & (kj <= qi)) | ((~is_cur) & (nblk > 0) & (kj > qi))
        bias.append(jnp.where(valid, 0.0, MASK_BIAS).astype(BF16))
    for g in range(members):
        k = rope(kv_ref[g, :, 0:KV_WIDTH], k_tabs[g])
        v = kv_ref[g, :, KV_WIDTH:].astype(F32)
        k_sw = pltpu.roll(k, HEAD_DIM, axis=1)
        v_sw = pltpu.roll(v, HEAD_DIM, axis=1)
        for kvh in range(ATT_KV_HEADS):
            for slot in range(2):
                ks = (k, k_sw)[kvh ^ slot]
                vs = (v, v_sw)[kvh ^ slot]
                keep = lo if slot == 0 else ~lo
                rows = pl.ds(pl.multiple_of((2 * slot + parity) * blk, blk), blk)
                kb_ref[g, kvh, rows, 0:LANES] = jnp.where(keep, ks, 0.0).astype(BF16)
                vb_ref[g, kvh, rows, 0:LANES] = jnp.where(keep, vs, 0.0).astype(BF16)
                for half in range(2):
                    rows = slice((2 * slot + half) * blk, (2 * slot + half + 1) * blk)
                    kb_ref[g, kvh, rows, LANES:] = bias[half]

    pairs_per_kv = ATT_HEADS // ATT_KV_HEADS // 2
    cols = [slice(p * LANES, (p + 1) * LANES) for p in range(ATT_HEADS // 2)]
    groups = [(g, kvh) for g in range(members) for kvh in range(ATT_KV_HEADS)]
    pairs_of = lambda kvh: range(kvh * pairs_per_kv, (kvh + 1) * pairs_per_kv)
    for g in range(members):
        stacked = jnp.concatenate([q_ref[g, :, cs] for cs in cols], axis=0)
        tables = tuple(jnp.concatenate([t] * len(cols), axis=0) for t in q_tabs[g])
        qb_ref[g, :, 0:LANES] = rope(stacked, tables).astype(BF16)
    rows_of = lambda kvh: slice(kvh * pairs_per_kv * blk, (kvh + 1) * pairs_per_kv * blk)
    s = [_dot_nt(qb_ref[g, rows_of(kvh), :], kb_ref[g, kvh]) for g, kvh in groups]
    probs, sink = [], []
    for w, (g, kvh) in enumerate(groups):
        pp, ss = [], []
        for i, p in enumerate(pairs_of(kvh)):
            pe, se = [], []
            for e in range(2):
                logits = s[w][i * blk:(i + 1) * blk, e * 2 * blk:(e + 1) * 2 * blk]
                sink2 = sink_ref[2 * p + e] * LOG2E
                m = jnp.maximum(jnp.max(logits, axis=-1, keepdims=True), sink2)
                pe.append(jnp.exp2(logits - m).astype(BF16))
                se.append(jnp.exp2(sink2 - m))
            pp.append(jnp.concatenate(pe, axis=1))
            ss.append(jnp.where(lo, se[0], se[1]))
        probs.append(jnp.concatenate(pp, axis=0))
        sink.append(ss)
    o = [_dot(probs[w], vb_ref[g, kvh]) for w, (g, kvh) in enumerate(groups)]
    for w, (g, kvh) in enumerate(groups):
        for i, p in enumerate(pairs_of(kvh)):
            op = o[w][i * blk:(i + 1) * blk]
            o_ref[g, :, cols[p]] = (op[:, 0:LANES] / (op[:, LANES:] + sink[w][i])).astype(BF16)


def _attention(sinks, freq8, pos4, proj3, kv3, q_col_block):
    bsz, seq = proj3.shape[0], proj3.shape[1]
    blk = WINDOW
    members = 4
    expand, shift = _attn_constants()
    full = lambda a: pl.BlockSpec(a.shape, lambda b, i: (0,) * a.ndim)
    est = (2 * members * (blk * ATT_WIDTH * 2 * 2 + blk * 2 * KV_WIDTH * 2)
           + 2 * members * ATT_KV_HEADS * 4 * blk * 2 * LANES * 2 + members * ATT_WIDTH * blk * 2
           + members * 64 * blk * 4 * blk * 4)
    return pl.pallas_call(
        _attn_kernel,
        grid=(bsz // members, seq // blk),
        in_specs=[pl.BlockSpec(memory_space=pltpu.SMEM),
                  full(freq8), full(expand), full(shift),
                  pl.BlockSpec((members, None, 1, blk), lambda b, i: (b, i, 0, 0)),
                  pl.BlockSpec((members, blk, ATT_WIDTH), lambda b, i: (b, i, q_col_block)),
                  pl.BlockSpec((members, blk, 2 * KV_WIDTH), lambda b, i: (b, i, 0))],
        out_specs=pl.BlockSpec((members, blk, ATT_WIDTH), lambda b, i: (b, i, 0)),
        out_shape=jax.ShapeDtypeStruct((bsz, seq, ATT_WIDTH), BF16),
        scratch_shapes=[pltpu.VMEM((members, ATT_HEADS // 2 * blk, 2 * LANES), BF16),
                        pltpu.VMEM((members, ATT_KV_HEADS, 4 * blk, 2 * LANES), BF16),
                        pltpu.VMEM((members, ATT_KV_HEADS, 4 * blk, 2 * LANES), BF16)],
        compiler_params=pltpu.CompilerParams(
            dimension_semantics=("parallel", "arbitrary"), vmem_limit_bytes=_vmem_limit(est)),
        name="swa_attn",
    )(sinks, freq8, expand, shift, pos4, proj3, kv3)


def _hgrn_constants():
    t = np.arange(CHUNK)[:, None]
    j = np.arange(CHUNK)[None, :]
    tri = (j <= t).astype(np.float32)
    cumsum = np.concatenate([tri, tri], axis=1)
    reduce = np.zeros((DIAG, HG_DIM, LANES), np.float32)
    for d in range(DIAG):
        reduce[d, :, DIAG - 1 - d] = 1.0
    return jnp.asarray(cumsum, BF16), jnp.asarray(reduce.reshape(DIAG * HG_DIM, LANES), BF16)


def _hgrn_kernel(cm_ref, red_ref, lbp_ref, gn_ref, q_ref, f_ref, i_ref, g_ref, o_ref, st_ref, *,
                 tokens):
    @pl.when(pl.program_id(1) == 0)
    def _():
        st_ref[...] = jnp.zeros_like(st_ref)

    lbp = lbp_ref[...]
    e = jnp.exp(lbp - jnp.max(lbp, axis=0, keepdims=True))
    lb = e[0:1, :] / jnp.sum(e, axis=0, keepdims=True)
    half_key_scale = 0.5 * (1.0 - lb)

    row = lax.broadcasted_iota(jnp.int32, (CHUNK, CHUNK), 0)
    col = lax.broadcasted_iota(jnp.int32, (CHUNK, CHUNK), 1)
    same = {w: row // w == col // w for w in (DIAG, 2 * DIAG, 4 * DIAG)}
    groups = CHUNK // DIAG

    heads = [slice(h * HG_DIM, (h + 1) * HG_DIM) for h in range(HG_HEADS)]
    chunks = [slice(c * CHUNK, (c + 1) * CHUNK) for c in range(tokens // CHUNK)]
    per_chunk = []

    for rs in chunks:
        half_q = 0.5 * q_ref[rs, :].astype(F32)
        qs = half_q + half_q * jnp.tanh(half_q)
        key = half_key_scale - half_key_scale * jnp.tanh(0.5 * f_ref[rs, :].astype(F32))
        f = 1.0 - key
        val = i_ref[rs, :]

        b = _dot(cm_ref[...], jnp.concatenate(_split_bf16(jnp.log2(f), 2), axis=0).astype(BF16))
        eb = jnp.exp2(b)
        q_state = (qs * eb).astype(BF16)
        k_state = (key * jnp.exp2(b[CHUNK - 1:CHUNK, :] - b)).astype(BF16)
        decay = eb[CHUNK - 1:CHUNK, :]

        q_lv, k_lv = [], []
        for lv in LEVELS:
            qp, kp = [], []
            for piece in range(CHUNK // lv):
                ps = slice(piece * lv, (piece + 1) * lv)
                zero = jnp.zeros((lv, HG_WIDTH), F32)
                if piece % 2:
                    ref = b[piece * lv - 1:piece * lv, :]
                    qp.append(qs[ps] * jnp.exp2(b[ps] - ref)); kp.append(zero)
                else:
                    ref = b[(piece + 1) * lv - 1:(piece + 1) * lv, :]
                    qp.append(zero); kp.append(key[ps] * jnp.exp2(ref - b[ps]))
            q_lv.append(jnp.concatenate(qp, axis=0).astype(BF16))
            k_lv.append(jnp.concatenate(kp, axis=0).astype(BF16))

        carry = key.reshape(groups, DIAG, HG_WIDTH)
        f3 = f.reshape(groups, DIAG, HG_WIDTH)
        qs_bf16 = qs.astype(BF16)
        diag = [qs_bf16 * key.astype(BF16)]
        for d in range(1, DIAG):
            carry = f3 * pltpu.roll(carry, 1, axis=1)
            diag.append(qs_bf16 * carry.reshape(CHUNK, HG_WIDTH).astype(BF16))

        parts = [[_dot_nt(q_lv[n][:, hs], k_lv[n][:, hs]) for n in range(len(LEVELS))]
                 for hs in heads]
        r = _dot(jnp.concatenate([jnp.concatenate([dg[:, hs] for dg in diag], axis=1)
                                  for hs in heads], axis=0), red_ref[...])
        update = [_dot_tn(val[:, hs], k_state[:, hs]) for hs in heads]
        per_chunk.append((parts, r, update, q_state, val, decay))

    g_all = []
    for parts, r, _, _, _, _ in per_chunk:
        g_chunk = []
        for h in range(HG_HEADS):
            g_diag = pltpu.roll(r[h * CHUNK:(h + 1) * CHUNK], LANES - (DIAG - 1), axis=1,
                                stride=1, stride_axis=0)[:, 0:CHUNK]
            g_mat = jnp.where(same[DIAG], g_diag,
                              jnp.where(same[2 * DIAG], parts[h][2],
                                        jnp.where(same[4 * DIAG], parts[h][1], parts[h][0])))
            g_chunk.append(g_mat.astype(BF16))
        g_all.append(g_chunk)

    state = [st_ref[h] for h in range(HG_HEADS)]
    for rs, g_chunk, (_, _, update, q_state, val, decay) in zip(chunks, g_all, per_chunk):
        for h, hs in enumerate(heads):
            o = _dot_nt(q_state[:, hs], state[h].astype(BF16)) + _dot(g_chunk[h], val[:, hs])
            state[h] = state[h] * decay[:, hs] + update[h]
            y = _rms(o) * gn_ref[...]
            o_ref[rs, hs] = (y * _sigmoid(g_ref[rs, hs].astype(F32))).astype(BF16)
    for h in range(HG_HEADS):
        st_ref[h] = state[h]


def _hgrn(lb_params, gnorm, proj, bsz, seq, col_blocks):
    n = proj.shape[0]
    tokens = 8 * CHUNK
    nt = seq // tokens
    cmat, rmat = _hgrn_constants()
    row = lambda b, i: b * nt + i
    spec = lambda cb: pl.BlockSpec((tokens, HG_WIDTH), lambda b, i: (row(b, i), cb))
    full = lambda a: pl.BlockSpec(a.shape, lambda b, i: (0,) * a.ndim)
    est = 2 * 5 * tokens * HG_WIDTH * 2 + HG_HEADS * HG_DIM * HG_DIM * 4 + 40 * CHUNK * HG_WIDTH * 4
    return pl.pallas_call(
        functools.partial(_hgrn_kernel, tokens=tokens),
        grid=(bsz, nt),
        in_specs=[full(cmat), full(rmat), full(lb_params), full(gnorm),
                  spec(col_blocks[0]), spec(col_blocks[1]), spec(col_blocks[2]), spec(col_blocks[3])],
        out_specs=pl.BlockSpec((tokens, HG_WIDTH), lambda b, i: (row(b, i), 0)),
        out_shape=jax.ShapeDtypeStruct((n, HG_WIDTH), BF16),
        scratch_shapes=[pltpu.VMEM((HG_HEADS, HG_DIM, HG_DIM), F32)],
        compiler_params=pltpu.CompilerParams(
            dimension_semantics=("parallel", "arbitrary"), vmem_limit_bytes=_vmem_limit(est)),
        name="hgrn2",
    )(cmat, rmat, lb_params, gnorm, proj, proj, proj, proj)


def _mix_kernel(x_ref, mod_ref, g_ref, a_ref, hg_ref, ga_ref, gh_ref, wa_ref, wh_ref, wo_ref,
                o_ref, m_ref, *, tc):
    d = x_ref.shape[1]
    for c in range(d // tc):
        cs = slice(c * tc, (c + 1) * tc)
        ya = _dot(a_ref[...], wa_ref[:, cs])
        yh = _dot(hg_ref[...], wh_ref[:, cs])
        merged = (_sigmoid(ga_ref[:, cs].astype(F32)) * ya
                  + _sigmoid(gh_ref[:, cs].astype(F32)) * yh)
        m_ref[:, cs] = merged.astype(BF16)
    y = _dot(m_ref[...], wo_ref[...])
    o_ref[...] = _gated_norm_residual(x_ref[...], y, mod_ref[2:3, :], g_ref[...])


def _mix_out(x2, mod3, gain, attn, hg, proj, wa, wh, wo, seq, ga_block, gh_block):
    n, d = x2.shape
    tm, tc = 512, 512
    per_batch = seq // tm
    const = lambda shape: pl.BlockSpec(shape, lambda i: (0, 0), pipeline_mode=pl.Buffered(1))
    est = ((wa.size + wh.size + wo.size) * 2 + 2 * 2 * tm * d * 4 + 2 * 2 * tm * d * 2
           + 2 * 2 * tm * ATT_WIDTH * 2 + tm * d * 2 + 3 * tm * d * 4)
    return pl.pallas_call(
        functools.partial(_mix_kernel, tc=tc),
        grid=(n // tm,),
        in_specs=[pl.BlockSpec((tm, d), lambda i: (i, 0)),
                  pl.BlockSpec((None, N_MOD, d), lambda i: (i // per_batch, 0, 0)),
                  pl.BlockSpec((1, d), lambda i: (0, 0)),
                  pl.BlockSpec((tm, ATT_WIDTH), lambda i: (i, 0)),
                  pl.BlockSpec((tm, HG_WIDTH), lambda i: (i, 0)),
                  pl.BlockSpec((tm, d), lambda i: (i, ga_block)),
                  pl.BlockSpec((tm, d), lambda i: (i, gh_block)),
                  const(wa.shape), const(wh.shape), const(wo.shape)],
        out_specs=pl.BlockSpec((tm, d), lambda i: (i, 0)),
        out_shape=jax.ShapeDtypeStruct((n, d), F32),
        scratch_shapes=[pltpu.VMEM((tm, d), BF16)],
        compiler_params=pltpu.CompilerParams(
            dimension_semantics=("parallel",), vmem_limit_bytes=_vmem_limit(est)),
        name="mix_out",
    )(x2, mod3, gain, attn, hg, proj, proj, wa, wh, wo)


def _ffn_up_kernel(x_ref, mod_ref, gpre_ref, wg_ref, wu_ref, wd_ref, o_ref, wd_bf16_ref,
                   h_even_ref, h_odd_ref, *, slices):
    i, j = pl.program_id(0), pl.program_id(1)

    def modulated_norm(rows):
        y = _rms(x_ref[rows, :]) * gpre_ref[...]
        return (y * (1.0 + mod_ref[4:5, :]) + mod_ref[3:4, :]).astype(BF16)

    @pl.when((i == 0) & (j == 0))
    def _():
        h_even_ref[...] = modulated_norm(slice(None))

    def step(h_ref, h_next_ref):
        rows_per = x_ref.shape[0] // slices
        first = pl.multiple_of(jnp.clip(j - 1, 0, slices - 1) * rows_per, rows_per)
        rows = pl.ds(first, rows_per)
        h_next_ref[rows, :] = modulated_norm(rows)
        wd_bf16_ref[...] = wd_ref[...].astype(BF16)

        h = h_ref[...]
        g = _dot(h, wg_ref[...])
        u = _dot(h, wu_ref[...])
        o_ref[...] = (g * _sigmoid(g) * u).astype(BF16)

    pl.when(i % 2 == 0)(lambda: step(h_even_ref, h_odd_ref))
    pl.when(i % 2 == 1)(lambda: step(h_odd_ref, h_even_ref))


def _ffn_down_kernel(a_ref, x_ref, mod_ref, gpost_ref, wd_ref, o_ref):
    y = _dot(a_ref[...], wd_ref[...])
    o_ref[...] = _gated_norm_residual(x_ref[...], y, mod_ref[5:6, :], gpost_ref[...])


def _ffn(x1, mod3, gpre, gpost, w_in, w_down, seq):
    n, d = x1.shape
    hidden = w_down.shape[0]

    tm, th = 1024, 512
    nh = hidden // th
    per_batch = seq // tm
    last = n // tm - 1
    x_tile = lambda i, j: jnp.minimum(i + jnp.where(j > 0, 1, 0), last)
    steps = (n // tm) * nh
    wd_rows = hidden // steps
    assert wd_rows * steps == hidden and wd_rows % 16 == 0
    est = (2 * tm * d * 4 + 2 * tm * d * 2 + 2 * 2 * d * th * 2 + 2 * tm * th * 2
           + 4 * tm * th * 4 + 2 * wd_rows * d * 6)
    act, w_down_bf16 = pl.pallas_call(
        functools.partial(_ffn_up_kernel, slices=8),
        grid=(n // tm, nh),
        in_specs=[pl.BlockSpec((tm, d), lambda i, j: (x_tile(i, j), 0)),
                  pl.BlockSpec((None, N_MOD, d), lambda i, j: (x_tile(i, j) // per_batch, 0, 0)),
                  pl.BlockSpec((1, d), lambda i, j: (0, 0)),
                  pl.BlockSpec((d, th), lambda i, j: (0, j)),
                  pl.BlockSpec((d, th), lambda i, j: (0, nh + j)),
                  pl.BlockSpec((wd_rows, d), lambda i, j: (i * nh + j, 0))],
        out_specs=[pl.BlockSpec((tm, th), lambda i, j: (i, j)),
                   pl.BlockSpec((wd_rows, d), lambda i, j: (i * nh + j, 0))],
        out_shape=[jax.ShapeDtypeStruct((n, hidden), BF16),
                   jax.ShapeDtypeStruct((hidden, d), BF16)],
        scratch_shapes=[pltpu.VMEM((tm, d), BF16), pltpu.VMEM((tm, d), BF16)],
        compiler_params=pltpu.CompilerParams(
            dimension_semantics=("arbitrary", "arbitrary"), vmem_limit_bytes=_vmem_limit(est)),
        name="ffn_up",
    )(x1, mod3, gpre, w_in, w_in, w_down)
    w_down = w_down_bf16

    tm = 512
    per_batch = seq // tm
    est = hidden * d * 2 + 2 * tm * hidden * 2 + 2 * 2 * tm * d * 4 + 2 * tm * d * 4
    return pl.pallas_call(
        _ffn_down_kernel,
        grid=(n // tm,),
        in_specs=[pl.BlockSpec((tm, hidden), lambda i: (i, 0)),
                  pl.BlockSpec((tm, d), lambda i: (i, 0)),
                  pl.BlockSpec((None, N_MOD, d), lambda i: (i // per_batch, 0, 0)),
                  pl.BlockSpec((1, d), lambda i: (0, 0)),
                  pl.BlockSpec((hidden, d), lambda i: (0, 0), pipeline_mode=pl.Buffered(1))],
        out_specs=pl.BlockSpec((tm, d), lambda i: (i, 0)),
        out_shape=jax.ShapeDtypeStruct((n, d), F32),
        compiler_params=pltpu.CompilerParams(
            dimension_semantics=("parallel",), vmem_limit_bytes=_vmem_limit(est)),
        name="ffn_down",
    )(act, x1, mod3, gpost, w_down)


def kernel(x, c, positions, w_ada, b_ada, g_pre_mix, g_post_mix, g_pre_ffn, g_post_ffn, w_in,
           attn_sinks, w_attn_proj, hg_lower_bounds, hg_norm, w_hgrn_proj, w_out, w_ffn_in,
           w_ffn_out):
    bsz, seq, d = x.shape
    n = bsz * seq
    assert d == D_MODEL and w_ada.shape[0] == 1 and seq % 1024 == 0 and bsz % 4 == 0

    ga_block, gh_block = 0, 1
    q_block = 2 * D_MODEL // ATT_WIDTH
    hg_blocks = tuple((2 * D_MODEL + ATT_WIDTH) // HG_WIDTH + k for k in range(4))

    x2 = x.reshape(n, d)
    mod3 = _adaln(c, w_ada[0], b_ada[0]).reshape(bsz, N_MOD, d)

    proj, kv, wa, wh, wo, w_up = _in_proj(
        x2, mod3, g_pre_mix, w_in[0].astype(BF16), seq,
        ((w_attn_proj[0], (16, D_MODEL)), (w_hgrn_proj[0], (16, D_MODEL)), (w_out[0], (16, D_MODEL)),
         (w_ffn_in[0], (D_MODEL, LANES))))

    inv_freq = ROPE_THETA ** (-jnp.arange(0, ROT_DIM, 2, dtype=F32) / ROT_DIM)
    freq8 = jnp.broadcast_to(inv_freq[:, None], (ROT_HALF, WINDOW))
    pos4 = positions.astype(F32).reshape(bsz, seq // WINDOW, 1, WINDOW)
    attn = _attention(attn_sinks[0], freq8, pos4, proj.reshape(bsz, seq, -1),
                      kv.reshape(bsz, seq, -1), q_block).reshape(n, ATT_WIDTH)

    hg = _hgrn(hg_lower_bounds, hg_norm, proj, bsz, seq, hg_blocks)

    x1 = _mix_out(x2, mod3, g_post_mix, attn, hg, proj, wa, wh, wo, seq, ga_block, gh_block)

    out = _ffn(x1, mod3, g_pre_ffn, g_post_ffn, w_up, w_ffn_out[0], seq)
    return out.reshape(bsz, seq, d)
```

```python
import functools
import math

import numpy as np
import jax
import jax.numpy as jnp
from jax import lax
from jax.experimental import pallas as pl
from jax.experimental.pallas import tpu as pltpu

F32 = jnp.float32
BF16 = jnp.bfloat16

D_MODEL = 2048
ATT_HEADS = 16
ATT_KV_HEADS = 2
HEAD_DIM = 64
ATT_WIDTH = ATT_HEADS * HEAD_DIM
KV_WIDTH = ATT_KV_HEADS * HEAD_DIM
WINDOW = 128
ROT_DIM = HEAD_DIM // 4
ROT_HALF = ROT_DIM // 2
ROPE_THETA = 500000.0
HG_HEADS = 8
HG_DIM = 128
HG_WIDTH = HG_HEADS * HG_DIM
N_MOD = 6
EPS = 1e-6
LOG2E = math.log2(math.e)
MASK_BIAS = -1e30

LANES = 128
SUBLANES = 8
V7X_VMEM_BYTES = 64 * 1024 * 1024

CHUNK = 64
DIAG = SUBLANES
LEVELS = (32, 16, 8)


def _vmem_limit(estimate_bytes):
    return int(min(estimate_bytes * 3 // 2, V7X_VMEM_BYTES - 8 * 1024 * 1024))


def _dot(a, b):
    return jnp.dot(a, b, preferred_element_type=F32)


def _dot_nt(a, b):
    return lax.dot_general(a, b, (((1,), (1,)), ((), ())), preferred_element_type=F32)


def _dot_tn(a, b):
    return lax.dot_general(a, b, (((0,), (0,)), ((), ())), preferred_element_type=F32)


def _rms(t):
    return t * lax.rsqrt(jnp.mean(t * t, axis=-1, keepdims=True) + EPS)


def _gated_norm_residual(x, y, gate, gain):
    return x + _rms(y) * (gate * gain)


def _sigmoid(t):
    return 0.5 + 0.5 * jnp.tanh(0.5 * t)


def _split_bf16(x, parts):
    out = []
    for _ in range(parts - 1):
        p = x.astype(BF16).astype(F32)
        out.append(p)
        x = x - p
    out.append(x.astype(BF16).astype(F32))
    return out


def _adaln_kernel(c_ref, w_ref, b_ref, win_ref, o_ref, win_bf16_ref):
    bsz = c_ref.shape[0]
    c_hi, c_lo = _split_bf16(c_ref[...], 2)
    w_hi, w_lo = _split_bf16(w_ref[...], 2)
    by_hi = _dot(jnp.concatenate([c_hi, c_lo], axis=0).astype(BF16), w_hi.astype(BF16))
    o_ref[...] = (by_hi[0:bsz] + by_hi[bsz:] + _dot(c_hi.astype(BF16), w_lo.astype(BF16))
                  + b_ref[...])
    win_bf16_ref[...] = win_ref[...].astype(BF16)


def _adaln(c, w, b, w_in):
    bsz, d = c.shape
    n = w.shape[1]
    steps = 16
    tn = n // steps
    rows = w_in.shape[0] // steps
    assert tn * steps == n and tn % LANES == 0 and rows * steps == w_in.shape[0] and rows % 16 == 0
    est = (2 * (d * tn * 4) + 2 * bsz * d * 4 + 4 * bsz * tn * 4 + 3 * d * tn * 4
           + 2 * rows * w_in.shape[1] * 6)
    return pl.pallas_call(
        _adaln_kernel,
        grid=(steps,),
        in_specs=[pl.BlockSpec((bsz, d), lambda j: (0, 0)),
                  pl.BlockSpec((d, tn), lambda j: (0, j)),
                  pl.BlockSpec((1, tn), lambda j: (0, j)),
                  pl.BlockSpec((rows, w_in.shape[1]), lambda j: (j, 0))],
        out_specs=[pl.BlockSpec((bsz, tn), lambda j: (0, j)),
                   pl.BlockSpec((rows, w_in.shape[1]), lambda j: (j, 0))],
        out_shape=[jax.ShapeDtypeStruct((bsz, n), F32),
                   jax.ShapeDtypeStruct(w_in.shape, BF16)],
        compiler_params=pltpu.CompilerParams(
            dimension_semantics=("arbitrary",), vmem_limit_bytes=_vmem_limit(est)),
        name="adaln_mod",
    )(c, w, b.reshape(1, n), w_in)


def _in_proj_kernel(x_ref, mod_ref, g_ref, w_ref, wkv_ref, *refs, slices, n_cast):
    cast_src, (o_ref, okv_ref) = refs[:n_cast], refs[n_cast:n_cast + 2]
    cast_dst, (h_even_ref, h_odd_ref) = refs[n_cast + 2:2 * n_cast + 2], refs[2 * n_cast + 2:]
    i, j = pl.program_id(0), pl.program_id(1)

    def modulated_norm(rows):
        y = _rms(x_ref[rows, :]) * g_ref[...]
        return (y * (1.0 + mod_ref[1:2, :]) + mod_ref[0:1, :]).astype(BF16)

    @pl.when((i == 0) & (j == 0))
    def _():
        h_even_ref[...] = modulated_norm(slice(None))

    def step(h_ref, h_next_ref):
        @pl.when(j == 0)
        def _():
            okv_ref[...] = _dot(h_ref[...], wkv_ref[...]).astype(BF16)

        rows_per = x_ref.shape[0] // slices
        first = pl.multiple_of(jnp.clip(j - 1, 0, slices - 1) * rows_per, rows_per)
        rows = pl.ds(first, rows_per)
        h_next_ref[rows, :] = modulated_norm(rows)
        for src_ref, dst_ref in zip(cast_src, cast_dst):
            dst_ref[...] = src_ref[...].astype(BF16)
        o_ref[...] = _dot(h_ref[...], w_ref[...]).astype(BF16)

    pl.when(i % 2 == 0)(lambda: step(h_even_ref, h_odd_ref))
    pl.when(i % 2 == 1)(lambda: step(h_odd_ref, h_even_ref))


def _in_proj(x2, mod3, gain, w_in, seq, later_weights):
    n, d = x2.shape
    tm, tn = 1024, 1024
    per_batch = seq // tm
    o_kv = ATT_WIDTH
    o_qh = o_kv + 2 * KV_WIDTH
    o_ga = o_qh + 4 * HG_WIDTH
    cols = w_in.shape[1] - 2 * KV_WIDTH
    n_gate = 2 * D_MODEL // tn
    n_q = ATT_WIDTH // tn

    def src_col(j):
        unit = tn // LANES
        lane_tile = jnp.where(j < n_gate, o_ga // LANES + unit * j,
                              jnp.where(j < n_gate + n_q, unit * (j - n_gate),
                                        o_qh // LANES + unit * (j - n_gate - n_q)))
        return lane_tile * LANES

    last = n // tm - 1
    x_tile = lambda i, j: jnp.minimum(i + jnp.where(j > 0, 1, 0), last)

    n_col = cols // tn
    cast_specs, cast_bytes = [], 0
    for w, (br, bc) in later_weights:
        rb, cb = w.shape[0] // br, w.shape[1] // bc
        assert rb * br == w.shape[0] and cb * bc == w.shape[1] and rb * cb <= (n // tm) * n_col
        def block(i, j, rb=rb, cb=cb):
            k = jnp.minimum(i * n_col + j, rb * cb - 1)
            return k // cb, k % cb
        cast_specs.append(pl.BlockSpec((br, bc), block))
        cast_bytes += 2 * br * bc * 6

    est = (2 * tm * d * 4 + 2 * tm * d * 2 + 2 * d * tn * 2 + 2 * tm * tn * 2
           + 2 * d * KV_WIDTH * 2 * 2 + 2 * tm * 2 * KV_WIDTH * 2 + tm * tn * 4 + tm * d * 4 // 8
           + cast_bytes)
    return pl.pallas_call(
        functools.partial(_in_proj_kernel, slices=8, n_cast=len(later_weights)),
        grid=(n // tm, cols // tn),
        in_specs=[pl.BlockSpec((tm, d), lambda i, j: (x_tile(i, j), 0)),
                  pl.BlockSpec((None, N_MOD, d), lambda i, j: (x_tile(i, j) // per_batch, 0, 0)),
                  pl.BlockSpec((1, d), lambda i, j: (0, 0)),
                  pl.BlockSpec((pl.Element(d), pl.Element(tn)), lambda i, j: (0, src_col(j))),
                  pl.BlockSpec((d, 2 * KV_WIDTH), lambda i, j: (0, o_kv // (2 * KV_WIDTH)))]
                 + cast_specs,
        out_specs=[pl.BlockSpec((tm, tn), lambda i, j: (i, j)),
                   pl.BlockSpec((tm, 2 * KV_WIDTH), lambda i, j: (i, 0))] + cast_specs,
        out_shape=[jax.ShapeDtypeStruct((n, cols), BF16),
                   jax.ShapeDtypeStruct((n, 2 * KV_WIDTH), BF16)]
                  + [jax.ShapeDtypeStruct(w.shape, BF16) for w, _ in later_weights],
        scratch_shapes=[pltpu.VMEM((tm, d), BF16), pltpu.VMEM((tm, d), BF16)],
        compiler_params=pltpu.CompilerParams(
            dimension_semantics=("arbitrary", "arbitrary"), vmem_limit_bytes=_vmem_limit(est)),
        name="in_proj",
    )(x2, mod3, gain, w_in, w_in, *[w for w, _ in later_weights])


def _attn_constants():
    lane = np.arange(LANES)
    dim = lane % HEAD_DIM
    expand = np.zeros((2, 3, ROT_HALF, 3, LANES), np.float32)
    for j in range(ROT_HALF):
        expand[0, :, j, 0, (dim < ROT_DIM) & (dim % ROT_HALF == j)] = 1.0
        expand[1, :, j, 1, (dim < ROT_HALF) & (dim == j)] = -1.0
        expand[1, :, j, 2, (dim >= ROT_HALF) & (dim < ROT_DIM) & (dim - ROT_HALF == j)] = 1.0
    expand = expand.reshape(6 * ROT_HALF, 3 * LANES)
    shift = np.zeros((LANES, 2 * LANES), np.float32)
    for l in range(LANES):
        if l + ROT_HALF < LANES:
            shift[l + ROT_HALF, l] = 1.0
        if l - ROT_HALF >= 0:
            shift[l - ROT_HALF, LANES + l] = 1.0
    return jnp.asarray(expand, BF16), jnp.asarray(shift, BF16)


def _attn_kernel(sink_ref, freq_ref, expand_ref, shift_ref, pos_ref, q_ref, kv_ref, *refs):
    n_cast = (len(refs) - 4) // 2
    o_ref, (qb_ref, kb_ref, vb_ref) = refs[n_cast], refs[2 * n_cast + 1:]
    members = q_ref.shape[0]
    nblk = pl.program_id(1)
    parity = nblk % 2
    blk = WINDOW
    lane = lax.broadcasted_iota(jnp.int32, (blk, LANES), 1)
    lo = lane < HEAD_DIM
    dim = lane % HEAD_DIM
    kj = lax.broadcasted_iota(jnp.int32, (blk, blk), 0)
    qi = lax.broadcasted_iota(jnp.int32, (blk, blk), 1)

    @pl.when(nblk == 0)
    def _():
        kb_ref[...] = jnp.zeros_like(kb_ref)
        vb_ref[...] = jnp.zeros_like(vb_ref)
        eye = jnp.where(kj == qi, 1.0, 0.0).astype(BF16)
        ones_lo = jnp.where(lo, 1.0, 0.0).astype(BF16)
        ones_hi = jnp.where(lo, 0.0, 1.0).astype(BF16)
        for g in range(members):
            for p in range(ATT_HEADS // 2):
                qb_ref[g, p * blk:(p + 1) * blk, LANES:] = eye
            for kvh in range(ATT_KV_HEADS):
                for quarter in range(4):
                    rows = slice(quarter * blk, (quarter + 1) * blk)
                    vb_ref[g, kvh, rows, LANES:] = ones_hi if quarter >= 2 else ones_lo

    def rope(x_bf16, tables):
        t_cos, t_lo, t_hi = tables
        ud = _dot(x_bf16, shift_ref[...])
        return x_bf16.astype(F32) * t_cos + ud[:, 0:LANES] * t_lo + ud[:, LANES:] * t_hi

    k_tabs, q_tabs = [], []
    for g in range(members):
        ang = freq_ref[...] * pos_ref[g]
        parts = _split_bf16(jnp.cos(ang), 3) + _split_bf16(jnp.sin(ang), 3)
        tabs = _dot_tn(jnp.concatenate(parts, axis=0).astype(BF16), expand_ref[...])
        kt = (tabs[:, 0:LANES] + jnp.where(dim < ROT_DIM, 0.0, 1.0),
              tabs[:, LANES:2 * LANES], tabs[:, 2 * LANES:3 * LANES])
        k_tabs.append(kt)
        q_tabs.append(tuple(t * (HEAD_DIM ** -0.5 * LOG2E) for t in kt))

    bias = []
    for half in range(2):
        is_cur = parity == half
        valid = (is_cur & (kj <= qi)) | ((~is_cur) & (nblk > 0) & (kj > qi))
        bias.append(jnp.where(valid, 0.0, MASK_BIAS).astype(BF16))
    for g in range(members):
        k = rope(kv_ref[g, :, 0:KV_WIDTH], k_tabs[g])
        v = kv_ref[g, :, KV_WIDTH:].astype(F32)
        k_sw = pltpu.roll(k, HEAD_DIM, axis=1)
        v_sw = pltpu.roll(v, HEAD_DIM, axis=1)
        for kvh in range(ATT_KV_HEADS):
            for slot in range(2):
                ks = (k, k_sw)[kvh ^ slot]
                vs = (v, v_sw)[kvh ^ slot]
                keep = lo if slot == 0 else ~lo
                rows = pl.ds(pl.multiple_of((2 * slot + parity) * blk, blk), blk)
                kb_ref[g, kvh, rows, 0:LANES] = jnp.where(keep, ks, 0.0).astype(BF16)
                vb_ref[g, kvh, rows, 0:LANES] = jnp.where(keep, vs, 0.0).astype(BF16)
                for half in range(2):
                    rows = slice((2 * slot + half) * blk, (2 * slot + half + 1) * blk)
                    kb_ref[g, kvh, rows, LANES:] = bias[half]

    pairs_per_kv = ATT_HEADS // ATT_KV_HEADS // 2
    cols = [slice(p * LANES, (p + 1) * LANES) for p in range(ATT_HEADS // 2)]
    groups = [(g, kvh) for g in range(members) for kvh in range(ATT_KV_HEADS)]
    pairs_of = lambda kvh: range(kvh * pairs_per_kv, (kvh + 1) * pairs_per_kv)
    for g in range(members):
        stacked = jnp.concatenate([q_ref[g, :, cs] for cs in cols], axis=0)
        tables = tuple(jnp.concatenate([t] * len(cols), axis=0) for t in q_tabs[g])
        qb_ref[g, :, 0:LANES] = rope(stacked, tables).astype(BF16)
    rows_of = lambda kvh: slice(kvh * pairs_per_kv * blk, (kvh + 1) * pairs_per_kv * blk)
    s = [_dot_nt(qb_ref[g, rows_of(kvh), :], kb_ref[g, kvh]) for g, kvh in groups]
    probs, sink = [], []
    for w, (g, kvh) in enumerate(groups):
        pp, ss = [], []
        for i, p in enumerate(pairs_of(kvh)):
            pe, se = [], []
            for e in range(2):
                logits = s[w][i * blk:(i + 1) * blk, e * 2 * blk:(e + 1) * 2 * blk]
                sink2 = sink_ref[2 * p + e] * LOG2E
                m = jnp.maximum(jnp.max(logits, axis=-1, keepdims=True), sink2)
                pe.append(jnp.exp2(logits - m).astype(BF16))
                se.append(jnp.exp2(sink2 - m))
            pp.append(jnp.concatenate(pe, axis=1))
            ss.append(jnp.where(lo, se[0], se[1]))
        probs.append(jnp.concatenate(pp, axis=0))
        sink.append(ss)
    o = [_dot(probs[w], vb_ref[g, kvh]) for w, (g, kvh) in enumerate(groups)]
    for w, (g, kvh) in enumerate(groups):
        for i, p in enumerate(pairs_of(kvh)):
            op = o[w][i * blk:(i + 1) * blk]
            o_ref[g, :, cols[p]] = (op[:, 0:LANES] / (op[:, LANES:] + sink[w][i])).astype(BF16)

    for src_ref, dst_ref in zip(refs[:n_cast], refs[n_cast + 1:2 * n_cast + 1]):
        dst_ref[...] = src_ref[...].astype(BF16)


def _attention(sinks, freq8, pos4, proj3, kv3, q_col_block, later_weights):
    bsz, seq = proj3.shape[0], proj3.shape[1]
    blk = WINDOW
    members = 4
    steps = (bsz // members) * (seq // blk)
    cast_specs = []
    for w in later_weights:
        rows = w.shape[0] // steps
        assert rows * steps == w.shape[0] and rows % 16 == 0
        cast_specs.append(pl.BlockSpec((rows, w.shape[1]), lambda b, i: (b * (seq // blk) + i, 0)))
    expand, shift = _attn_constants()
    full = lambda a: pl.BlockSpec(a.shape, lambda b, i: (0,) * a.ndim)
    est = (2 * members * (blk * ATT_WIDTH * 2 * 2 + blk * 2 * KV_WIDTH * 2)
           + 2 * members * ATT_KV_HEADS * 4 * blk * 2 * LANES * 2 + members * ATT_WIDTH * blk * 2
           + members * 64 * blk * 4 * blk * 4)
    return pl.pallas_call(
        _attn_kernel,
        grid=(bsz // members, seq // blk),
        in_specs=[pl.BlockSpec(memory_space=pltpu.SMEM),
                  full(freq8), full(expand), full(shift),
                  pl.BlockSpec((members, None, 1, blk), lambda b, i: (b, i, 0, 0)),
                  pl.BlockSpec((members, blk, ATT_WIDTH), lambda b, i: (b, i, q_col_block)),
                  pl.BlockSpec((members, blk, 2 * KV_WIDTH), lambda b, i: (b, i, 0))] + cast_specs,
        out_specs=[pl.BlockSpec((members, blk, ATT_WIDTH), lambda b, i: (b, i, 0))] + cast_specs,
        out_shape=[jax.ShapeDtypeStruct((bsz, seq, ATT_WIDTH), BF16)]
                  + [jax.ShapeDtypeStruct(w.shape, BF16) for w in later_weights],
        scratch_shapes=[pltpu.VMEM((members, ATT_HEADS // 2 * blk, 2 * LANES), BF16),
                        pltpu.VMEM((members, ATT_KV_HEADS, 4 * blk, 2 * LANES), BF16),
                        pltpu.VMEM((members, ATT_KV_HEADS, 4 * blk, 2 * LANES), BF16)],
        compiler_params=pltpu.CompilerParams(
            dimension_semantics=("parallel", "arbitrary"), vmem_limit_bytes=_vmem_limit(est)),
        name="swa_attn",
    )(sinks, freq8, expand, shift, pos4, proj3, kv3, *later_weights)


def _hgrn_constants():
    t = np.arange(CHUNK)[:, None]
    j = np.arange(CHUNK)[None, :]
    tri = (j <= t).astype(np.float32)
    cumsum = np.concatenate([tri, tri], axis=1)
    reduce = np.zeros((DIAG, HG_DIM, LANES), np.float32)
    for d in range(DIAG):
        reduce[d, :, DIAG - 1 - d] = 1.0
    return jnp.asarray(cumsum, BF16), jnp.asarray(reduce.reshape(DIAG * HG_DIM, LANES), BF16)


def _hgrn_kernel(cm_ref, red_ref, lbp_ref, gn_ref, q_ref, f_ref, i_ref, g_ref, o_ref, st_ref, *,
                 tokens):
    @pl.when(pl.program_id(1) == 0)
    def _():
        st_ref[...] = jnp.zeros_like(st_ref)

    lbp = lbp_ref[...]
    e = jnp.exp(lbp - jnp.max(lbp, axis=0, keepdims=True))
    lb = e[0:1, :] / jnp.sum(e, axis=0, keepdims=True)
    half_key_scale = 0.5 * (1.0 - lb)

    row = lax.broadcasted_iota(jnp.int32, (CHUNK, CHUNK), 0)
    col = lax.broadcasted_iota(jnp.int32, (CHUNK, CHUNK), 1)
    same = {w: row // w == col // w for w in (DIAG, 2 * DIAG, 4 * DIAG)}
    groups = CHUNK // DIAG

    heads = [slice(h * HG_DIM, (h + 1) * HG_DIM) for h in range(HG_HEADS)]
    chunks = [slice(c * CHUNK, (c + 1) * CHUNK) for c in range(tokens // CHUNK)]
    per_chunk = []

    for rs in chunks:
        half_q = 0.5 * q_ref[rs, :].astype(F32)
        qs = half_q + half_q * jnp.tanh(half_q)
        key = half_key_scale - half_key_scale * jnp.tanh(0.5 * f_ref[rs, :].astype(F32))
        f = 1.0 - key
        val = i_ref[rs, :]

        b = _dot(cm_ref[...], jnp.concatenate(_split_bf16(jnp.log2(f), 2), axis=0).astype(BF16))
        eb = jnp.exp2(b)
        q_state = (qs * eb).astype(BF16)
        k_state = (key * jnp.exp2(b[CHUNK - 1:CHUNK, :] - b)).astype(BF16)
        decay = eb[CHUNK - 1:CHUNK, :]

        q_lv, k_lv = [], []
        for lv in LEVELS:
            qp, kp = [], []
            for piece in range(CHUNK // lv):
                ps = slice(piece * lv, (piece + 1) * lv)
                zero = jnp.zeros((lv, HG_WIDTH), F32)
                if piece % 2:
                    ref = b[piece * lv - 1:piece * lv, :]
                    qp.append(qs[ps] * jnp.exp2(b[ps] - ref)); kp.append(zero)
                else:
                    ref = b[(piece + 1) * lv - 1:(piece + 1) * lv, :]
                    qp.append(zero); kp.append(key[ps] * jnp.exp2(ref - b[ps]))
            q_lv.append(jnp.concatenate(qp, axis=0).astype(BF16))
            k_lv.append(jnp.concatenate(kp, axis=0).astype(BF16))

        carry = key.reshape(groups, DIAG, HG_WIDTH)
        f3 = f.reshape(groups, DIAG, HG_WIDTH)
        qs_bf16 = qs.astype(BF16)
        diag = [qs_bf16 * key.astype(BF16)]
        for d in range(1, DIAG):
            carry = f3 * pltpu.roll(carry, 1, axis=1)
            diag.append(qs_bf16 * carry.reshape(CHUNK, HG_WIDTH).astype(BF16))

        parts = [[_dot_nt(q_lv[n][:, hs], k_lv[n][:, hs]) for n in range(len(LEVELS))]
                 for hs in heads]
        r = _dot(jnp.concatenate([jnp.concatenate([dg[:, hs] for dg in diag], axis=1)
                                  for hs in heads], axis=0), red_ref[...])
        update = [_dot_tn(val[:, hs], k_state[:, hs]) for hs in heads]
        per_chunk.append((parts, r, update, q_state, val, decay))

    g_all = []
    for parts, r, _, _, _, _ in per_chunk:
        g_chunk = []
        for h in range(HG_HEADS):
            g_diag = pltpu.roll(r[h * CHUNK:(h + 1) * CHUNK], LANES - (DIAG - 1), axis=1,
                                stride=1, stride_axis=0)[:, 0:CHUNK]
            g_mat = jnp.where(same[DIAG], g_diag,
                              jnp.where(same[2 * DIAG], parts[h][2],
                                        jnp.where(same[4 * DIAG], parts[h][1], parts[h][0])))
            g_chunk.append(g_mat.astype(BF16))
        g_all.append(g_chunk)

    state = [st_ref[h] for h in range(HG_HEADS)]
    for rs, g_chunk, (_, _, update, q_state, val, decay) in zip(chunks, g_all, per_chunk):
        for h, hs in enumerate(heads):
            o = _dot_nt(q_state[:, hs], state[h].astype(BF16)) + _dot(g_chunk[h], val[:, hs])
            state[h] = state[h] * decay[:, hs] + update[h]
            y = _rms(o) * gn_ref[...]
            o_ref[rs, hs] = (y * _sigmoid(g_ref[rs, hs].astype(F32))).astype(BF16)
    for h in range(HG_HEADS):
        st_ref[h] = state[h]


def _hgrn(lb_params, gnorm, proj, bsz, seq, col_blocks):
    n = proj.shape[0]
    tokens = 8 * CHUNK
    nt = seq // tokens
    cmat, rmat = _hgrn_constants()
    row = lambda b, i: b * nt + i
    spec = lambda cb: pl.BlockSpec((tokens, HG_WIDTH), lambda b, i: (row(b, i), cb))
    full = lambda a: pl.BlockSpec(a.shape, lambda b, i: (0,) * a.ndim)
    est = 2 * 5 * tokens * HG_WIDTH * 2 + HG_HEADS * HG_DIM * HG_DIM * 4 + 40 * CHUNK * HG_WIDTH * 4
    return pl.pallas_call(
        functools.partial(_hgrn_kernel, tokens=tokens),
        grid=(bsz, nt),
        in_specs=[full(cmat), full(rmat), full(lb_params), full(gnorm),
                  spec(col_blocks[0]), spec(col_blocks[1]), spec(col_blocks[2]), spec(col_blocks[3])],
        out_specs=pl.BlockSpec((tokens, HG_WIDTH), lambda b, i: (row(b, i), 0)),
        out_shape=jax.ShapeDtypeStruct((n, HG_WIDTH), BF16),
        scratch_shapes=[pltpu.VMEM((HG_HEADS, HG_DIM, HG_DIM), F32)],
        compiler_params=pltpu.CompilerParams(
            dimension_semantics=("parallel", "arbitrary"), vmem_limit_bytes=_vmem_limit(est)),
        name="hgrn2",
    )(cmat, rmat, lb_params, gnorm, proj, proj, proj, proj)


def _mix_kernel(x_ref, mod_ref, g_ref, a_ref, hg_ref, ga_ref, gh_ref, wa_ref, wh_ref, wo_ref,
                o_ref, m_ref, *, tc):
    d = x_ref.shape[1]
    for c in range(d // tc):
        cs = slice(c * tc, (c + 1) * tc)
        ya = _dot(a_ref[...], wa_ref[:, cs])
        yh = _dot(hg_ref[...], wh_ref[:, cs])
        merged = (_sigmoid(ga_ref[:, cs].astype(F32)) * ya
                  + _sigmoid(gh_ref[:, cs].astype(F32)) * yh)
        m_ref[:, cs] = merged.astype(BF16)
    y = _dot(m_ref[...], wo_ref[...])
    o_ref[...] = _gated_norm_residual(x_ref[...], y, mod_ref[2:3, :], g_ref[...])


def _mix_out(x2, mod3, gain, attn, hg, proj, wa, wh, wo, seq, ga_block, gh_block):
    n, d = x2.shape
    tm, tc = 512, 512
    per_batch = seq // tm
    const = lambda shape: pl.BlockSpec(shape, lambda i: (0, 0), pipeline_mode=pl.Buffered(1))
    est = ((wa.size + wh.size + wo.size) * 2 + 2 * 2 * tm * d * 4 + 2 * 2 * tm * d * 2
           + 2 * 2 * tm * ATT_WIDTH * 2 + tm * d * 2 + 3 * tm * d * 4)
    return pl.pallas_call(
        functools.partial(_mix_kernel, tc=tc),
        grid=(n // tm,),
        in_specs=[pl.BlockSpec((tm, d), lambda i: (i, 0)),
                  pl.BlockSpec((None, N_MOD, d), lambda i: (i // per_batch, 0, 0)),
                  pl.BlockSpec((1, d), lambda i: (0, 0)),
                  pl.BlockSpec((tm, ATT_WIDTH), lambda i: (i, 0)),
                  pl.BlockSpec((tm, HG_WIDTH), lambda i: (i, 0)),
                  pl.BlockSpec((tm, d), lambda i: (i, ga_block)),
                  pl.BlockSpec((tm, d), lambda i: (i, gh_block)),
                  const(wa.shape), const(wh.shape), const(wo.shape)],
        out_specs=pl.BlockSpec((tm, d), lambda i: (i, 0)),
        out_shape=jax.ShapeDtypeStruct((n, d), F32),
        scratch_shapes=[pltpu.VMEM((tm, d), BF16)],
        compiler_params=pltpu.CompilerParams(
            dimension_semantics=("parallel",), vmem_limit_bytes=_vmem_limit(est)),
        name="mix_out",
    )(x2, mod3, gain, attn, hg, proj, proj, wa, wh, wo)


def _ffn_up_kernel(x_ref, mod_ref, gpre_ref, wg_ref, wu_ref, wd_ref, o_ref, wd_bf16_ref,
                   h_even_ref, h_odd_ref, *, slices):
    i, j = pl.program_id(0), pl.program_id(1)

    def modulated_norm(rows):
        y = _rms(x_ref[rows, :]) * gpre_ref[...]
        return (y * (1.0 + mod_ref[4:5, :]) + mod_ref[3:4, :]).astype(BF16)

    @pl.when((i == 0) & (j == 0))
    def _():
        h_even_ref[...] = modulated_norm(slice(None))

    def step(h_ref, h_next_ref):
        rows_per = x_ref.shape[0] // slices
        first = pl.multiple_of(jnp.clip(j - 1, 0, slices - 1) * rows_per, rows_per)
        rows = pl.ds(first, rows_per)
        h_next_ref[rows, :] = modulated_norm(rows)
        wd_bf16_ref[...] = wd_ref[...].astype(BF16)

        h = h_ref[...]
        g = _dot(h, wg_ref[...])
        u = _dot(h, wu_ref[...])
        o_ref[...] = (g * _sigmoid(g) * u).astype(BF16)

    pl.when(i % 2 == 0)(lambda: step(h_even_ref, h_odd_ref))
    pl.when(i % 2 == 1)(lambda: step(h_odd_ref, h_even_ref))


def _ffn_down_kernel(a_ref, x_ref, mod_ref, gpost_ref, wd_ref, o_ref):
    y = _dot(a_ref[...], wd_ref[...])
    o_ref[...] = _gated_norm_residual(x_ref[...], y, mod_ref[5:6, :], gpost_ref[...])


def _ffn(x1, mod3, gpre, gpost, w_in, w_down, seq):
    n, d = x1.shape
    hidden = w_down.shape[0]

    tm, th = 1024, 512
    nh = hidden // th
    per_batch = seq // tm
    last = n // tm - 1
    x_tile = lambda i, j: jnp.minimum(i + jnp.where(j > 0, 1, 0), last)
    steps = (n // tm) * nh
    wd_rows = hidden // steps
    assert wd_rows * steps == hidden and wd_rows % 16 == 0
    est = (2 * tm * d * 4 + 2 * tm * d * 2 + 2 * 2 * d * th * 2 + 2 * tm * th * 2
           + 4 * tm * th * 4 + 2 * wd_rows * d * 6)
    act, w_down_bf16 = pl.pallas_call(
        functools.partial(_ffn_up_kernel, slices=8),
        grid=(n // tm, nh),
        in_specs=[pl.BlockSpec((tm, d), lambda i, j: (x_tile(i, j), 0)),
                  pl.BlockSpec((None, N_MOD, d), lambda i, j: (x_tile(i, j) // per_batch, 0, 0)),
                  pl.BlockSpec((1, d), lambda i, j: (0, 0)),
                  pl.BlockSpec((d, th), lambda i, j: (0, j)),
                  pl.BlockSpec((d, th), lambda i, j: (0, nh + j)),
                  pl.BlockSpec((wd_rows, d), lambda i, j: (i * nh + j, 0))],
        out_specs=[pl.BlockSpec((tm, th), lambda i, j: (i, j)),
                   pl.BlockSpec((wd_rows, d), lambda i, j: (i * nh + j, 0))],
        out_shape=[jax.ShapeDtypeStruct((n, hidden), BF16),
                   jax.ShapeDtypeStruct((hidden, d), BF16)],
        scratch_shapes=[pltpu.VMEM((tm, d), BF16), pltpu.VMEM((tm, d), BF16)],
        compiler_params=pltpu.CompilerParams(
            dimension_semantics=("arbitrary", "arbitrary"), vmem_limit_bytes=_vmem_limit(est)),
        name="ffn_up",
    )(x1, mod3, gpre, w_in, w_in, w_down)
    w_down = w_down_bf16

    tm = 512
    per_batch = seq // tm
    est = hidden * d * 2 + 2 * tm * hidden * 2 + 2 * 2 * tm * d * 4 + 2 * tm * d * 4
    return pl.pallas_call(
        _ffn_down_kernel,
        grid=(n // tm,),
        in_specs=[pl.BlockSpec((tm, hidden), lambda i: (i, 0)),
                  pl.BlockSpec((tm, d), lambda i: (i, 0)),
                  pl.BlockSpec((None, N_MOD, d), lambda i: (i // per_batch, 0, 0)),
                  pl.BlockSpec((1, d), lambda i: (0, 0)),
                  pl.BlockSpec((hidden, d), lambda i: (0, 0), pipeline_mode=pl.Buffered(1))],
        out_specs=pl.BlockSpec((tm, d), lambda i: (i, 0)),
        out_shape=jax.ShapeDtypeStruct((n, d), F32),
        compiler_params=pltpu.CompilerParams(
            dimension_semantics=("parallel",), vmem_limit_bytes=_vmem_limit(est)),
        name="ffn_down",
    )(act, x1, mod3, gpost, w_down)


def kernel(x, c, positions, w_ada, b_ada, g_pre_mix, g_post_mix, g_pre_ffn, g_post_ffn, w_in,
           attn_sinks, w_attn_proj, hg_lower_bounds, hg_norm, w_hgrn_proj, w_out, w_ffn_in,
           w_ffn_out):
    bsz, seq, d = x.shape
    n = bsz * seq
    assert d == D_MODEL and w_ada.shape[0] == 1 and seq % 1024 == 0 and bsz % 4 == 0

    ga_block, gh_block = 0, 1
    q_block = 2 * D_MODEL // ATT_WIDTH
    hg_blocks = tuple((2 * D_MODEL + ATT_WIDTH) // HG_WIDTH + k for k in range(4))

    x2 = x.reshape(n, d)
    mod, w_in_bf16 = _adaln(c, w_ada[0], b_ada[0], w_in[0])
    mod3 = mod.reshape(bsz, N_MOD, d)

    proj, kv, w_up = _in_proj(x2, mod3, g_pre_mix, w_in_bf16, seq, ((w_ffn_in[0], (D_MODEL, LANES)),))

    inv_freq = ROPE_THETA ** (-jnp.arange(0, ROT_DIM, 2, dtype=F32) / ROT_DIM)
    freq8 = jnp.broadcast_to(inv_freq[:, None], (ROT_HALF, WINDOW))
    pos4 = positions.astype(F32).reshape(bsz, seq // WINDOW, 1, WINDOW)
    attn, wa, wh, wo = _attention(attn_sinks[0], freq8, pos4, proj.reshape(bsz, seq, -1),
                                  kv.reshape(bsz, seq, -1), q_block,
                                  (w_attn_proj[0], w_hgrn_proj[0], w_out[0]))
    attn = attn.reshape(n, ATT_WIDTH)

    hg = _hgrn(hg_lower_bounds, hg_norm, proj, bsz, seq, hg_blocks)

    x1 = _mix_out(x2, mod3, g_post_mix, attn, hg, proj, wa, wh, wo, seq, ga_block, gh_block)

    out = _ffn(x1, mod3, g_pre_ffn, g_post_ffn, w_up, w_ffn_out[0], seq)
    return out.reshape(bsz, seq, d)
```

```python
import functools
import math

import numpy as np
import jax
import jax.numpy as jnp
from jax import lax
from jax.experimental import pallas as pl
from jax.experimental.pallas import tpu as pltpu

F32 = jnp.float32
BF16 = jnp.bfloat16

D_MODEL = 2048
ATT_HEADS = 16
ATT_KV_HEADS = 2
HEAD_DIM = 64
ATT_WIDTH = ATT_HEADS * HEAD_DIM
KV_WIDTH = ATT_KV_HEADS * HEAD_DIM
WINDOW = 128
ROT_DIM = HEAD_DIM // 4
ROT_HALF = ROT_DIM // 2
ROPE_THETA = 500000.0
HG_HEADS = 8
HG_DIM = 128
HG_WIDTH = HG_HEADS * HG_DIM
N_MOD = 6
EPS = 1e-6
LOG2E = math.log2(math.e)
MASK_BIAS = -1e30

LANES = 128
SUBLANES = 8
V7X_VMEM_BYTES = 64 * 1024 * 1024

CHUNK = 64
DIAG = SUBLANES
LEVELS = (32, 16, 8)
ATT_MEMBERS = 8


def _vmem_limit(estimate_bytes):
    return int(min(estimate_bytes * 3 // 2, V7X_VMEM_BYTES - 8 * 1024 * 1024))


def _dot(a, b):
    return jnp.dot(a, b, preferred_element_type=F32)


def _dot_nt(a, b):
    return lax.dot_general(a, b, (((1,), (1,)), ((), ())), preferred_element_type=F32)


def _dot_tn(a, b):
    return lax.dot_general(a, b, (((0,), (0,)), ((), ())), preferred_element_type=F32)


def _rms(t):
    return t * lax.rsqrt(jnp.mean(t * t, axis=-1, keepdims=True) + EPS)


def _gated_norm_residual(x, y, gate, gain):
    return x + _rms(y) * (gate * gain)


def _sigmoid(t):
    return 0.5 + 0.5 * jnp.tanh(0.5 * t)


def _split_bf16(x, parts):
    out = []
    for _ in range(parts - 1):
        p = x.astype(BF16).astype(F32)
        out.append(p)
        x = x - p
    out.append(x.astype(BF16).astype(F32))
    return out


def _adaln_kernel(c_ref, w_ref, b_ref, win_ref, o_ref, win_bf16_ref):
    bsz = c_ref.shape[0]
    c_hi, c_lo = _split_bf16(c_ref[...], 2)
    w_hi, w_lo = _split_bf16(w_ref[...], 2)
    by_hi = _dot(jnp.concatenate([c_hi, c_lo], axis=0).astype(BF16), w_hi.astype(BF16))
    o_ref[...] = (by_hi[0:bsz] + by_hi[bsz:] + _dot(c_hi.astype(BF16), w_lo.astype(BF16))
                  + b_ref[...])
    win_bf16_ref[...] = win_ref[...].astype(BF16)


def _adaln(c, w, b, w_in):
    bsz, d = c.shape
    n = w.shape[1]
    steps = 16
    tn = n // steps
    rows = w_in.shape[0] // steps
    assert tn * steps == n and tn % LANES == 0 and rows * steps == w_in.shape[0] and rows % 16 == 0
    est = (2 * (d * tn * 4) + 2 * bsz * d * 4 + 4 * bsz * tn * 4 + 3 * d * tn * 4
           + 2 * rows * w_in.shape[1] * 6)
    return pl.pallas_call(
        _adaln_kernel,
        grid=(steps,),
        in_specs=[pl.BlockSpec((bsz, d), lambda j: (0, 0)),
                  pl.BlockSpec((d, tn), lambda j: (0, j)),
                  pl.BlockSpec((1, tn), lambda j: (0, j)),
                  pl.BlockSpec((rows, w_in.shape[1]), lambda j: (j, 0))],
        out_specs=[pl.BlockSpec((bsz, tn), lambda j: (0, j)),
                   pl.BlockSpec((rows, w_in.shape[1]), lambda j: (j, 0))],
        out_shape=[jax.ShapeDtypeStruct((bsz, n), F32),
                   jax.ShapeDtypeStruct(w_in.shape, BF16)],
        compiler_params=pltpu.CompilerParams(
            dimension_semantics=("arbitrary",), vmem_limit_bytes=_vmem_limit(est)),
        name="adaln_mod",
    )(c, w, b.reshape(1, n), w_in)


def _in_proj_kernel(x_ref, mod_ref, g_ref, w_ref, wkv_ref, *refs, slices, n_cast):
    cast_src, (o_ref, okv_ref) = refs[:n_cast], refs[n_cast:n_cast + 2]
    cast_dst, (h_even_ref, h_odd_ref) = refs[n_cast + 2:2 * n_cast + 2], refs[2 * n_cast + 2:]
    i, j = pl.program_id(0), pl.program_id(1)

    def modulated_norm(rows):
        y = _rms(x_ref[rows, :]) * g_ref[...]
        return (y * (1.0 + mod_ref[1:2, :]) + mod_ref[0:1, :]).astype(BF16)

    @pl.when((i == 0) & (j == 0))
    def _():
        h_even_ref[...] = modulated_norm(slice(None))

    def step(h_ref, h_next_ref):
        @pl.when(j == 0)
        def _():
            okv_ref[...] = _dot(h_ref[...], wkv_ref[...]).astype(BF16)

        rows_per = x_ref.shape[0] // slices
        first = pl.multiple_of(jnp.clip(j - 1, 0, slices - 1) * rows_per, rows_per)
        rows = pl.ds(first, rows_per)
        h_next_ref[rows, :] = modulated_norm(rows)
        for src_ref, dst_ref in zip(cast_src, cast_dst):
            dst_ref[...] = src_ref[...].astype(BF16)
        o_ref[...] = _dot(h_ref[...], w_ref[...]).astype(BF16)

    pl.when(i % 2 == 0)(lambda: step(h_even_ref, h_odd_ref))
    pl.when(i % 2 == 1)(lambda: step(h_odd_ref, h_even_ref))


def _in_proj(x2, mod3, gain, w_in, seq, later_weights):
    n, d = x2.shape
    tm, tn = 1024, 1024
    per_batch = seq // tm
    o_kv = ATT_WIDTH
    o_qh = o_kv + 2 * KV_WIDTH
    o_ga = o_qh + 4 * HG_WIDTH
    cols = w_in.shape[1] - 2 * KV_WIDTH
    n_gate = 2 * D_MODEL // tn
    n_q = ATT_WIDTH // tn

    def src_col(j):
        unit = tn // LANES
        lane_tile = jnp.where(j < n_gate, o_ga // LANES + unit * j,
                              jnp.where(j < n_gate + n_q, unit * (j - n_gate),
                                        o_qh // LANES + unit * (j - n_gate - n_q)))
        return lane_tile * LANES

    last = n // tm - 1
    x_tile = lambda i, j: jnp.minimum(i + jnp.where(j > 0, 1, 0), last)

    n_col = cols // tn
    cast_specs, cast_bytes = [], 0
    for w, (br, bc) in later_weights:
        rb, cb = w.shape[0] // br, w.shape[1] // bc
        assert rb * br == w.shape[0] and cb * bc == w.shape[1] and rb * cb <= (n // tm) * n_col
        def block(i, j, rb=rb, cb=cb):
            k = jnp.minimum(i * n_col + j, rb * cb - 1)
            return k // cb, k % cb
        cast_specs.append(pl.BlockSpec((br, bc), block))
        cast_bytes += 2 * br * bc * 6

    est = (2 * tm * d * 4 + 2 * tm * d * 2 + 2 * d * tn * 2 + 2 * tm * tn * 2
           + 2 * d * KV_WIDTH * 2 * 2 + 2 * tm * 2 * KV_WIDTH * 2 + tm * tn * 4 + tm * d * 4 // 8
           + cast_bytes)
    return pl.pallas_call(
        functools.partial(_in_proj_kernel, slices=8, n_cast=len(later_weights)),
        grid=(n // tm, cols // tn),
        in_specs=[pl.BlockSpec((tm, d), lambda i, j: (x_tile(i, j), 0)),
                  pl.BlockSpec((None, N_MOD, d), lambda i, j: (x_tile(i, j) // per_batch, 0, 0)),
                  pl.BlockSpec((1, d), lambda i, j: (0, 0)),
                  pl.BlockSpec((pl.Element(d), pl.Element(tn)), lambda i, j: (0, src_col(j))),
                  pl.BlockSpec((d, 2 * KV_WIDTH), lambda i, j: (0, o_kv // (2 * KV_WIDTH)))]
                 + cast_specs,
        out_specs=[pl.BlockSpec((tm, tn), lambda i, j: (i, j)),
                   pl.BlockSpec((tm, 2 * KV_WIDTH), lambda i, j: (i, 0))] + cast_specs,
        out_shape=[jax.ShapeDtypeStruct((n, cols), BF16),
                   jax.ShapeDtypeStruct((n, 2 * KV_WIDTH), BF16)]
                  + [jax.ShapeDtypeStruct(w.shape, BF16) for w, _ in later_weights],
        scratch_shapes=[pltpu.VMEM((tm, d), BF16), pltpu.VMEM((tm, d), BF16)],
        compiler_params=pltpu.CompilerParams(
            dimension_semantics=("arbitrary", "arbitrary"), vmem_limit_bytes=_vmem_limit(est)),
        name="in_proj",
    )(x2, mod3, gain, w_in, w_in, *[w for w, _ in later_weights])


def _attn_constants():
    lane = np.arange(LANES)
    dim = lane % HEAD_DIM
    expand = np.zeros((2, 3, ROT_HALF, 3, LANES), np.float32)
    for j in range(ROT_HALF):
        expand[0, :, j, 0, (dim < ROT_DIM) & (dim % ROT_HALF == j)] = 1.0
        expand[1, :, j, 1, (dim < ROT_HALF) & (dim == j)] = -1.0
        expand[1, :, j, 2, (dim >= ROT_HALF) & (dim < ROT_DIM) & (dim - ROT_HALF == j)] = 1.0
    expand = expand.reshape(6 * ROT_HALF, 3 * LANES)
    shift = np.zeros((LANES, 2 * LANES), np.float32)
    for l in range(LANES):
        if l + ROT_HALF < LANES:
            shift[l + ROT_HALF, l] = 1.0
        if l - ROT_HALF >= 0:
            shift[l - ROT_HALF, LANES + l] = 1.0
    return jnp.asarray(expand, BF16), jnp.asarray(shift, BF16)


def _attn_kernel(sink_ref, freq_ref, expand_ref, shift_ref, pos_ref, q_ref, kv_ref, *refs):
    n_cast = (len(refs) - 4) // 2
    o_ref, (qb_ref, kb_ref, vb_ref) = refs[n_cast], refs[2 * n_cast + 1:]
    members = q_ref.shape[0]
    nblk = pl.program_id(1)
    parity = nblk % 2
    blk = WINDOW
    lane = lax.broadcasted_iota(jnp.int32, (blk, LANES), 1)
    lo = lane < HEAD_DIM
    dim = lane % HEAD_DIM
    kj = lax.broadcasted_iota(jnp.int32, (blk, blk), 0)
    qi = lax.broadcasted_iota(jnp.int32, (blk, blk), 1)

    @pl.when(nblk == 0)
    def _():
        kb_ref[...] = jnp.zeros_like(kb_ref)
        vb_ref[...] = jnp.zeros_like(vb_ref)
        eye = jnp.where(kj == qi, 1.0, 0.0).astype(BF16)
        ones_lo = jnp.where(lo, 1.0, 0.0).astype(BF16)
        ones_hi = jnp.where(lo, 0.0, 1.0).astype(BF16)
        for g in range(members):
            for p in range(ATT_HEADS // 2):
                qb_ref[g, p * blk:(p + 1) * blk, LANES:] = eye
            for kvh in range(ATT_KV_HEADS):
                for quarter in range(4):
                    rows = slice(quarter * blk, (quarter + 1) * blk)
                    vb_ref[g, kvh, rows, LANES:] = ones_hi if quarter >= 2 else ones_lo

    def rope(x_bf16, tables):
        t_cos, t_lo, t_hi = tables
        ud = _dot(x_bf16, shift_ref[...])
        return x_bf16.astype(F32) * t_cos + ud[:, 0:LANES] * t_lo + ud[:, LANES:] * t_hi

    k_tabs, q_tabs = [], []
    for g in range(members):
        ang = freq_ref[...] * pos_ref[g]
        parts = _split_bf16(jnp.cos(ang), 3) + _split_bf16(jnp.sin(ang), 3)
        tabs = _dot_tn(jnp.concatenate(parts, axis=0).astype(BF16), expand_ref[...])
        kt = (tabs[:, 0:LANES] + jnp.where(dim < ROT_DIM, 0.0, 1.0),
              tabs[:, LANES:2 * LANES], tabs[:, 2 * LANES:3 * LANES])
        k_tabs.append(kt)
        q_tabs.append(tuple(t * (HEAD_DIM ** -0.5 * LOG2E) for t in kt))

    bias = []
    for half in range(2):
        is_cur = parity == half
        valid = (is_cur & (kj <= qi)) | ((~is_cur) & (nblk > 0) & (kj > qi))
        bias.append(jnp.where(valid, 0.0, MASK_BIAS).astype(BF16))
    for g in range(members):
        k = rope(kv_ref[g, :, 0:KV_WIDTH], k_tabs[g])
        v = kv_ref[g, :, KV_WIDTH:].astype(F32)
        k_sw = pltpu.roll(k, HEAD_DIM, axis=1)
        v_sw = pltpu.roll(v, HEAD_DIM, axis=1)
        for kvh in range(ATT_KV_HEADS):
            for slot in range(2):
                ks = (k, k_sw)[kvh ^ slot]
                vs = (v, v_sw)[kvh ^ slot]
                keep = lo if slot == 0 else ~lo
                rows = pl.ds(pl.multiple_of((2 * slot + parity) * blk, blk), blk)
                kb_ref[g, kvh, rows, 0:LANES] = jnp.where(keep, ks, 0.0).astype(BF16)
                vb_ref[g, kvh, rows, 0:LANES] = jnp.where(keep, vs, 0.0).astype(BF16)
                for half in range(2):
                    rows = slice((2 * slot + half) * blk, (2 * slot + half + 1) * blk)
                    kb_ref[g, kvh, rows, LANES:] = bias[half]

    pairs_per_kv = ATT_HEADS // ATT_KV_HEADS // 2
    cols = [slice(p * LANES, (p + 1) * LANES) for p in range(ATT_HEADS // 2)]
    groups = [(g, kvh) for g in range(members) for kvh in range(ATT_KV_HEADS)]
    pairs_of = lambda kvh: range(kvh * pairs_per_kv, (kvh + 1) * pairs_per_kv)
    for g in range(members):
        stacked = jnp.concatenate([q_ref[g, :, cs] for cs in cols], axis=0)
        tables = tuple(jnp.concatenate([t] * len(cols), axis=0) for t in q_tabs[g])
        qb_ref[g, :, 0:LANES] = rope(stacked, tables).astype(BF16)
    rows_of = lambda kvh: slice(kvh * pairs_per_kv * blk, (kvh + 1) * pairs_per_kv * blk)
    s = [_dot_nt(qb_ref[g, rows_of(kvh), :], kb_ref[g, kvh]) for g, kvh in groups]
    probs, sink = [], []
    for w, (g, kvh) in enumerate(groups):
        pp, ss = [], []
        for i, p in enumerate(pairs_of(kvh)):
            pe, se = [], []
            for e in range(2):
                logits = s[w][i * blk:(i + 1) * blk, e * 2 * blk:(e + 1) * 2 * blk]
                sink2 = sink_ref[2 * p + e] * LOG2E
                m = jnp.maximum(jnp.max(logits, axis=-1, keepdims=True), sink2)
                pe.append(jnp.exp2(logits - m).astype(BF16))
                se.append(jnp.exp2(sink2 - m))
            pp.append(jnp.concatenate(pe, axis=1))
            ss.append(jnp.where(lo, se[0], se[1]))
        probs.append(jnp.concatenate(pp, axis=0))
        sink.append(ss)
    o = [_dot(probs[w], vb_ref[g, kvh]) for w, (g, kvh) in enumerate(groups)]
    for w, (g, kvh) in enumerate(groups):
        for i, p in enumerate(pairs_of(kvh)):
            op = o[w][i * blk:(i + 1) * blk]
            o_ref[g, :, cols[p]] = (op[:, 0:LANES] / (op[:, LANES:] + sink[w][i])).astype(BF16)

    for src_ref, dst_ref in zip(refs[:n_cast], refs[n_cast + 1:2 * n_cast + 1]):
        dst_ref[...] = src_ref[...].astype(BF16)


def _attention(sinks, freq8, pos4, proj3, kv3, q_col_block, later_weights):
    bsz, seq = proj3.shape[0], proj3.shape[1]
    blk = WINDOW
    members = ATT_MEMBERS
    steps = (bsz // members) * (seq // blk)
    cast_specs = []
    for w in later_weights:
        rows = w.shape[0] // steps
        assert rows * steps == w.shape[0] and rows % 16 == 0
        cast_specs.append(pl.BlockSpec((rows, w.shape[1]), lambda b, i: (b * (seq // blk) + i, 0)))
    expand, shift = _attn_constants()
    full = lambda a: pl.BlockSpec(a.shape, lambda b, i: (0,) * a.ndim)
    est = (2 * members * (blk * ATT_WIDTH * 2 * 2 + blk * 2 * KV_WIDTH * 2)
           + 2 * members * ATT_KV_HEADS * 4 * blk * 2 * LANES * 2 + members * ATT_WIDTH * blk * 2
           + members * 64 * blk * 4 * blk * 4)
    return pl.pallas_call(
        _attn_kernel,
        grid=(bsz // members, seq // blk),
        in_specs=[pl.BlockSpec(memory_space=pltpu.SMEM),
                  full(freq8), full(expand), full(shift),
                  pl.BlockSpec((members, None, 1, blk), lambda b, i: (b, i, 0, 0)),
                  pl.BlockSpec((members, blk, ATT_WIDTH), lambda b, i: (b, i, q_col_block)),
                  pl.BlockSpec((members, blk, 2 * KV_WIDTH), lambda b, i: (b, i, 0))] + cast_specs,
        out_specs=[pl.BlockSpec((members, blk, ATT_WIDTH), lambda b, i: (b, i, 0))] + cast_specs,
        out_shape=[jax.ShapeDtypeStruct((bsz, seq, ATT_WIDTH), BF16)]
                  + [jax.ShapeDtypeStruct(w.shape, BF16) for w in later_weights],
        scratch_shapes=[pltpu.VMEM((members, ATT_HEADS // 2 * blk, 2 * LANES), BF16),
                        pltpu.VMEM((members, ATT_KV_HEADS, 4 * blk, 2 * LANES), BF16),
                        pltpu.VMEM((members, ATT_KV_HEADS, 4 * blk, 2 * LANES), BF16)],
        compiler_params=pltpu.CompilerParams(
            dimension_semantics=("parallel", "arbitrary"), vmem_limit_bytes=_vmem_limit(est)),
        name="swa_attn",
    )(sinks, freq8, expand, shift, pos4, proj3, kv3, *later_weights)


def _hgrn_constants():
    t = np.arange(CHUNK)[:, None]
    j = np.arange(CHUNK)[None, :]
    tri = (j <= t).astype(np.float32)
    cumsum = np.concatenate([tri, tri], axis=1)
    reduce = np.zeros((DIAG, HG_DIM, LANES), np.float32)
    for d in range(DIAG):
        reduce[d, :, DIAG - 1 - d] = 1.0
    return jnp.asarray(cumsum, BF16), jnp.asarray(reduce.reshape(DIAG * HG_DIM, LANES), BF16)


def _hgrn_kernel(cm_ref, red_ref, lbp_ref, gn_ref, q_ref, f_ref, i_ref, g_ref, o_ref, st_ref, *,
                 tokens):
    @pl.when(pl.program_id(1) == 0)
    def _():
        st_ref[...] = jnp.zeros_like(st_ref)

    lbp = lbp_ref[...]
    e = jnp.exp(lbp - jnp.max(lbp, axis=0, keepdims=True))
    lb = e[0:1, :] / jnp.sum(e, axis=0, keepdims=True)
    half_key_scale = 0.5 * (1.0 - lb)

    row = lax.broadcasted_iota(jnp.int32, (CHUNK, CHUNK), 0)
    col = lax.broadcasted_iota(jnp.int32, (CHUNK, CHUNK), 1)
    same = {w: row // w == col // w for w in (DIAG, 2 * DIAG, 4 * DIAG)}
    groups = CHUNK // DIAG

    heads = [slice(h * HG_DIM, (h + 1) * HG_DIM) for h in range(HG_HEADS)]
    chunks = [slice(c * CHUNK, (c + 1) * CHUNK) for c in range(tokens // CHUNK)]
    per_chunk = []

    for rs in chunks:
        half_q = 0.5 * q_ref[rs, :].astype(F32)
        qs = half_q + half_q * jnp.tanh(half_q)
        key = half_key_scale - half_key_scale * jnp.tanh(0.5 * f_ref[rs, :].astype(F32))
        f = 1.0 - key
        val = i_ref[rs, :]

        b = _dot(cm_ref[...], jnp.concatenate(_split_bf16(jnp.log2(f), 2), axis=0).astype(BF16))
        eb = jnp.exp2(b)
        q_state = (qs * eb).astype(BF16)
        k_state = (key * jnp.exp2(b[CHUNK - 1:CHUNK, :] - b)).astype(BF16)
        decay = eb[CHUNK - 1:CHUNK, :]

        q_lv, k_lv = [], []
        for lv in LEVELS:
            qp, kp = [], []
            for piece in range(CHUNK // lv):
                ps = slice(piece * lv, (piece + 1) * lv)
                zero = jnp.zeros((lv, HG_WIDTH), F32)
                if piece % 2:
                    ref = b[piece * lv - 1:piece * lv, :]
                    qp.append(qs[ps] * jnp.exp2(b[ps] - ref)); kp.append(zero)
                else:
                    ref = b[(piece + 1) * lv - 1:(piece + 1) * lv, :]
                    qp.append(zero); kp.append(key[ps] * jnp.exp2(ref - b[ps]))
            q_lv.append(jnp.concatenate(qp, axis=0).astype(BF16))
            k_lv.append(jnp.concatenate(kp, axis=0).astype(BF16))

        carry = key.reshape(groups, DIAG, HG_WIDTH)
        f3 = f.reshape(groups, DIAG, HG_WIDTH)
        qs_bf16 = qs.astype(BF16)
        diag = [qs_bf16 * key.astype(BF16)]
        for d in range(1, DIAG):
            carry = f3 * pltpu.roll(carry, 1, axis=1)
            diag.append(qs_bf16 * carry.reshape(CHUNK, HG_WIDTH).astype(BF16))

        parts = [[_dot_nt(q_lv[n][:, hs], k_lv[n][:, hs]) for n in range(len(LEVELS))]
                 for hs in heads]
        r = _dot(jnp.concatenate([jnp.concatenate([dg[:, hs] for dg in diag], axis=1)
                                  for hs in heads], axis=0), red_ref[...])
        update = [_dot_tn(val[:, hs], k_state[:, hs]) for hs in heads]
        per_chunk.append((parts, r, update, q_state, val, decay))

    g_all = []
    for parts, r, _, _, _, _ in per_chunk:
        g_chunk = []
        for h in range(HG_HEADS):
            g_diag = pltpu.roll(r[h * CHUNK:(h + 1) * CHUNK], LANES - (DIAG - 1), axis=1,
                                stride=1, stride_axis=0)[:, 0:CHUNK]
            g_mat = jnp.where(same[DIAG], g_diag,
                              jnp.where(same[2 * DIAG], parts[h][2],
                                        jnp.where(same[4 * DIAG], parts[h][1], parts[h][0])))
            g_chunk.append(g_mat.astype(BF16))
        g_all.append(g_chunk)

    state = [st_ref[h] for h in range(HG_HEADS)]
    for rs, g_chunk, (_, _, update, q_state, val, decay) in zip(chunks, g_all, per_chunk):
        for h, hs in enumerate(heads):
            o = _dot_nt(q_state[:, hs], state[h].astype(BF16)) + _dot(g_chunk[h], val[:, hs])
            state[h] = state[h] * decay[:, hs] + update[h]
            y = _rms(o) * gn_ref[...]
            o_ref[rs, hs] = (y * _sigmoid(g_ref[rs, hs].astype(F32))).astype(BF16)
    for h in range(HG_HEADS):
        st_ref[h] = state[h]


def _hgrn(lb_params, gnorm, proj, bsz, seq, col_blocks):
    n = proj.shape[0]
    tokens = 8 * CHUNK
    nt = seq // tokens
    cmat, rmat = _hgrn_constants()
    row = lambda b, i: b * nt + i
    spec = lambda cb: pl.BlockSpec((tokens, HG_WIDTH), lambda b, i: (row(b, i), cb))
    full = lambda a: pl.BlockSpec(a.shape, lambda b, i: (0,) * a.ndim)
    est = 2 * 5 * tokens * HG_WIDTH * 2 + HG_HEADS * HG_DIM * HG_DIM * 4 + 40 * CHUNK * HG_WIDTH * 4
    return pl.pallas_call(
        functools.partial(_hgrn_kernel, tokens=tokens),
        grid=(bsz, nt),
        in_specs=[full(cmat), full(rmat), full(lb_params), full(gnorm),
                  spec(col_blocks[0]), spec(col_blocks[1]), spec(col_blocks[2]), spec(col_blocks[3])],
        out_specs=pl.BlockSpec((tokens, HG_WIDTH), lambda b, i: (row(b, i), 0)),
        out_shape=jax.ShapeDtypeStruct((n, HG_WIDTH), BF16),
        scratch_shapes=[pltpu.VMEM((HG_HEADS, HG_DIM, HG_DIM), F32)],
        compiler_params=pltpu.CompilerParams(
            dimension_semantics=("parallel", "arbitrary"), vmem_limit_bytes=_vmem_limit(est)),
        name="hgrn2",
    )(cmat, rmat, lb_params, gnorm, proj, proj, proj, proj)


def _mix_kernel(x_ref, mod_ref, g_ref, a_ref, hg_ref, ga_ref, gh_ref, wa_ref, wh_ref, wo_ref,
                o_ref, m_ref, *, tc):
    d = x_ref.shape[1]
    for c in range(d // tc):
        cs = slice(c * tc, (c + 1) * tc)
        ya = _dot(a_ref[...], wa_ref[:, cs])
        yh = _dot(hg_ref[...], wh_ref[:, cs])
        merged = (_sigmoid(ga_ref[:, cs].astype(F32)) * ya
                  + _sigmoid(gh_ref[:, cs].astype(F32)) * yh)
        m_ref[:, cs] = merged.astype(BF16)
    y = _dot(m_ref[...], wo_ref[...])
    o_ref[...] = _gated_norm_residual(x_ref[...], y, mod_ref[2:3, :], g_ref[...])


def _mix_out(x2, mod3, gain, attn, hg, proj, wa, wh, wo, seq, ga_block, gh_block):
    n, d = x2.shape
    tm, tc = 512, 512
    per_batch = seq // tm
    const = lambda shape: pl.BlockSpec(shape, lambda i: (0, 0), pipeline_mode=pl.Buffered(1))
    est = ((wa.size + wh.size + wo.size) * 2 + 2 * 2 * tm * d * 4 + 2 * 2 * tm * d * 2
           + 2 * 2 * tm * ATT_WIDTH * 2 + tm * d * 2 + 3 * tm * d * 4)
    return pl.pallas_call(
        functools.partial(_mix_kernel, tc=tc),
        grid=(n // tm,),
        in_specs=[pl.BlockSpec((tm, d), lambda i: (i, 0)),
                  pl.BlockSpec((None, N_MOD, d), lambda i: (i // per_batch, 0, 0)),
                  pl.BlockSpec((1, d), lambda i: (0, 0)),
                  pl.BlockSpec((tm, ATT_WIDTH), lambda i: (i, 0)),
                  pl.BlockSpec((tm, HG_WIDTH), lambda i: (i, 0)),
                  pl.BlockSpec((tm, d), lambda i: (i, ga_block)),
                  pl.BlockSpec((tm, d), lambda i: (i, gh_block)),
                  const(wa.shape), const(wh.shape), const(wo.shape)],
        out_specs=pl.BlockSpec((tm, d), lambda i: (i, 0)),
        out_shape=jax.ShapeDtypeStruct((n, d), F32),
        scratch_shapes=[pltpu.VMEM((tm, d), BF16)],
        compiler_params=pltpu.CompilerParams(
            dimension_semantics=("parallel",), vmem_limit_bytes=_vmem_limit(est)),
        name="mix_out",
    )(x2, mod3, gain, attn, hg, proj, proj, wa, wh, wo)


def _ffn_up_kernel(x_ref, mod_ref, gpre_ref, wg_ref, wu_ref, wd_ref, o_ref, wd_bf16_ref,
                   h_even_ref, h_odd_ref, *, slices):
    i, j = pl.program_id(0), pl.program_id(1)

    def modulated_norm(rows):
        y = _rms(x_ref[rows, :]) * gpre_ref[...]
        return (y * (1.0 + mod_ref[4:5, :]) + mod_ref[3:4, :]).astype(BF16)

    @pl.when((i == 0) & (j == 0))
    def _():
        h_even_ref[...] = modulated_norm(slice(None))

    def step(h_ref, h_next_ref):
        rows_per = x_ref.shape[0] // slices
        first = pl.multiple_of(jnp.clip(j - 1, 0, slices - 1) * rows_per, rows_per)
        rows = pl.ds(first, rows_per)
        h_next_ref[rows, :] = modulated_norm(rows)
        wd_bf16_ref[...] = wd_ref[...].astype(BF16)

        h = h_ref[...]
        g = _dot(h, wg_ref[...])
        u = _dot(h, wu_ref[...])
        o_ref[...] = (g * _sigmoid(g) * u).astype(BF16)

    pl.when(i % 2 == 0)(lambda: step(h_even_ref, h_odd_ref))
    pl.when(i % 2 == 1)(lambda: step(h_odd_ref, h_even_ref))


def _ffn_down_kernel(a_ref, x_ref, mod_ref, gpost_ref, wd_ref, o_ref):
    y = _dot(a_ref[...], wd_ref[...])
    o_ref[...] = _gated_norm_residual(x_ref[...], y, mod_ref[5:6, :], gpost_ref[...])


def _ffn(x1, mod3, gpre, gpost, w_in, w_down, seq):
    n, d = x1.shape
    hidden = w_down.shape[0]

    tm, th = 1024, 512
    nh = hidden // th
    per_batch = seq // tm
    last = n // tm - 1
    x_tile = lambda i, j: jnp.minimum(i + jnp.where(j > 0, 1, 0), last)
    steps = (n // tm) * nh
    wd_rows = hidden // steps
    assert wd_rows * steps == hidden and wd_rows % 16 == 0
    est = (2 * tm * d * 4 + 2 * tm * d * 2 + 2 * 2 * d * th * 2 + 2 * tm * th * 2
           + 4 * tm * th * 4 + 2 * wd_rows * d * 6)
    act, w_down_bf16 = pl.pallas_call(
        functools.partial(_ffn_up_kernel, slices=8),
        grid=(n // tm, nh),
        in_specs=[pl.BlockSpec((tm, d), lambda i, j: (x_tile(i, j), 0)),
                  pl.BlockSpec((None, N_MOD, d), lambda i, j: (x_tile(i, j) // per_batch, 0, 0)),
                  pl.BlockSpec((1, d), lambda i, j: (0, 0)),
                  pl.BlockSpec((d, th), lambda i, j: (0, j)),
                  pl.BlockSpec((d, th), lambda i, j: (0, nh + j)),
                  pl.BlockSpec((wd_rows, d), lambda i, j: (i * nh + j, 0))],
        out_specs=[pl.BlockSpec((tm, th), lambda i, j: (i, j)),
                   pl.BlockSpec((wd_rows, d), lambda i, j: (i * nh + j, 0))],
        out_shape=[jax.ShapeDtypeStruct((n, hidden), BF16),
                   jax.ShapeDtypeStruct((hidden, d), BF16)],
        scratch_shapes=[pltpu.VMEM((tm, d), BF16), pltpu.VMEM((tm, d), BF16)],
        compiler_params=pltpu.CompilerParams(
            dimension_semantics=("arbitrary", "arbitrary"), vmem_limit_bytes=_vmem_limit(est)),
        name="ffn_up",
    )(x1, mod3, gpre, w_in, w_in, w_down)
    w_down = w_down_bf16

    tm = 512
    per_batch = seq // tm
    est = hidden * d * 2 + 2 * tm * hidden * 2 + 2 * 2 * tm * d * 4 + 2 * tm * d * 4
    return pl.pallas_call(
        _ffn_down_kernel,
        grid=(n // tm,),
        in_specs=[pl.BlockSpec((tm, hidden), lambda i: (i, 0)),
                  pl.BlockSpec((tm, d), lambda i: (i, 0)),
                  pl.BlockSpec((None, N_MOD, d), lambda i: (i // per_batch, 0, 0)),
                  pl.BlockSpec((1, d), lambda i: (0, 0)),
                  pl.BlockSpec((hidden, d), lambda i: (0, 0), pipeline_mode=pl.Buffered(1))],
        out_specs=pl.BlockSpec((tm, d), lambda i: (i, 0)),
        out_shape=jax.ShapeDtypeStruct((n, d), F32),
        compiler_params=pltpu.CompilerParams(
            dimension_semantics=("parallel",), vmem_limit_bytes=_vmem_limit(est)),
        name="ffn_down",
    )(act, x1, mod3, gpost, w_down)


def kernel(x, c, positions, w_ada, b_ada, g_pre_mix, g_post_mix, g_pre_ffn, g_post_ffn, w_in,
           attn_sinks, w_attn_proj, hg_lower_bounds, hg_norm, w_hgrn_proj, w_out, w_ffn_in,
           w_ffn_out):
    bsz, seq, d = x.shape
    n = bsz * seq
    assert d == D_MODEL and w_ada.shape[0] == 1 and seq % 1024 == 0 and bsz % ATT_MEMBERS == 0

    ga_block, gh_block = 0, 1
    q_block = 2 * D_MODEL // ATT_WIDTH
    hg_blocks = tuple((2 * D_MODEL + ATT_WIDTH) // HG_WIDTH + k for k in range(4))

    x2 = x.reshape(n, d)
    mod, w_in_bf16 = _adaln(c, w_ada[0], b_ada[0], w_in[0])
    mod3 = mod.reshape(bsz, N_MOD, d)

    proj, kv, w_up = _in_proj(x2, mod3, g_pre_mix, w_in_bf16, seq, ((w_ffn_in[0], (D_MODEL, LANES)),))

    inv_freq = ROPE_THETA ** (-jnp.arange(0, ROT_DIM, 2, dtype=F32) / ROT_DIM)
    freq8 = jnp.broadcast_to(inv_freq[:, None], (ROT_HALF, WINDOW))
    pos4 = positions.astype(F32).reshape(bsz, seq // WINDOW, 1, WINDOW)
    attn, wa, wh, wo = _attention(attn_sinks[0], freq8, pos4, proj.reshape(bsz, seq, -1),
                                  kv.reshape(bsz, seq, -1), q_block,
                                  (w_attn_proj[0], w_hgrn_proj[0], w_out[0]))
    attn = attn.reshape(n, ATT_WIDTH)

    hg = _hgrn(hg_lower_bounds, hg_norm, proj, bsz, seq, hg_blocks)

    x1 = _mix_out(x2, mod3, g_post_mix, attn, hg, proj, wa, wh, wo, seq, ga_block, gh_block)

    out = _ffn(x1, mod3, g_pre_ffn, g_post_ffn, w_up, w_ffn_out[0], seq)
    return out.reshape(bsz, seq, d)
```

```python
import functools
import math

import numpy as np
import jax
import jax.numpy as jnp
from jax import lax
from jax.experimental import pallas as pl
from jax.experimental.pallas import tpu as pltpu

F32 = jnp.float32
BF16 = jnp.bfloat16

D_MODEL = 2048
ATT_HEADS = 16
ATT_KV_HEADS = 2
HEAD_DIM = 64
ATT_WIDTH = ATT_HEADS * HEAD_DIM
KV_WIDTH = ATT_KV_HEADS * HEAD_DIM
WINDOW = 128
ROT_DIM = HEAD_DIM // 4
ROT_HALF = ROT_DIM // 2
ROPE_THETA = 500000.0
HG_HEADS = 8
HG_DIM = 128
HG_WIDTH = HG_HEADS * HG_DIM
N_MOD = 6
EPS = 1e-6
LOG2E = math.log2(math.e)
MASK_BIAS = -1e30

LANES = 128
SUBLANES = 8
V7X_VMEM_BYTES = 64 * 1024 * 1024

CHUNK = 64
DIAG = SUBLANES
LEVELS = (32, 16, 8)
ATT_MEMBERS = 8


def _vmem_limit(estimate_bytes):
    return int(min(estimate_bytes * 3 // 2, V7X_VMEM_BYTES - 8 * 1024 * 1024))


def _dot(a, b):
    return jnp.dot(a, b, preferred_element_type=F32)


def _dot_nt(a, b):
    return lax.dot_general(a, b, (((1,), (1,)), ((), ())), preferred_element_type=F32)


def _dot_tn(a, b):
    return lax.dot_general(a, b, (((0,), (0,)), ((), ())), preferred_element_type=F32)


def _rms(t):
    return t * lax.rsqrt(jnp.mean(t * t, axis=-1, keepdims=True) + EPS)


def _gated_norm_residual(x, y, gate, gain):
    return x + _rms(y) * (gate * gain)


def _sigmoid(t):
    return 0.5 + 0.5 * jnp.tanh(0.5 * t)


def _split_bf16(x, parts):
    out = []
    for _ in range(parts - 1):
        p = x.astype(BF16).astype(F32)
        out.append(p)
        x = x - p
    out.append(x.astype(BF16).astype(F32))
    return out


def _adaln_kernel(c_ref, w_ref, b_ref, win_ref, o_ref, win_bf16_ref):
    bsz = c_ref.shape[0]
    c_hi, c_lo = _split_bf16(c_ref[...], 2)
    w_hi, w_lo = _split_bf16(w_ref[...], 2)
    by_hi = _dot(jnp.concatenate([c_hi, c_lo], axis=0).astype(BF16), w_hi.astype(BF16))
    o_ref[...] = (by_hi[0:bsz] + by_hi[bsz:] + _dot(c_hi.astype(BF16), w_lo.astype(BF16))
                  + b_ref[...])
    win_bf16_ref[...] = win_ref[...].astype(BF16)


def _adaln(c, w, b, w_in):
    bsz, d = c.shape
    n = w.shape[1]
    steps = 16
    tn = n // steps
    rows = w_in.shape[0] // steps
    assert tn * steps == n and tn % LANES == 0 and rows * steps == w_in.shape[0] and rows % 16 == 0
    est = (2 * (d * tn * 4) + 2 * bsz * d * 4 + 4 * bsz * tn * 4 + 3 * d * tn * 4
           + 2 * rows * w_in.shape[1] * 6)
    return pl.pallas_call(
        _adaln_kernel,
        grid=(steps,),
        in_specs=[pl.BlockSpec((bsz, d), lambda j: (0, 0)),
                  pl.BlockSpec((d, tn), lambda j: (0, j)),
                  pl.BlockSpec((1, tn), lambda j: (0, j)),
                  pl.BlockSpec((rows, w_in.shape[1]), lambda j: (j, 0))],
        out_specs=[pl.BlockSpec((bsz, tn), lambda j: (0, j)),
                   pl.BlockSpec((rows, w_in.shape[1]), lambda j: (j, 0))],
        out_shape=[jax.ShapeDtypeStruct((bsz, n), F32),
                   jax.ShapeDtypeStruct(w_in.shape, BF16)],
        compiler_params=pltpu.CompilerParams(
            dimension_semantics=("arbitrary",), vmem_limit_bytes=_vmem_limit(est)),
        name="adaln_mod",
    )(c, w, b.reshape(1, n), w_in)


def _in_proj_kernel(x_ref, mod_ref, g_ref, w_ref, wkv_ref, *refs, slices, n_cast):
    cast_src, (o_ref, okv_ref) = refs[:n_cast], refs[n_cast:n_cast + 2]
    cast_dst, (h_even_ref, h_odd_ref) = refs[n_cast + 2:2 * n_cast + 2], refs[2 * n_cast + 2:]
    i, j = pl.program_id(0), pl.program_id(1)

    def modulated_norm(rows):
        y = _rms(x_ref[rows, :]) * g_ref[...]
        return (y * (1.0 + mod_ref[1:2, :]) + mod_ref[0:1, :]).astype(BF16)

    @pl.when((i == 0) & (j == 0))
    def _():
        h_even_ref[...] = modulated_norm(slice(None))

    def step(h_ref, h_next_ref):
        @pl.when(j == 0)
        def _():
            okv_ref[...] = _dot(h_ref[...], wkv_ref[...]).astype(BF16)

        rows_per = x_ref.shape[0] // slices
        first = pl.multiple_of(jnp.clip(j - 1, 0, slices - 1) * rows_per, rows_per)
        rows = pl.ds(first, rows_per)
        h_next_ref[rows, :] = modulated_norm(rows)
        for src_ref, dst_ref in zip(cast_src, cast_dst):
            dst_ref[...] = src_ref[...].astype(BF16)
        o_ref[...] = _dot(h_ref[...], w_ref[...]).astype(BF16)

    pl.when(i % 2 == 0)(lambda: step(h_even_ref, h_odd_ref))
    pl.when(i % 2 == 1)(lambda: step(h_odd_ref, h_even_ref))


def _in_proj(x2, mod3, gain, w_in, seq, later_weights):
    n, d = x2.shape
    tm, tn = 1024, 1024
    per_batch = seq // tm
    o_kv = ATT_WIDTH
    o_qh = o_kv + 2 * KV_WIDTH
    o_ga = o_qh + 4 * HG_WIDTH
    cols = w_in.shape[1] - 2 * KV_WIDTH
    n_gate = 2 * D_MODEL // tn
    n_q = ATT_WIDTH // tn

    def src_col(j):
        unit = tn // LANES
        lane_tile = jnp.where(j < n_gate, o_ga // LANES + unit * j,
                              jnp.where(j < n_gate + n_q, unit * (j - n_gate),
                                        o_qh // LANES + unit * (j - n_gate - n_q)))
        return lane_tile * LANES

    last = n // tm - 1
    x_tile = lambda i, j: jnp.minimum(i + jnp.where(j > 0, 1, 0), last)

    n_col = cols // tn
    cast_specs, cast_bytes = [], 0
    for w, (br, bc) in later_weights:
        rb, cb = w.shape[0] // br, w.shape[1] // bc
        assert rb * br == w.shape[0] and cb * bc == w.shape[1] and rb * cb <= (n // tm) * n_col
        def block(i, j, rb=rb, cb=cb):
            k = jnp.minimum(i * n_col + j, rb * cb - 1)
            return k // cb, k % cb
        cast_specs.append(pl.BlockSpec((br, bc), block))
        cast_bytes += 2 * br * bc * 6

    est = (2 * tm * d * 4 + 2 * tm * d * 2 + 2 * d * tn * 2 + 2 * tm * tn * 2
           + 2 * d * KV_WIDTH * 2 * 2 + 2 * tm * 2 * KV_WIDTH * 2 + tm * tn * 4 + tm * d * 4 // 8
           + cast_bytes)
    return pl.pallas_call(
        functools.partial(_in_proj_kernel, slices=8, n_cast=len(later_weights)),
        grid=(n // tm, cols // tn),
        in_specs=[pl.BlockSpec((tm, d), lambda i, j: (x_tile(i, j), 0)),
                  pl.BlockSpec((None, N_MOD, d), lambda i, j: (x_tile(i, j) // per_batch, 0, 0)),
                  pl.BlockSpec((1, d), lambda i, j: (0, 0)),
                  pl.BlockSpec((pl.Element(d), pl.Element(tn)), lambda i, j: (0, src_col(j))),
                  pl.BlockSpec((d, 2 * KV_WIDTH), lambda i, j: (0, o_kv // (2 * KV_WIDTH)))]
                 + cast_specs,
        out_specs=[pl.BlockSpec((tm, tn), lambda i, j: (i, j)),
                   pl.BlockSpec((tm, 2 * KV_WIDTH), lambda i, j: (i, 0))] + cast_specs,
        out_shape=[jax.ShapeDtypeStruct((n, cols), BF16),
                   jax.ShapeDtypeStruct((n, 2 * KV_WIDTH), BF16)]
                  + [jax.ShapeDtypeStruct(w.shape, BF16) for w, _ in later_weights],
        scratch_shapes=[pltpu.VMEM((tm, d), BF16), pltpu.VMEM((tm, d), BF16)],
        compiler_params=pltpu.CompilerParams(
            dimension_semantics=("arbitrary", "arbitrary"), vmem_limit_bytes=_vmem_limit(est)),
        name="in_proj",
    )(x2, mod3, gain, w_in, w_in, *[w for w, _ in later_weights])


def _attn_constants():
    lane = np.arange(LANES)
    dim = lane % HEAD_DIM
    expand = np.zeros((2, 3, ROT_HALF, 3, LANES), np.float32)
    for j in range(ROT_HALF):
        expand[0, :, j, 0, (dim < ROT_DIM) & (dim % ROT_HALF == j)] = 1.0
        expand[1, :, j, 1, (dim < ROT_HALF) & (dim == j)] = -1.0
        expand[1, :, j, 2, (dim >= ROT_HALF) & (dim < ROT_DIM) & (dim - ROT_HALF == j)] = 1.0
    expand = expand.reshape(6 * ROT_HALF, 3 * LANES)
    shift = np.zeros((LANES, 2 * LANES), np.float32)
    for l in range(LANES):
        if l + ROT_HALF < LANES:
            shift[l + ROT_HALF, l] = 1.0
        if l - ROT_HALF >= 0:
            shift[l - ROT_HALF, LANES + l] = 1.0
    return jnp.asarray(expand, BF16), jnp.asarray(shift, BF16)


def _attn_kernel(sink_ref, freq_ref, expand_ref, shift_ref, pos_ref, q_ref, kv_ref, *refs):
    n_cast = (len(refs) - 4) // 2
    o_ref, (qb_ref, kb_ref, vb_ref) = refs[n_cast], refs[2 * n_cast + 1:]
    members = q_ref.shape[0]
    nblk = pl.program_id(1)
    parity = nblk % 2
    blk = WINDOW
    lane = lax.broadcasted_iota(jnp.int32, (blk, LANES), 1)
    lo = lane < HEAD_DIM
    dim = lane % HEAD_DIM
    kj = lax.broadcasted_iota(jnp.int32, (blk, blk), 0)
    qi = lax.broadcasted_iota(jnp.int32, (blk, blk), 1)

    @pl.when(nblk == 0)
    def _():
        kb_ref[...] = jnp.zeros_like(kb_ref)
        vb_ref[...] = jnp.zeros_like(vb_ref)
        ones_lo = jnp.where(lo, 1.0, 0.0).astype(BF16)
        ones_hi = jnp.where(lo, 0.0, 1.0).astype(BF16)
        for g in range(members):
            for kvh in range(ATT_KV_HEADS):
                for quarter in range(4):
                    rows = slice(quarter * blk, (quarter + 1) * blk)
                    vb_ref[g, kvh, rows, LANES:] = ones_hi if quarter >= 2 else ones_lo

    def rope(x_bf16, tables):
        t_cos, t_lo, t_hi = tables
        ud = _dot(x_bf16, shift_ref[...])
        return x_bf16.astype(F32) * t_cos + ud[:, 0:LANES] * t_lo + ud[:, LANES:] * t_hi

    k_tabs, q_tabs = [], []
    for g in range(members):
        ang = freq_ref[...] * pos_ref[g]
        parts = _split_bf16(jnp.cos(ang), 3) + _split_bf16(jnp.sin(ang), 3)
        tabs = _dot_tn(jnp.concatenate(parts, axis=0).astype(BF16), expand_ref[...])
        kt = (tabs[:, 0:LANES] + jnp.where(dim < ROT_DIM, 0.0, 1.0),
              tabs[:, LANES:2 * LANES], tabs[:, 2 * LANES:3 * LANES])
        k_tabs.append(kt)
        q_tabs.append(tuple(t * (HEAD_DIM ** -0.5 * LOG2E) for t in kt))

    q_i = lax.broadcasted_iota(jnp.int32, (blk, 2 * blk), 0)
    k_j = lax.broadcasted_iota(jnp.int32, (blk, 2 * blk), 1)
    is_cur = (k_j >= blk) == (parity == 1)
    k_in = k_j % blk
    valid = (is_cur & (k_in <= q_i)) | ((~is_cur) & (nblk > 0) & (k_in > q_i))

    for g in range(members):
        k = rope(kv_ref[g, :, 0:KV_WIDTH], k_tabs[g])
        v = kv_ref[g, :, KV_WIDTH:].astype(F32)
        k_sw = pltpu.roll(k, HEAD_DIM, axis=1)
        v_sw = pltpu.roll(v, HEAD_DIM, axis=1)
        for kvh in range(ATT_KV_HEADS):
            for slot in range(2):
                ks = (k, k_sw)[kvh ^ slot]
                vs = (v, v_sw)[kvh ^ slot]
                keep = lo if slot == 0 else ~lo
                rows = pl.ds(pl.multiple_of((2 * slot + parity) * blk, blk), blk)
                kb_ref[g, kvh, rows, 0:LANES] = jnp.where(keep, ks, 0.0).astype(BF16)
                vb_ref[g, kvh, rows, 0:LANES] = jnp.where(keep, vs, 0.0).astype(BF16)

    pairs_per_kv = ATT_HEADS // ATT_KV_HEADS // 2
    cols = [slice(p * LANES, (p + 1) * LANES) for p in range(ATT_HEADS // 2)]
    groups = [(g, kvh) for g in range(members) for kvh in range(ATT_KV_HEADS)]
    pairs_of = lambda kvh: range(kvh * pairs_per_kv, (kvh + 1) * pairs_per_kv)
    for g in range(members):
        stacked = jnp.concatenate([q_ref[g, :, cs] for cs in cols], axis=0)
        tables = tuple(jnp.concatenate([t] * len(cols), axis=0) for t in q_tabs[g])
        qb_ref[g] = rope(stacked, tables).astype(BF16)
    rows_of = lambda kvh: slice(kvh * pairs_per_kv * blk, (kvh + 1) * pairs_per_kv * blk)
    s = [_dot_nt(qb_ref[g, rows_of(kvh), :], kb_ref[g, kvh]) for g, kvh in groups]
    probs, sink = [], []
    for w, (g, kvh) in enumerate(groups):
        pp, ss = [], []
        for i, p in enumerate(pairs_of(kvh)):
            pe, se = [], []
            for e in range(2):
                logits = jnp.where(valid, s[w][i * blk:(i + 1) * blk, e * 2 * blk:(e + 1) * 2 * blk],
                                   MASK_BIAS)
                sink2 = sink_ref[2 * p + e] * LOG2E
                m = jnp.maximum(jnp.max(logits, axis=-1, keepdims=True), sink2)
                pe.append(jnp.exp2(logits - m).astype(BF16))
                se.append(jnp.exp2(sink2 - m))
            pp.append(jnp.concatenate(pe, axis=1))
            ss.append(jnp.where(lo, se[0], se[1]))
        probs.append(jnp.concatenate(pp, axis=0))
        sink.append(ss)
    o = [_dot(probs[w], vb_ref[g, kvh]) for w, (g, kvh) in enumerate(groups)]
    for w, (g, kvh) in enumerate(groups):
        for i, p in enumerate(pairs_of(kvh)):
            op = o[w][i * blk:(i + 1) * blk]
            o_ref[g, :, cols[p]] = (op[:, 0:LANES] / (op[:, LANES:] + sink[w][i])).astype(BF16)

    for src_ref, dst_ref in zip(refs[:n_cast], refs[n_cast + 1:2 * n_cast + 1]):
        dst_ref[...] = src_ref[...].astype(BF16)


def _attention(sinks, freq8, pos4, proj3, kv3, q_col_block, later_weights):
    bsz, seq = proj3.shape[0], proj3.shape[1]
    blk = WINDOW
    members = ATT_MEMBERS
    steps = (bsz // members) * (seq // blk)
    cast_specs = []
    for w in later_weights:
        rows = w.shape[0] // steps
        assert rows * steps == w.shape[0] and rows % 16 == 0
        cast_specs.append(pl.BlockSpec((rows, w.shape[1]), lambda b, i: (b * (seq // blk) + i, 0)))
    expand, shift = _attn_constants()
    full = lambda a: pl.BlockSpec(a.shape, lambda b, i: (0,) * a.ndim)
    est = (2 * members * (blk * ATT_WIDTH * 2 * 2 + blk * 2 * KV_WIDTH * 2)
           + 2 * members * ATT_KV_HEADS * 4 * blk * 2 * LANES * 2 + members * ATT_WIDTH * blk * 2
           + members * 64 * blk * 4 * blk * 4)
    return pl.pallas_call(
        _attn_kernel,
        grid=(bsz // members, seq // blk),
        in_specs=[pl.BlockSpec(memory_space=pltpu.SMEM),
                  full(freq8), full(expand), full(shift),
                  pl.BlockSpec((members, None, 1, blk), lambda b, i: (b, i, 0, 0)),
                  pl.BlockSpec((members, blk, ATT_WIDTH), lambda b, i: (b, i, q_col_block)),
                  pl.BlockSpec((members, blk, 2 * KV_WIDTH), lambda b, i: (b, i, 0))] + cast_specs,
        out_specs=[pl.BlockSpec((members, blk, ATT_WIDTH), lambda b, i: (b, i, 0))] + cast_specs,
        out_shape=[jax.ShapeDtypeStruct((bsz, seq, ATT_WIDTH), BF16)]
                  + [jax.ShapeDtypeStruct(w.shape, BF16) for w in later_weights],
        scratch_shapes=[pltpu.VMEM((members, ATT_HEADS // 2 * blk, LANES), BF16),
                        pltpu.VMEM((members, ATT_KV_HEADS, 4 * blk, LANES), BF16),
                        pltpu.VMEM((members, ATT_KV_HEADS, 4 * blk, 2 * LANES), BF16)],
        compiler_params=pltpu.CompilerParams(
            dimension_semantics=("parallel", "arbitrary"), vmem_limit_bytes=_vmem_limit(est)),
        name="swa_attn",
    )(sinks, freq8, expand, shift, pos4, proj3, kv3, *later_weights)


def _hgrn_constants():
    t = np.arange(CHUNK)[:, None]
    j = np.arange(CHUNK)[None, :]
    tri = (j <= t).astype(np.float32)
    cumsum = np.concatenate([tri, tri], axis=1)
    reduce = np.zeros((DIAG, HG_DIM, LANES), np.float32)
    for d in range(DIAG):
        reduce[d, :, DIAG - 1 - d] = 1.0
    return jnp.asarray(cumsum, BF16), jnp.asarray(reduce.reshape(DIAG * HG_DIM, LANES), BF16)


def _hgrn_kernel(cm_ref, red_ref, lbp_ref, gn_ref, q_ref, f_ref, i_ref, g_ref, o_ref, st_ref, *,
                 tokens):
    @pl.when(pl.program_id(1) == 0)
    def _():
        st_ref[...] = jnp.zeros_like(st_ref)

    lbp = lbp_ref[...]
    e = jnp.exp(lbp - jnp.max(lbp, axis=0, keepdims=True))
    lb = e[0:1, :] / jnp.sum(e, axis=0, keepdims=True)
    half_key_scale = 0.5 * (1.0 - lb)

    row = lax.broadcasted_iota(jnp.int32, (CHUNK, CHUNK), 0)
    col = lax.broadcasted_iota(jnp.int32, (CHUNK, CHUNK), 1)
    same = {w: row // w == col // w for w in (DIAG, 2 * DIAG, 4 * DIAG)}
    groups = CHUNK // DIAG

    heads = [slice(h * HG_DIM, (h + 1) * HG_DIM) for h in range(HG_HEADS)]
    chunks = [slice(c * CHUNK, (c + 1) * CHUNK) for c in range(tokens // CHUNK)]
    per_chunk = []

    for rs in chunks:
        half_q = 0.5 * q_ref[rs, :].astype(F32)
        qs = half_q + half_q * jnp.tanh(half_q)
        key = half_key_scale - half_key_scale * jnp.tanh(0.5 * f_ref[rs, :].astype(F32))
        f = 1.0 - key
        val = i_ref[rs, :]

        b = _dot(cm_ref[...], jnp.concatenate(_split_bf16(jnp.log2(f), 2), axis=0).astype(BF16))
        eb = jnp.exp2(b)
        q_state = (qs * eb).astype(BF16)
        k_state = (key * jnp.exp2(b[CHUNK - 1:CHUNK, :] - b)).astype(BF16)
        decay = eb[CHUNK - 1:CHUNK, :]

        q_lv, k_lv = [], []
        for lv in LEVELS:
            qp, kp = [], []
            for piece in range(CHUNK // lv):
                ps = slice(piece * lv, (piece + 1) * lv)
                zero = jnp.zeros((lv, HG_WIDTH), F32)
                if piece % 2:
                    ref = b[piece * lv - 1:piece * lv, :]
                    qp.append(qs[ps] * jnp.exp2(b[ps] - ref)); kp.append(zero)
                else:
                    ref = b[(piece + 1) * lv - 1:(piece + 1) * lv, :]
                    qp.append(zero); kp.append(key[ps] * jnp.exp2(ref - b[ps]))
            q_lv.append(jnp.concatenate(qp, axis=0).astype(BF16))
            k_lv.append(jnp.concatenate(kp, axis=0).astype(BF16))

        carry = key.reshape(groups, DIAG, HG_WIDTH)
        f3 = f.reshape(groups, DIAG, HG_WIDTH)
        qs_bf16 = qs.astype(BF16)
        diag = [qs_bf16 * key.astype(BF16)]
        for d in range(1, DIAG):
            carry = f3 * pltpu.roll(carry, 1, axis=1)
            diag.append(qs_bf16 * carry.reshape(CHUNK, HG_WIDTH).astype(BF16))

        parts = [[_dot_nt(q_lv[n][:, hs], k_lv[n][:, hs]) for n in range(len(LEVELS))]
                 for hs in heads]
        r = _dot(jnp.concatenate([jnp.concatenate([dg[:, hs] for dg in diag], axis=1)
                                  for hs in heads], axis=0), red_ref[...])
        update = [_dot_tn(val[:, hs], k_state[:, hs]) for hs in heads]
        per_chunk.append((parts, r, update, q_state, val, decay))

    g_all = []
    for parts, r, _, _, _, _ in per_chunk:
        g_chunk = []
        for h in range(HG_HEADS):
            g_diag = pltpu.roll(r[h * CHUNK:(h + 1) * CHUNK], LANES - (DIAG - 1), axis=1,
                                stride=1, stride_axis=0)[:, 0:CHUNK]
            g_mat = jnp.where(same[DIAG], g_diag,
                              jnp.where(same[2 * DIAG], parts[h][2],
                                        jnp.where(same[4 * DIAG], parts[h][1], parts[h][0])))
            g_chunk.append(g_mat.astype(BF16))
        g_all.append(g_chunk)

    state = [st_ref[h] for h in range(HG_HEADS)]
    for rs, g_chunk, (_, _, update, q_state, val, decay) in zip(chunks, g_all, per_chunk):
        for h, hs in enumerate(heads):
            o = _dot_nt(q_state[:, hs], state[h].astype(BF16)) + _dot(g_chunk[h], val[:, hs])
            state[h] = state[h] * decay[:, hs] + update[h]
            y = _rms(o) * gn_ref[...]
            o_ref[rs, hs] = (y * _sigmoid(g_ref[rs, hs].astype(F32))).astype(BF16)
    for h in range(HG_HEADS):
        st_ref[h] = state[h]


def _hgrn(lb_params, gnorm, proj, bsz, seq, col_blocks):
    n = proj.shape[0]
    tokens = 8 * CHUNK
    nt = seq // tokens
    cmat, rmat = _hgrn_constants()
    row = lambda b, i: b * nt + i
    spec = lambda cb: pl.BlockSpec((tokens, HG_WIDTH), lambda b, i: (row(b, i), cb))
    full = lambda a: pl.BlockSpec(a.shape, lambda b, i: (0,) * a.ndim)
    est = 2 * 5 * tokens * HG_WIDTH * 2 + HG_HEADS * HG_DIM * HG_DIM * 4 + 40 * CHUNK * HG_WIDTH * 4
    return pl.pallas_call(
        functools.partial(_hgrn_kernel, tokens=tokens),
        grid=(bsz, nt),
        in_specs=[full(cmat), full(rmat), full(lb_params), full(gnorm),
                  spec(col_blocks[0]), spec(col_blocks[1]), spec(col_blocks[2]), spec(col_blocks[3])],
        out_specs=pl.BlockSpec((tokens, HG_WIDTH), lambda b, i: (row(b, i), 0)),
        out_shape=jax.ShapeDtypeStruct((n, HG_WIDTH), BF16),
        scratch_shapes=[pltpu.VMEM((HG_HEADS, HG_DIM, HG_DIM), F32)],
        compiler_params=pltpu.CompilerParams(
            dimension_semantics=("parallel", "arbitrary"), vmem_limit_bytes=_vmem_limit(est)),
        name="hgrn2",
    )(cmat, rmat, lb_params, gnorm, proj, proj, proj, proj)


def _mix_kernel(x_ref, mod_ref, g_ref, a_ref, hg_ref, ga_ref, gh_ref, wa_ref, wh_ref, wo_ref,
                o_ref, m_ref, *, tc):
    d = x_ref.shape[1]
    for c in range(d // tc):
        cs = slice(c * tc, (c + 1) * tc)
        ya = _dot(a_ref[...], wa_ref[:, cs])
        yh = _dot(hg_ref[...], wh_ref[:, cs])
        merged = (_sigmoid(ga_ref[:, cs].astype(F32)) * ya
                  + _sigmoid(gh_ref[:, cs].astype(F32)) * yh)
        m_ref[:, cs] = merged.astype(BF16)
    y = _dot(m_ref[...], wo_ref[...])
    o_ref[...] = _gated_norm_residual(x_ref[...], y, mod_ref[2:3, :], g_ref[...])


def _mix_out(x2, mod3, gain, attn, hg, proj, wa, wh, wo, seq, ga_block, gh_block):
    n, d = x2.shape
    tm, tc = 512, 512
    per_batch = seq // tm
    const = lambda shape: pl.BlockSpec(shape, lambda i: (0, 0), pipeline_mode=pl.Buffered(1))
    est = ((wa.size + wh.size + wo.size) * 2 + 2 * 2 * tm * d * 4 + 2 * 2 * tm * d * 2
           + 2 * 2 * tm * ATT_WIDTH * 2 + tm * d * 2 + 3 * tm * d * 4)
    return pl.pallas_call(
        functools.partial(_mix_kernel, tc=tc),
        grid=(n // tm,),
        in_specs=[pl.BlockSpec((tm, d), lambda i: (i, 0)),
                  pl.BlockSpec((None, N_MOD, d), lambda i: (i // per_batch, 0, 0)),
                  pl.BlockSpec((1, d), lambda i: (0, 0)),
                  pl.BlockSpec((tm, ATT_WIDTH), lambda i: (i, 0)),
                  pl.BlockSpec((tm, HG_WIDTH), lambda i: (i, 0)),
                  pl.BlockSpec((tm, d), lambda i: (i, ga_block)),
                  pl.BlockSpec((tm, d), lambda i: (i, gh_block)),
                  const(wa.shape), const(wh.shape), const(wo.shape)],
        out_specs=pl.BlockSpec((tm, d), lambda i: (i, 0)),
        out_shape=jax.ShapeDtypeStruct((n, d), F32),
        scratch_shapes=[pltpu.VMEM((tm, d), BF16)],
        compiler_params=pltpu.CompilerParams(
            dimension_semantics=("parallel",), vmem_limit_bytes=_vmem_limit(est)),
        name="mix_out",
    )(x2, mod3, gain, attn, hg, proj, proj, wa, wh, wo)


def _ffn_up_kernel(x_ref, mod_ref, gpre_ref, wg_ref, wu_ref, wd_ref, o_ref, wd_bf16_ref,
                   h_even_ref, h_odd_ref, *, slices):
    i, j = pl.program_id(0), pl.program_id(1)

    def modulated_norm(rows):
        y = _rms(x_ref[rows, :]) * gpre_ref[...]
        return (y * (1.0 + mod_ref[4:5, :]) + mod_ref[3:4, :]).astype(BF16)

    @pl.when((i == 0) & (j == 0))
    def _():
        h_even_ref[...] = modulated_norm(slice(None))

    def step(h_ref, h_next_ref):
        rows_per = x_ref.shape[0] // slices
        first = pl.multiple_of(jnp.clip(j - 1, 0, slices - 1) * rows_per, rows_per)
        rows = pl.ds(first, rows_per)
        h_next_ref[rows, :] = modulated_norm(rows)
        wd_bf16_ref[...] = wd_ref[...].astype(BF16)

        h = h_ref[...]
        g = _dot(h, wg_ref[...])
        u = _dot(h, wu_ref[...])
        o_ref[...] = (g * _sigmoid(g) * u).astype(BF16)

    pl.when(i % 2 == 0)(lambda: step(h_even_ref, h_odd_ref))
    pl.when(i % 2 == 1)(lambda: step(h_odd_ref, h_even_ref))


def _ffn_down_kernel(a_ref, x_ref, mod_ref, gpost_ref, wd_ref, o_ref):
    y = _dot(a_ref[...], wd_ref[...])
    o_ref[...] = _gated_norm_residual(x_ref[...], y, mod_ref[5:6, :], gpost_ref[...])


def _ffn(x1, mod3, gpre, gpost, w_in, w_down, seq):
    n, d = x1.shape
    hidden = w_down.shape[0]

    tm, th = 1024, 512
    nh = hidden // th
    per_batch = seq // tm
    last = n // tm - 1
    x_tile = lambda i, j: jnp.minimum(i + jnp.where(j > 0, 1, 0), last)
    steps = (n // tm) * nh
    wd_rows = hidden // steps
    assert wd_rows * steps == hidden and wd_rows % 16 == 0
    est = (2 * tm * d * 4 + 2 * tm * d * 2 + 2 * 2 * d * th * 2 + 2 * tm * th * 2
           + 4 * tm * th * 4 + 2 * wd_rows * d * 6)
    act, w_down_bf16 = pl.pallas_call(
        functools.partial(_ffn_up_kernel, slices=8),
        grid=(n // tm, nh),
        in_specs=[pl.BlockSpec((tm, d), lambda i, j: (x_tile(i, j), 0)),
                  pl.BlockSpec((None, N_MOD, d), lambda i, j: (x_tile(i, j) // per_batch, 0, 0)),
                  pl.BlockSpec((1, d), lambda i, j: (0, 0)),
                  pl.BlockSpec((d, th), lambda i, j: (0, j)),
                  pl.BlockSpec((d, th), lambda i, j: (0, nh + j)),
                  pl.BlockSpec((wd_rows, d), lambda i, j: (i * nh + j, 0))],
        out_specs=[pl.BlockSpec((tm, th), lambda i, j: (i, j)),
                   pl.BlockSpec((wd_rows, d), lambda i, j: (i * nh + j, 0))],
        out_shape=[jax.ShapeDtypeStruct((n, hidden), BF16),
                   jax.ShapeDtypeStruct((hidden, d), BF16)],
        scratch_shapes=[pltpu.VMEM((tm, d), BF16), pltpu.VMEM((tm, d), BF16)],
        compiler_params=pltpu.CompilerParams(
            dimension_semantics=("arbitrary", "arbitrary"), vmem_limit_bytes=_vmem_limit(est)),
        name="ffn_up",
    )(x1, mod3, gpre, w_in, w_in, w_down)
    w_down = w_down_bf16

    tm = 512
    per_batch = seq // tm
    est = hidden * d * 2 + 2 * tm * hidden * 2 + 2 * 2 * tm * d * 4 + 2 * tm * d * 4
    return pl.pallas_call(
        _ffn_down_kernel,
        grid=(n // tm,),
        in_specs=[pl.BlockSpec((tm, hidden), lambda i: (i, 0)),
                  pl.BlockSpec((tm, d), lambda i: (i, 0)),
                  pl.BlockSpec((None, N_MOD, d), lambda i: (i // per_batch, 0, 0)),
                  pl.BlockSpec((1, d), lambda i: (0, 0)),
                  pl.BlockSpec((hidden, d), lambda i: (0, 0), pipeline_mode=pl.Buffered(1))],
        out_specs=pl.BlockSpec((tm, d), lambda i: (i, 0)),
        out_shape=jax.ShapeDtypeStruct((n, d), F32),
        compiler_params=pltpu.CompilerParams(
            dimension_semantics=("parallel",), vmem_limit_bytes=_vmem_limit(est)),
        name="ffn_down",
    )(act, x1, mod3, gpost, w_down)


def kernel(x, c, positions, w_ada, b_ada, g_pre_mix, g_post_mix, g_pre_ffn, g_post_ffn, w_in,
           attn_sinks, w_attn_proj, hg_lower_bounds, hg_norm, w_hgrn_proj, w_out, w_ffn_in,
           w_ffn_out):
    bsz, seq, d = x.shape
    n = bsz * seq
    assert d == D_MODEL and w_ada.shape[0] == 1 and seq % 1024 == 0 and bsz % ATT_MEMBERS == 0

    ga_block, gh_block = 0, 1
    q_block = 2 * D_MODEL // ATT_WIDTH
    hg_blocks = tuple((2 * D_MODEL + ATT_WIDTH) // HG_WIDTH + k for k in range(4))

    x2 = x.reshape(n, d)
    mod, w_in_bf16 = _adaln(c, w_ada[0], b_ada[0], w_in[0])
    mod3 = mod.reshape(bsz, N_MOD, d)

    proj, kv, w_up = _in_proj(x2, mod3, g_pre_mix, w_in_bf16, seq, ((w_ffn_in[0], (D_MODEL, LANES)),))

    inv_freq = ROPE_THETA ** (-jnp.arange(0, ROT_DIM, 2, dtype=F32) / ROT_DIM)
    freq8 = jnp.broadcast_to(inv_freq[:, None], (ROT_HALF, WINDOW))
    pos4 = positions.astype(F32).reshape(bsz, seq // WINDOW, 1, WINDOW)
    attn, wa, wh, wo = _attention(attn_sinks[0], freq8, pos4, proj.reshape(bsz, seq, -1),
                                  kv.reshape(bsz, seq, -1), q_block,
                                  (w_attn_proj[0], w_hgrn_proj[0], w_out[0]))
    attn = attn.reshape(n, ATT_WIDTH)

    hg = _hgrn(hg_lower_bounds, hg_norm, proj, bsz, seq, hg_blocks)

    x1 = _mix_out(x2, mod3, g_post_mix, attn, hg, proj, wa, wh, wo, seq, ga_block, gh_block)

    out = _ffn(x1, mod3, g_pre_ffn, g_post_ffn, w_up, w_ffn_out[0], seq)
    return out.reshape(bsz, seq, d)
```

```python
import functools
import math

import numpy as np
import jax
import jax.numpy as jnp
from jax import lax
from jax.experimental import pallas as pl
from jax.experimental.pallas import tpu as pltpu

F32 = jnp.float32
BF16 = jnp.bfloat16

D_MODEL = 2048
ATT_HEADS = 16
ATT_KV_HEADS = 2
HEAD_DIM = 64
ATT_WIDTH = ATT_HEADS * HEAD_DIM
KV_WIDTH = ATT_KV_HEADS * HEAD_DIM
WINDOW = 128
ROT_DIM = HEAD_DIM // 4
ROT_HALF = ROT_DIM // 2
ROPE_THETA = 500000.0
HG_HEADS = 8
HG_DIM = 128
HG_WIDTH = HG_HEADS * HG_DIM
N_MOD = 6
EPS = 1e-6
LOG2E = math.log2(math.e)
MASKED_LOGIT = -1e30

LANES = 128
SUBLANES = 8
V7X_VMEM_BYTES = 64 * 1024 * 1024

CHUNK = 64
DIAG = SUBLANES
LEVELS = (32, 16, 8)
ATT_MEMBERS = 8


def _vmem_limit(estimate_bytes):
    return int(min(estimate_bytes * 3 // 2, V7X_VMEM_BYTES - 8 * 1024 * 1024))


def _dot(a, b):
    return jnp.dot(a, b, preferred_element_type=F32)


def _dot_nt(a, b):
    return lax.dot_general(a, b, (((1,), (1,)), ((), ())), preferred_element_type=F32)


def _dot_tn(a, b):
    return lax.dot_general(a, b, (((0,), (0,)), ((), ())), preferred_element_type=F32)


def _rms(t):
    return t * lax.rsqrt(jnp.mean(t * t, axis=-1, keepdims=True) + EPS)


def _gated_norm_residual(x, y, gate, gain):
    return x + _rms(y) * (gate * gain)


def _sigmoid(t):
    return 0.5 + 0.5 * jnp.tanh(0.5 * t)


def _split_bf16(x, parts):
    out = []
    for _ in range(parts - 1):
        p = x.astype(BF16).astype(F32)
        out.append(p)
        x = x - p
    out.append(x.astype(BF16).astype(F32))
    return out


def _adaln_kernel(c_ref, w_ref, b_ref, win_ref, o_ref, win_bf16_ref):
    bsz = c_ref.shape[0]
    c_hi, c_lo = _split_bf16(c_ref[...], 2)
    w_hi, w_lo = _split_bf16(w_ref[...], 2)
    by_hi = _dot(jnp.concatenate([c_hi, c_lo], axis=0).astype(BF16), w_hi.astype(BF16))
    o_ref[...] = (by_hi[0:bsz] + by_hi[bsz:] + _dot(c_hi.astype(BF16), w_lo.astype(BF16))
                  + b_ref[...])
    win_bf16_ref[...] = win_ref[...].astype(BF16)


def _adaln(c, w, b, w_in):
    bsz, d = c.shape
    n = w.shape[1]
    steps = 16
    tn = n // steps
    rows = w_in.shape[0] // steps
    assert tn * steps == n and tn % LANES == 0 and rows * steps == w_in.shape[0] and rows % 16 == 0
    est = (2 * (d * tn * 4) + 2 * bsz * d * 4 + 4 * bsz * tn * 4 + 3 * d * tn * 4
           + 2 * rows * w_in.shape[1] * 6)
    return pl.pallas_call(
        _adaln_kernel,
        grid=(steps,),
        in_specs=[pl.BlockSpec((bsz, d), lambda j: (0, 0)),
                  pl.BlockSpec((d, tn), lambda j: (0, j)),
                  pl.BlockSpec((1, tn), lambda j: (0, j)),
                  pl.BlockSpec((rows, w_in.shape[1]), lambda j: (j, 0))],
        out_specs=[pl.BlockSpec((bsz, tn), lambda j: (0, j)),
                   pl.BlockSpec((rows, w_in.shape[1]), lambda j: (j, 0))],
        out_shape=[jax.ShapeDtypeStruct((bsz, n), F32),
                   jax.ShapeDtypeStruct(w_in.shape, BF16)],
        compiler_params=pltpu.CompilerParams(
            dimension_semantics=("arbitrary",), vmem_limit_bytes=_vmem_limit(est)),
        name="adaln_mod",
    )(c, w, b.reshape(1, n), w_in)


def _in_proj_kernel(x_ref, mod_ref, g_ref, w_ref, wkv_ref, *refs, slices, n_cast):
    cast_src, (o_ref, okv_ref) = refs[:n_cast], refs[n_cast:n_cast + 2]
    cast_dst, (h_even_ref, h_odd_ref) = refs[n_cast + 2:2 * n_cast + 2], refs[2 * n_cast + 2:]
    i, j = pl.program_id(0), pl.program_id(1)

    def modulated_norm(rows):
        y = _rms(x_ref[rows, :]) * g_ref[...]
        return (y * (1.0 + mod_ref[1:2, :]) + mod_ref[0:1, :]).astype(BF16)

    @pl.when((i == 0) & (j == 0))
    def _():
        h_even_ref[...] = modulated_norm(slice(None))

    def step(h_ref, h_next_ref):
        @pl.when(j == 0)
        def _():
            okv_ref[...] = _dot(h_ref[...], wkv_ref[...]).astype(BF16)

        rows_per = x_ref.shape[0] // slices
        first = pl.multiple_of(jnp.clip(j - 1, 0, slices - 1) * rows_per, rows_per)
        rows = pl.ds(first, rows_per)
        h_next_ref[rows, :] = modulated_norm(rows)
        for src_ref, dst_ref in zip(cast_src, cast_dst):
            dst_ref[...] = src_ref[...].astype(BF16)
        o_ref[...] = _dot(h_ref[...], w_ref[...]).astype(BF16)

    pl.when(i % 2 == 0)(lambda: step(h_even_ref, h_odd_ref))
    pl.when(i % 2 == 1)(lambda: step(h_odd_ref, h_even_ref))


def _in_proj(x2, mod3, gain, w_in, seq, later_weights):
    n, d = x2.shape
    tm, tn = 1024, 1024
    per_batch = seq // tm
    o_kv = ATT_WIDTH
    o_qh = o_kv + 2 * KV_WIDTH
    o_ga = o_qh + 4 * HG_WIDTH
    cols = w_in.shape[1] - 2 * KV_WIDTH
    n_gate = 2 * D_MODEL // tn
    n_q = ATT_WIDTH // tn

    def src_col(j):
        unit = tn // LANES
        lane_tile = jnp.where(j < n_gate, o_ga // LANES + unit * j,
                              jnp.where(j < n_gate + n_q, unit * (j - n_gate),
                                        o_qh // LANES + unit * (j - n_gate - n_q)))
        return lane_tile * LANES

    last = n // tm - 1
    x_tile = lambda i, j: jnp.minimum(i + jnp.where(j > 0, 1, 0), last)

    n_col = cols // tn
    cast_specs, cast_bytes = [], 0
    for w, (br, bc) in later_weights:
        rb, cb = w.shape[0] // br, w.shape[1] // bc
        assert rb * br == w.shape[0] and cb * bc == w.shape[1] and rb * cb <= (n // tm) * n_col
        def block(i, j, rb=rb, cb=cb):
            k = jnp.minimum(i * n_col + j, rb * cb - 1)
            return k // cb, k % cb
        cast_specs.append(pl.BlockSpec((br, bc), block))
        cast_bytes += 2 * br * bc * 6

    est = (2 * tm * d * 4 + 2 * tm * d * 2 + 2 * d * tn * 2 + 2 * tm * tn * 2
           + 2 * d * KV_WIDTH * 2 * 2 + 2 * tm * 2 * KV_WIDTH * 2 + tm * tn * 4 + tm * d * 4 // 8
           + cast_bytes)
    return pl.pallas_call(
        functools.partial(_in_proj_kernel, slices=8, n_cast=len(later_weights)),
        grid=(n // tm, cols // tn),
        in_specs=[pl.BlockSpec((tm, d), lambda i, j: (x_tile(i, j), 0)),
                  pl.BlockSpec((None, N_MOD, d), lambda i, j: (x_tile(i, j) // per_batch, 0, 0)),
                  pl.BlockSpec((1, d), lambda i, j: (0, 0)),
                  pl.BlockSpec((pl.Element(d), pl.Element(tn)), lambda i, j: (0, src_col(j))),
                  pl.BlockSpec((d, 2 * KV_WIDTH), lambda i, j: (0, o_kv // (2 * KV_WIDTH)))]
                 + cast_specs,
        out_specs=[pl.BlockSpec((tm, tn), lambda i, j: (i, j)),
                   pl.BlockSpec((tm, 2 * KV_WIDTH), lambda i, j: (i, 0))] + cast_specs,
        out_shape=[jax.ShapeDtypeStruct((n, cols), BF16),
                   jax.ShapeDtypeStruct((n, 2 * KV_WIDTH), BF16)]
                  + [jax.ShapeDtypeStruct(w.shape, BF16) for w, _ in later_weights],
        scratch_shapes=[pltpu.VMEM((tm, d), BF16), pltpu.VMEM((tm, d), BF16)],
        compiler_params=pltpu.CompilerParams(
            dimension_semantics=("arbitrary", "arbitrary"), vmem_limit_bytes=_vmem_limit(est)),
        name="in_proj",
    )(x2, mod3, gain, w_in, w_in, *[w for w, _ in later_weights])


def _attn_constants():
    lane = np.arange(LANES)
    dim = lane % HEAD_DIM
    expand = np.zeros((2, 3, ROT_HALF, 3, LANES), np.float32)
    for j in range(ROT_HALF):
        expand[0, :, j, 0, (dim < ROT_DIM) & (dim % ROT_HALF == j)] = 1.0
        expand[1, :, j, 1, (dim < ROT_HALF) & (dim == j)] = -1.0
        expand[1, :, j, 2, (dim >= ROT_HALF) & (dim < ROT_DIM) & (dim - ROT_HALF == j)] = 1.0
    expand = expand.reshape(6 * ROT_HALF, 3 * LANES)
    shift = np.zeros((LANES, 2 * LANES), np.float32)
    for l in range(LANES):
        if l + ROT_HALF < LANES:
            shift[l + ROT_HALF, l] = 1.0
        if l - ROT_HALF >= 0:
            shift[l - ROT_HALF, LANES + l] = 1.0
    return jnp.asarray(expand, BF16), jnp.asarray(shift, BF16)


def _attn_kernel(sink_ref, freq_ref, expand_ref, shift_ref, pos_ref, q_ref, kv_ref, *refs):
    n_cast = (len(refs) - 4) // 2
    o_ref, (qb_ref, kb_ref, vb_ref) = refs[n_cast], refs[2 * n_cast + 1:]
    members = q_ref.shape[0]
    nblk = pl.program_id(1)
    parity = nblk % 2
    blk = WINDOW
    lane = lax.broadcasted_iota(jnp.int32, (blk, LANES), 1)
    lo = lane < HEAD_DIM
    dim = lane % HEAD_DIM

    @pl.when(nblk == 0)
    def _():
        kb_ref[...] = jnp.zeros_like(kb_ref)
        vb_ref[...] = jnp.zeros_like(vb_ref)
        ones_lo = jnp.where(lo, 1.0, 0.0).astype(BF16)
        ones_hi = jnp.where(lo, 0.0, 1.0).astype(BF16)
        for g in range(members):
            for kvh in range(ATT_KV_HEADS):
                for quarter in range(4):
                    rows = slice(quarter * blk, (quarter + 1) * blk)
                    vb_ref[g, kvh, rows, LANES:] = ones_hi if quarter >= 2 else ones_lo

    def rope(x_bf16, tables):
        t_cos, t_lo, t_hi = tables
        ud = _dot(x_bf16, shift_ref[...])
        return x_bf16.astype(F32) * t_cos + ud[:, 0:LANES] * t_lo + ud[:, LANES:] * t_hi

    k_tabs, q_tabs = [], []
    for g in range(members):
        ang = freq_ref[...] * pos_ref[g]
        parts = _split_bf16(jnp.cos(ang), 3) + _split_bf16(jnp.sin(ang), 3)
        tabs = _dot_tn(jnp.concatenate(parts, axis=0).astype(BF16), expand_ref[...])
        kt = (tabs[:, 0:LANES] + jnp.where(dim < ROT_DIM, 0.0, 1.0),
              tabs[:, LANES:2 * LANES], tabs[:, 2 * LANES:3 * LANES])
        k_tabs.append(kt)
        q_tabs.append(tuple(t * (HEAD_DIM ** -0.5 * LOG2E) for t in kt))

    q_i = lax.broadcasted_iota(jnp.int32, (blk, 2 * blk), 0)
    k_j = lax.broadcasted_iota(jnp.int32, (blk, 2 * blk), 1)
    is_cur = (k_j >= blk) == (parity == 1)
    k_in = k_j % blk
    valid = (is_cur & (k_in <= q_i)) | ((~is_cur) & (nblk > 0) & (k_in > q_i))

    for g in range(members):
        k = rope(kv_ref[g, :, 0:KV_WIDTH], k_tabs[g])
        v = kv_ref[g, :, KV_WIDTH:].astype(F32)
        k_sw = pltpu.roll(k, HEAD_DIM, axis=1)
        v_sw = pltpu.roll(v, HEAD_DIM, axis=1)
        for kvh in range(ATT_KV_HEADS):
            for slot in range(2):
                ks = (k, k_sw)[kvh ^ slot]
                vs = (v, v_sw)[kvh ^ slot]
                keep = lo if slot == 0 else ~lo
                rows = pl.ds(pl.multiple_of((2 * slot + parity) * blk, blk), blk)
                kb_ref[g, kvh, rows, 0:LANES] = jnp.where(keep, ks, 0.0).astype(BF16)
                vb_ref[g, kvh, rows, 0:LANES] = jnp.where(keep, vs, 0.0).astype(BF16)

    pairs_per_kv = ATT_HEADS // ATT_KV_HEADS // 2
    cols = [slice(p * LANES, (p + 1) * LANES) for p in range(ATT_HEADS // 2)]
    groups = [(g, kvh) for g in range(members) for kvh in range(ATT_KV_HEADS)]
    pairs_of = lambda kvh: range(kvh * pairs_per_kv, (kvh + 1) * pairs_per_kv)
    for g in range(members):
        stacked = jnp.concatenate([q_ref[g, :, cs] for cs in cols], axis=0)
        tables = tuple(jnp.concatenate([t] * len(cols), axis=0) for t in q_tabs[g])
        qb_ref[g] = rope(stacked, tables).astype(BF16)
    rows_of = lambda kvh: slice(kvh * pairs_per_kv * blk, (kvh + 1) * pairs_per_kv * blk)
    s = [_dot_nt(qb_ref[g, rows_of(kvh), :], kb_ref[g, kvh]) for g, kvh in groups]
    probs, sink = [], []
    for w, (g, kvh) in enumerate(groups):
        pp, ss = [], []
        for i, p in enumerate(pairs_of(kvh)):
            pe, se = [], []
            for e in range(2):
                logits = jnp.where(valid, s[w][i * blk:(i + 1) * blk, e * 2 * blk:(e + 1) * 2 * blk],
                                   MASKED_LOGIT)
                sink2 = sink_ref[2 * p + e] * LOG2E
                m = jnp.maximum(jnp.max(logits, axis=-1, keepdims=True), sink2)
                pe.append(jnp.exp2(logits - m).astype(BF16))
                se.append(jnp.exp2(sink2 - m))
            pp.append(jnp.concatenate(pe, axis=1))
            ss.append(jnp.where(lo, se[0], se[1]))
        probs.append(jnp.concatenate(pp, axis=0))
        sink.append(ss)
    o = [_dot(probs[w], vb_ref[g, kvh]) for w, (g, kvh) in enumerate(groups)]
    for w, (g, kvh) in enumerate(groups):
        for i, p in enumerate(pairs_of(kvh)):
            op = o[w][i * blk:(i + 1) * blk]
            o_ref[g, :, cols[p]] = (op[:, 0:LANES] / (op[:, LANES:] + sink[w][i])).astype(BF16)

    for src_ref, dst_ref in zip(refs[:n_cast], refs[n_cast + 1:2 * n_cast + 1]):
        dst_ref[...] = src_ref[...].astype(BF16)


def _attention(sinks, freq8, pos4, proj3, kv3, q_col_block, later_weights):
    bsz, seq = proj3.shape[0], proj3.shape[1]
    blk = WINDOW
    members = ATT_MEMBERS
    steps = (bsz // members) * (seq // blk)
    cast_specs = []
    for w in later_weights:
        rows = w.shape[0] // steps
        assert rows * steps == w.shape[0] and rows % 16 == 0
        cast_specs.append(pl.BlockSpec((rows, w.shape[1]), lambda b, i: (b * (seq // blk) + i, 0)))
    expand, shift = _attn_constants()
    full = lambda a: pl.BlockSpec(a.shape, lambda b, i: (0,) * a.ndim)
    est = (2 * members * (blk * ATT_WIDTH * 2 * 2 + blk * 2 * KV_WIDTH * 2)
           + 2 * members * ATT_KV_HEADS * 4 * blk * 2 * LANES * 2 + members * ATT_WIDTH * blk * 2
           + members * 64 * blk * 4 * blk * 4)
    return pl.pallas_call(
        _attn_kernel,
        grid=(bsz // members, seq // blk),
        in_specs=[pl.BlockSpec(memory_space=pltpu.SMEM),
                  full(freq8), full(expand), full(shift),
                  pl.BlockSpec((members, None, 1, blk), lambda b, i: (b, i, 0, 0)),
                  pl.BlockSpec((members, blk, ATT_WIDTH), lambda b, i: (b, i, q_col_block)),
                  pl.BlockSpec((members, blk, 2 * KV_WIDTH), lambda b, i: (b, i, 0))] + cast_specs,
        out_specs=[pl.BlockSpec((members, blk, ATT_WIDTH), lambda b, i: (b, i, 0))] + cast_specs,
        out_shape=[jax.ShapeDtypeStruct((bsz, seq, ATT_WIDTH), BF16)]
                  + [jax.ShapeDtypeStruct(w.shape, BF16) for w in later_weights],
        scratch_shapes=[pltpu.VMEM((members, ATT_HEADS // 2 * blk, LANES), BF16),
                        pltpu.VMEM((members, ATT_KV_HEADS, 4 * blk, LANES), BF16),
                        pltpu.VMEM((members, ATT_KV_HEADS, 4 * blk, 2 * LANES), BF16)],
        compiler_params=pltpu.CompilerParams(
            dimension_semantics=("parallel", "arbitrary"), vmem_limit_bytes=_vmem_limit(est)),
        name="swa_attn",
    )(sinks, freq8, expand, shift, pos4, proj3, kv3, *later_weights)


def _hgrn_constants():
    t = np.arange(CHUNK)[:, None]
    j = np.arange(CHUNK)[None, :]
    tri = (j <= t).astype(np.float32)
    cumsum = np.concatenate([tri, tri], axis=1)
    reduce = np.zeros((DIAG, HG_DIM, LANES), np.float32)
    for d in range(DIAG):
        reduce[d, :, DIAG - 1 - d] = 1.0
    return jnp.asarray(cumsum, BF16), jnp.asarray(reduce.reshape(DIAG * HG_DIM, LANES), BF16)


def _hgrn_kernel(cm_ref, red_ref, lbp_ref, gn_ref, q_ref, f_ref, i_ref, g_ref, o_ref, st_ref, *,
                 tokens):
    @pl.when(pl.program_id(1) == 0)
    def _():
        st_ref[...] = jnp.zeros_like(st_ref)

    lbp = lbp_ref[...]
    e = jnp.exp(lbp - jnp.max(lbp, axis=0, keepdims=True))
    lb = e[0:1, :] / jnp.sum(e, axis=0, keepdims=True)
    half_key_scale = 0.5 * (1.0 - lb)

    row = lax.broadcasted_iota(jnp.int32, (CHUNK, CHUNK), 0)
    col = lax.broadcasted_iota(jnp.int32, (CHUNK, CHUNK), 1)
    same = {w: row // w == col // w for w in (DIAG, 2 * DIAG, 4 * DIAG)}
    groups = CHUNK // DIAG

    heads = [slice(h * HG_DIM, (h + 1) * HG_DIM) for h in range(HG_HEADS)]
    chunks = [slice(c * CHUNK, (c + 1) * CHUNK) for c in range(tokens // CHUNK)]
    per_chunk = []

    for rs in chunks:
        half_q = 0.5 * q_ref[rs, :].astype(F32)
        qs = half_q + half_q * jnp.tanh(half_q)
        key = half_key_scale - half_key_scale * jnp.tanh(0.5 * f_ref[rs, :].astype(F32))
        f = 1.0 - key
        val = i_ref[rs, :]

        b = _dot(cm_ref[...], jnp.concatenate(_split_bf16(jnp.log2(f), 2), axis=0).astype(BF16))
        eb = jnp.exp2(b)
        q_state = (qs * eb).astype(BF16)
        k_state = (key * jnp.exp2(b[CHUNK - 1:CHUNK, :] - b)).astype(BF16)
        decay = eb[CHUNK - 1:CHUNK, :]

        q_lv, k_lv = [], []
        for lv in LEVELS:
            qp, kp = [], []
            for piece in range(CHUNK // lv):
                ps = slice(piece * lv, (piece + 1) * lv)
                zero = jnp.zeros((lv, HG_WIDTH), F32)
                if piece % 2:
                    ref = b[piece * lv - 1:piece * lv, :]
                    qp.append(qs[ps] * jnp.exp2(b[ps] - ref)); kp.append(zero)
                else:
                    ref = b[(piece + 1) * lv - 1:(piece + 1) * lv, :]
                    qp.append(zero); kp.append(key[ps] * jnp.exp2(ref - b[ps]))
            q_lv.append(jnp.concatenate(qp, axis=0).astype(BF16))
            k_lv.append(jnp.concatenate(kp, axis=0).astype(BF16))

        carry = key.reshape(groups, DIAG, HG_WIDTH)
        f3 = f.reshape(groups, DIAG, HG_WIDTH)
        qs_bf16 = qs.astype(BF16)
        diag = [qs_bf16 * key.astype(BF16)]
        for d in range(1, DIAG):
            carry = f3 * pltpu.roll(carry, 1, axis=1)
            diag.append(qs_bf16 * carry.reshape(CHUNK, HG_WIDTH).astype(BF16))

        parts = [[_dot_nt(q_lv[n][:, hs], k_lv[n][:, hs]) for n in range(len(LEVELS))]
                 for hs in heads]
        r = _dot(jnp.concatenate([jnp.concatenate([dg[:, hs] for dg in diag], axis=1)
                                  for hs in heads], axis=0), red_ref[...])
        update = [_dot_tn(val[:, hs], k_state[:, hs]) for hs in heads]
        per_chunk.append((parts, r, update, q_state, val, decay))

    g_all = []
    for parts, r, _, _, _, _ in per_chunk:
        g_chunk = []
        for h in range(HG_HEADS):
            g_diag = pltpu.roll(r[h * CHUNK:(h + 1) * CHUNK], LANES - (DIAG - 1), axis=1,
                                stride=1, stride_axis=0)[:, 0:CHUNK]
            g_mat = jnp.where(same[DIAG], g_diag,
                              jnp.where(same[2 * DIAG], parts[h][2],
                                        jnp.where(same[4 * DIAG], parts[h][1], parts[h][0])))
            g_chunk.append(g_mat.astype(BF16))
        g_all.append(g_chunk)

    state = [st_ref[h] for h in range(HG_HEADS)]
    for rs, g_chunk, (_, _, update, q_state, val, decay) in zip(chunks, g_all, per_chunk):
        for h, hs in enumerate(heads):
            o = _dot_nt(q_state[:, hs], state[h].astype(BF16)) + _dot(g_chunk[h], val[:, hs])
            state[h] = state[h] * decay[:, hs] + update[h]
            y = _rms(o) * gn_ref[...]
            o_ref[rs, hs] = (y * _sigmoid(g_ref[rs, hs].astype(F32))).astype(BF16)
    for h in range(HG_HEADS):
        st_ref[h] = state[h]


def _hgrn(lb_params, gnorm, proj, bsz, seq, col_blocks):
    n = proj.shape[0]
    tokens = 8 * CHUNK
    nt = seq // tokens
    cmat, rmat = _hgrn_constants()
    row = lambda b, i: b * nt + i
    spec = lambda cb: pl.BlockSpec((tokens, HG_WIDTH), lambda b, i: (row(b, i), cb))
    full = lambda a: pl.BlockSpec(a.shape, lambda b, i: (0,) * a.ndim)
    est = 2 * 5 * tokens * HG_WIDTH * 2 + HG_HEADS * HG_DIM * HG_DIM * 4 + 40 * CHUNK * HG_WIDTH * 4
    return pl.pallas_call(
        functools.partial(_hgrn_kernel, tokens=tokens),
        grid=(bsz, nt),
        in_specs=[full(cmat), full(rmat), full(lb_params), full(gnorm),
                  spec(col_blocks[0]), spec(col_blocks[1]), spec(col_blocks[2]), spec(col_blocks[3])],
        out_specs=pl.BlockSpec((tokens, HG_WIDTH), lambda b, i: (row(b, i), 0)),
        out_shape=jax.ShapeDtypeStruct((n, HG_WIDTH), BF16),
        scratch_shapes=[pltpu.VMEM((HG_HEADS, HG_DIM, HG_DIM), F32)],
        compiler_params=pltpu.CompilerParams(
            dimension_semantics=("parallel", "arbitrary"), vmem_limit_bytes=_vmem_limit(est)),
        name="hgrn2",
    )(cmat, rmat, lb_params, gnorm, proj, proj, proj, proj)


def _mix_kernel(x_ref, mod_ref, g_ref, a_ref, hg_ref, ga_ref, gh_ref, wa_ref, wh_ref, wo_ref,
                o_ref, m_ref, *, tc):
    d = x_ref.shape[1]
    for c in range(d // tc):
        cs = slice(c * tc, (c + 1) * tc)
        ya = _dot(a_ref[...], wa_ref[:, cs])
        yh = _dot(hg_ref[...], wh_ref[:, cs])
        merged = (_sigmoid(ga_ref[:, cs].astype(F32)) * ya
                  + _sigmoid(gh_ref[:, cs].astype(F32)) * yh)
        m_ref[:, cs] = merged.astype(BF16)
    y = _dot(m_ref[...], wo_ref[...])
    o_ref[...] = _gated_norm_residual(x_ref[...], y, mod_ref[2:3, :], g_ref[...])


def _mix_out(x2, mod3, gain, attn, hg, proj, wa, wh, wo, seq, ga_block, gh_block):
    n, d = x2.shape
    tm, tc = 512, 512
    per_batch = seq // tm
    const = lambda shape: pl.BlockSpec(shape, lambda i: (0, 0), pipeline_mode=pl.Buffered(1))
    est = ((wa.size + wh.size + wo.size) * 2 + 2 * 2 * tm * d * 4 + 2 * 2 * tm * d * 2
           + 2 * 2 * tm * ATT_WIDTH * 2 + tm * d * 2 + 3 * tm * d * 4)
    return pl.pallas_call(
        functools.partial(_mix_kernel, tc=tc),
        grid=(n // tm,),
        in_specs=[pl.BlockSpec((tm, d), lambda i: (i, 0)),
                  pl.BlockSpec((None, N_MOD, d), lambda i: (i // per_batch, 0, 0)),
                  pl.BlockSpec((1, d), lambda i: (0, 0)),
                  pl.BlockSpec((tm, ATT_WIDTH), lambda i: (i, 0)),
                  pl.BlockSpec((tm, HG_WIDTH), lambda i: (i, 0)),
                  pl.BlockSpec((tm, d), lambda i: (i, ga_block)),
                  pl.BlockSpec((tm, d), lambda i: (i, gh_block)),
                  const(wa.shape), const(wh.shape), const(wo.shape)],
        out_specs=pl.BlockSpec((tm, d), lambda i: (i, 0)),
        out_shape=jax.ShapeDtypeStruct((n, d), F32),
        scratch_shapes=[pltpu.VMEM((tm, d), BF16)],
        compiler_params=pltpu.CompilerParams(
            dimension_semantics=("parallel",), vmem_limit_bytes=_vmem_limit(est)),
        name="mix_out",
    )(x2, mod3, gain, attn, hg, proj, proj, wa, wh, wo)


def _ffn_up_kernel(x_ref, mod_ref, gpre_ref, wg_ref, wu_ref, wd_ref, o_ref, wd_bf16_ref,
                   h_even_ref, h_odd_ref, *, slices):
    i, j = pl.program_id(0), pl.program_id(1)

    def modulated_norm(rows):
        y = _rms(x_ref[rows, :]) * gpre_ref[...]
        return (y * (1.0 + mod_ref[4:5, :]) + mod_ref[3:4, :]).astype(BF16)

    @pl.when((i == 0) & (j == 0))
    def _():
        h_even_ref[...] = modulated_norm(slice(None))

    def step(h_ref, h_next_ref):
        rows_per = x_ref.shape[0] // slices
        first = pl.multiple_of(jnp.clip(j - 1, 0, slices - 1) * rows_per, rows_per)
        rows = pl.ds(first, rows_per)
        h_next_ref[rows, :] = modulated_norm(rows)
        wd_bf16_ref[...] = wd_ref[...].astype(BF16)

        h = h_ref[...]
        g = _dot(h, wg_ref[...])
        u = _dot(h, wu_ref[...])
        o_ref[...] = (g * _sigmoid(g) * u).astype(BF16)

    pl.when(i % 2 == 0)(lambda: step(h_even_ref, h_odd_ref))
    pl.when(i % 2 == 1)(lambda: step(h_odd_ref, h_even_ref))


def _ffn_down_kernel(a_ref, x_ref, mod_ref, gpost_ref, wd_ref, o_ref):
    y = _dot(a_ref[...], wd_ref[...])
    o_ref[...] = _gated_norm_residual(x_ref[...], y, mod_ref[5:6, :], gpost_ref[...])


def _ffn(x1, mod3, gpre, gpost, w_in, w_down, seq):
    n, d = x1.shape
    hidden = w_down.shape[0]

    tm, th = 1024, 512
    nh = hidden // th
    per_batch = seq // tm
    last = n // tm - 1
    x_tile = lambda i, j: jnp.minimum(i + jnp.where(j > 0, 1, 0), last)
    steps = (n // tm) * nh
    wd_rows = hidden // steps
    assert wd_rows * steps == hidden and wd_rows % 16 == 0
    est = (2 * tm * d * 4 + 2 * tm * d * 2 + 2 * 2 * d * th * 2 + 2 * tm * th * 2
           + 4 * tm * th * 4 + 2 * wd_rows * d * 6)
    act, w_down_bf16 = pl.pallas_call(
        functools.partial(_ffn_up_kernel, slices=8),
        grid=(n // tm, nh),
        in_specs=[pl.BlockSpec((tm, d), lambda i, j: (x_tile(i, j), 0)),
                  pl.BlockSpec((None, N_MOD, d), lambda i, j: (x_tile(i, j) // per_batch, 0, 0)),
                  pl.BlockSpec((1, d), lambda i, j: (0, 0)),
                  pl.BlockSpec((d, th), lambda i, j: (0, j)),
                  pl.BlockSpec((d, th), lambda i, j: (0, nh + j)),
                  pl.BlockSpec((wd_rows, d), lambda i, j: (i * nh + j, 0))],
        out_specs=[pl.BlockSpec((tm, th), lambda i, j: (i, j)),
                   pl.BlockSpec((wd_rows, d), lambda i, j: (i * nh + j, 0))],
        out_shape=[jax.ShapeDtypeStruct((n, hidden), BF16),
                   jax.ShapeDtypeStruct((hidden, d), BF16)],
        scratch_shapes=[pltpu.VMEM((tm, d), BF16), pltpu.VMEM((tm, d), BF16)],
        compiler_params=pltpu.CompilerParams(
            dimension_semantics=("arbitrary", "arbitrary"), vmem_limit_bytes=_vmem_limit(est)),
        name="ffn_up",
    )(x1, mod3, gpre, w_in, w_in, w_down)
    w_down = w_down_bf16

    tm = 512
    per_batch = seq // tm
    est = hidden * d * 2 + 2 * tm * hidden * 2 + 2 * 2 * tm * d * 4 + 2 * tm * d * 4
    return pl.pallas_call(
        _ffn_down_kernel,
        grid=(n // tm,),
        in_specs=[pl.BlockSpec((tm, hidden), lambda i: (i, 0)),
                  pl.BlockSpec((tm, d), lambda i: (i, 0)),
                  pl.BlockSpec((None, N_MOD, d), lambda i: (i // per_batch, 0, 0)),
                  pl.BlockSpec((1, d), lambda i: (0, 0)),
                  pl.BlockSpec((hidden, d), lambda i: (0, 0), pipeline_mode=pl.Buffered(1))],
        out_specs=pl.BlockSpec((tm, d), lambda i: (i, 0)),
        out_shape=jax.ShapeDtypeStruct((n, d), F32),
        compiler_params=pltpu.CompilerParams(
            dimension_semantics=("parallel",), vmem_limit_bytes=_vmem_limit(est)),
        name="ffn_down",
    )(act, x1, mod3, gpost, w_down)


def kernel(x, c, positions, w_ada, b_ada, g_pre_mix, g_post_mix, g_pre_ffn, g_post_ffn, w_in,
           attn_sinks, w_attn_proj, hg_lower_bounds, hg_norm, w_hgrn_proj, w_out, w_ffn_in,
           w_ffn_out):
    bsz, seq, d = x.shape
    n = bsz * seq
    assert d == D_MODEL and w_ada.shape[0] == 1 and seq % 1024 == 0 and bsz % ATT_MEMBERS == 0

    ga_block, gh_block = 0, 1
    q_block = 2 * D_MODEL // ATT_WIDTH
    hg_blocks = tuple((2 * D_MODEL + ATT_WIDTH) // HG_WIDTH + k for k in range(4))

    x2 = x.reshape(n, d)
    mod, w_in_bf16 = _adaln(c, w_ada[0], b_ada[0], w_in[0])
    mod3 = mod.reshape(bsz, N_MOD, d)

    proj, kv, w_up = _in_proj(x2, mod3, g_pre_mix, w_in_bf16, seq, ((w_ffn_in[0], (D_MODEL, LANES)),))

    inv_freq = ROPE_THETA ** (-jnp.arange(0, ROT_DIM, 2, dtype=F32) / ROT_DIM)
    freq8 = jnp.broadcast_to(inv_freq[:, None], (ROT_HALF, WINDOW))
    pos4 = positions.astype(F32).reshape(bsz, seq // WINDOW, 1, WINDOW)
    attn, wa, wh, wo = _attention(attn_sinks[0], freq8, pos4, proj.reshape(bsz, seq, -1),
                                  kv.reshape(bsz, seq, -1), q_block,
                                  (w_attn_proj[0], w_hgrn_proj[0], w_out[0]))
    attn = attn.reshape(n, ATT_WIDTH)

    hg = _hgrn(hg_lower_bounds, hg_norm, proj, bsz, seq, hg_blocks)

    x1 = _mix_out(x2, mod3, g_post_mix, attn, hg, proj, wa, wh, wo, seq, ga_block, gh_block)

    out = _ffn(x1, mod3, g_pre_ffn, g_post_ffn, w_up, w_ffn_out[0], seq)
    return out.reshape(bsz, seq, d)
```

```python
import functools
import math

import numpy as np
import jax
import jax.numpy as jnp
from jax import lax
from jax.experimental import pallas as pl
from jax.experimental.pallas import tpu as pltpu

F32 = jnp.float32
BF16 = jnp.bfloat16

D_MODEL = 2048
ATT_HEADS = 16
ATT_KV_HEADS = 2
HEAD_DIM = 64
ATT_WIDTH = ATT_HEADS * HEAD_DIM
KV_WIDTH = ATT_KV_HEADS * HEAD_DIM
WINDOW = 128
ROT_DIM = HEAD_DIM // 4
ROT_HALF = ROT_DIM // 2
ROPE_THETA = 500000.0
HG_HEADS = 8
HG_DIM = 128
HG_WIDTH = HG_HEADS * HG_DIM
N_MOD = 6
EPS = 1e-6
LOG2E = math.log2(math.e)
MASKED_LOGIT = -1e30

LANES = 128
SUBLANES = 8
V7X_VMEM_BYTES = 64 * 1024 * 1024

CHUNK = 64
DIAG = SUBLANES
LEVELS = (32, 16, 8)
ATT_MEMBERS = 8
HG_MEMBERS = 2


def _vmem_limit(estimate_bytes):
    return int(min(estimate_bytes * 3 // 2, V7X_VMEM_BYTES - 8 * 1024 * 1024))


def _dot(a, b):
    return jnp.dot(a, b, preferred_element_type=F32)


def _dot_nt(a, b):
    return lax.dot_general(a, b, (((1,), (1,)), ((), ())), preferred_element_type=F32)


def _dot_tn(a, b):
    return lax.dot_general(a, b, (((0,), (0,)), ((), ())), preferred_element_type=F32)


def _rms(t):
    return t * lax.rsqrt(jnp.mean(t * t, axis=-1, keepdims=True) + EPS)


def _gated_norm_residual(x, y, gate, gain):
    return x + _rms(y) * (gate * gain)


def _sigmoid(t):
    return 0.5 + 0.5 * jnp.tanh(0.5 * t)


def _split_bf16(x, parts):
    out = []
    for _ in range(parts - 1):
        p = x.astype(BF16).astype(F32)
        out.append(p)
        x = x - p
    out.append(x.astype(BF16).astype(F32))
    return out


def _adaln_kernel(c_ref, w_ref, b_ref, win_ref, o_ref, win_bf16_ref):
    bsz = c_ref.shape[0]
    c_hi, c_lo = _split_bf16(c_ref[...], 2)
    w_hi, w_lo = _split_bf16(w_ref[...], 2)
    by_hi = _dot(jnp.concatenate([c_hi, c_lo], axis=0).astype(BF16), w_hi.astype(BF16))
    o_ref[...] = (by_hi[0:bsz] + by_hi[bsz:] + _dot(c_hi.astype(BF16), w_lo.astype(BF16))
                  + b_ref[...])
    win_bf16_ref[...] = win_ref[...].astype(BF16)


def _adaln(c, w, b, w_in):
    bsz, d = c.shape
    n = w.shape[1]
    steps = 16
    tn = n // steps
    rows = w_in.shape[0] // steps
    assert tn * steps == n and tn % LANES == 0 and rows * steps == w_in.shape[0] and rows % 16 == 0
    est = (2 * (d * tn * 4) + 2 * bsz * d * 4 + 4 * bsz * tn * 4 + 3 * d * tn * 4
           + 2 * rows * w_in.shape[1] * 6)
    return pl.pallas_call(
        _adaln_kernel,
        grid=(steps,),
        in_specs=[pl.BlockSpec((bsz, d), lambda j: (0, 0)),
                  pl.BlockSpec((d, tn), lambda j: (0, j)),
                  pl.BlockSpec((1, tn), lambda j: (0, j)),
                  pl.BlockSpec((rows, w_in.shape[1]), lambda j: (j, 0))],
        out_specs=[pl.BlockSpec((bsz, tn), lambda j: (0, j)),
                   pl.BlockSpec((rows, w_in.shape[1]), lambda j: (j, 0))],
        out_shape=[jax.ShapeDtypeStruct((bsz, n), F32),
                   jax.ShapeDtypeStruct(w_in.shape, BF16)],
        compiler_params=pltpu.CompilerParams(
            dimension_semantics=("arbitrary",), vmem_limit_bytes=_vmem_limit(est)),
        name="adaln_mod",
    )(c, w, b.reshape(1, n), w_in)


def _in_proj_kernel(x_ref, mod_ref, g_ref, w_ref, wkv_ref, *refs, slices, n_cast):
    cast_src, (o_ref, okv_ref) = refs[:n_cast], refs[n_cast:n_cast + 2]
    cast_dst, (h_even_ref, h_odd_ref) = refs[n_cast + 2:2 * n_cast + 2], refs[2 * n_cast + 2:]
    i, j = pl.program_id(0), pl.program_id(1)

    def modulated_norm(rows):
        y = _rms(x_ref[rows, :]) * g_ref[...]
        return (y * (1.0 + mod_ref[1:2, :]) + mod_ref[0:1, :]).astype(BF16)

    @pl.when((i == 0) & (j == 0))
    def _():
        h_even_ref[...] = modulated_norm(slice(None))

    def step(h_ref, h_next_ref):
        @pl.when(j == 0)
        def _():
            okv_ref[...] = _dot(h_ref[...], wkv_ref[...]).astype(BF16)

        rows_per = x_ref.shape[0] // slices
        first = pl.multiple_of(jnp.clip(j - 1, 0, slices - 1) * rows_per, rows_per)
        rows = pl.ds(first, rows_per)
        h_next_ref[rows, :] = modulated_norm(rows)
        for src_ref, dst_ref in zip(cast_src, cast_dst):
            dst_ref[...] = src_ref[...].astype(BF16)
        o_ref[...] = _dot(h_ref[...], w_ref[...]).astype(BF16)

    pl.when(i % 2 == 0)(lambda: step(h_even_ref, h_odd_ref))
    pl.when(i % 2 == 1)(lambda: step(h_odd_ref, h_even_ref))


def _in_proj(x2, mod3, gain, w_in, seq, later_weights):
    n, d = x2.shape
    tm, tn = 1024, 1024
    per_batch = seq // tm
    o_kv = ATT_WIDTH
    o_qh = o_kv + 2 * KV_WIDTH
    o_ga = o_qh + 4 * HG_WIDTH
    cols = w_in.shape[1] - 2 * KV_WIDTH
    n_gate = 2 * D_MODEL // tn
    n_q = ATT_WIDTH // tn

    def src_col(j):
        unit = tn // LANES
        lane_tile = jnp.where(j < n_gate, o_ga // LANES + unit * j,
                              jnp.where(j < n_gate + n_q, unit * (j - n_gate),
                                        o_qh // LANES + unit * (j - n_gate - n_q)))
        return lane_tile * LANES

    last = n // tm - 1
    x_tile = lambda i, j: jnp.minimum(i + jnp.where(j > 0, 1, 0), last)

    n_col = cols // tn
    cast_specs, cast_bytes = [], 0
    for w, (br, bc) in later_weights:
        rb, cb = w.shape[0] // br, w.shape[1] // bc
        assert rb * br == w.shape[0] and cb * bc == w.shape[1] and rb * cb <= (n // tm) * n_col
        def block(i, j, rb=rb, cb=cb):
            k = jnp.minimum(i * n_col + j, rb * cb - 1)
            return k // cb, k % cb
        cast_specs.append(pl.BlockSpec((br, bc), block))
        cast_bytes += 2 * br * bc * 6

    est = (2 * tm * d * 4 + 2 * tm * d * 2 + 2 * d * tn * 2 + 2 * tm * tn * 2
           + 2 * d * KV_WIDTH * 2 * 2 + 2 * tm * 2 * KV_WIDTH * 2 + tm * tn * 4 + tm * d * 4 // 8
           + cast_bytes)
    return pl.pallas_call(
        functools.partial(_in_proj_kernel, slices=8, n_cast=len(later_weights)),
        grid=(n // tm, cols // tn),
        in_specs=[pl.BlockSpec((tm, d), lambda i, j: (x_tile(i, j), 0)),
                  pl.BlockSpec((None, N_MOD, d), lambda i, j: (x_tile(i, j) // per_batch, 0, 0)),
                  pl.BlockSpec((1, d), lambda i, j: (0, 0)),
                  pl.BlockSpec((pl.Element(d), pl.Element(tn)), lambda i, j: (0, src_col(j))),
                  pl.BlockSpec((d, 2 * KV_WIDTH), lambda i, j: (0, o_kv // (2 * KV_WIDTH)))]
                 + cast_specs,
        out_specs=[pl.BlockSpec((tm, tn), lambda i, j: (i, j)),
                   pl.BlockSpec((tm, 2 * KV_WIDTH), lambda i, j: (i, 0))] + cast_specs,
        out_shape=[jax.ShapeDtypeStruct((n, cols), BF16),
                   jax.ShapeDtypeStruct((n, 2 * KV_WIDTH), BF16)]
                  + [jax.ShapeDtypeStruct(w.shape, BF16) for w, _ in later_weights],
        scratch_shapes=[pltpu.VMEM((tm, d), BF16), pltpu.VMEM((tm, d), BF16)],
        compiler_params=pltpu.CompilerParams(
            dimension_semantics=("arbitrary", "arbitrary"), vmem_limit_bytes=_vmem_limit(est)),
        name="in_proj",
    )(x2, mod3, gain, w_in, w_in, *[w for w, _ in later_weights])


def _attn_constants():
    lane = np.arange(LANES)
    dim = lane % HEAD_DIM
    expand = np.zeros((2, 3, ROT_HALF, 3, LANES), np.float32)
    for j in range(ROT_HALF):
        expand[0, :, j, 0, (dim < ROT_DIM) & (dim % ROT_HALF == j)] = 1.0
        expand[1, :, j, 1, (dim < ROT_HALF) & (dim == j)] = -1.0
        expand[1, :, j, 2, (dim >= ROT_HALF) & (dim < ROT_DIM) & (dim - ROT_HALF == j)] = 1.0
    expand = expand.reshape(6 * ROT_HALF, 3 * LANES)
    shift = np.zeros((LANES, 2 * LANES), np.float32)
    for l in range(LANES):
        if l + ROT_HALF < LANES:
            shift[l + ROT_HALF, l] = 1.0
        if l - ROT_HALF >= 0:
            shift[l - ROT_HALF, LANES + l] = 1.0
    return jnp.asarray(expand, BF16), jnp.asarray(shift, BF16)


def _attn_kernel(sink_ref, freq_ref, expand_ref, shift_ref, pos_ref, q_ref, kv_ref, *refs):
    n_cast = (len(refs) - 4) // 2
    o_ref, (qb_ref, kb_ref, vb_ref) = refs[n_cast], refs[2 * n_cast + 1:]
    members = q_ref.shape[0]
    nblk = pl.program_id(1)
    parity = nblk % 2
    blk = WINDOW
    lane = lax.broadcasted_iota(jnp.int32, (blk, LANES), 1)
    lo = lane < HEAD_DIM
    dim = lane % HEAD_DIM

    @pl.when(nblk == 0)
    def _():
        kb_ref[...] = jnp.zeros_like(kb_ref)
        vb_ref[...] = jnp.zeros_like(vb_ref)
        ones_lo = jnp.where(lo, 1.0, 0.0).astype(BF16)
        ones_hi = jnp.where(lo, 0.0, 1.0).astype(BF16)
        for g in range(members):
            for kvh in range(ATT_KV_HEADS):
                for quarter in range(4):
                    rows = slice(quarter * blk, (quarter + 1) * blk)
                    vb_ref[g, kvh, rows, LANES:] = ones_hi if quarter >= 2 else ones_lo

    def rope(x_bf16, tables):
        t_cos, t_lo, t_hi = tables
        ud = _dot(x_bf16, shift_ref[...])
        return x_bf16.astype(F32) * t_cos + ud[:, 0:LANES] * t_lo + ud[:, LANES:] * t_hi

    k_tabs, q_tabs = [], []
    for g in range(members):
        ang = freq_ref[...] * pos_ref[g]
        parts = _split_bf16(jnp.cos(ang), 3) + _split_bf16(jnp.sin(ang), 3)
        tabs = _dot_tn(jnp.concatenate(parts, axis=0).astype(BF16), expand_ref[...])
        kt = (tabs[:, 0:LANES] + jnp.where(dim < ROT_DIM, 0.0, 1.0),
              tabs[:, LANES:2 * LANES], tabs[:, 2 * LANES:3 * LANES])
        k_tabs.append(kt)
        q_tabs.append(tuple(t * (HEAD_DIM ** -0.5 * LOG2E) for t in kt))

    q_i = lax.broadcasted_iota(jnp.int32, (blk, 2 * blk), 0)
    k_j = lax.broadcasted_iota(jnp.int32, (blk, 2 * blk), 1)
    is_cur = (k_j >= blk) == (parity == 1)
    k_in = k_j % blk
    valid = (is_cur & (k_in <= q_i)) | ((~is_cur) & (nblk > 0) & (k_in > q_i))

    for g in range(members):
        k = rope(kv_ref[g, :, 0:KV_WIDTH], k_tabs[g])
        v = kv_ref[g, :, KV_WIDTH:].astype(F32)
        k_sw = pltpu.roll(k, HEAD_DIM, axis=1)
        v_sw = pltpu.roll(v, HEAD_DIM, axis=1)
        for kvh in range(ATT_KV_HEADS):
            for slot in range(2):
                ks = (k, k_sw)[kvh ^ slot]
                vs = (v, v_sw)[kvh ^ slot]
                keep = lo if slot == 0 else ~lo
                rows = pl.ds(pl.multiple_of((2 * slot + parity) * blk, blk), blk)
                kb_ref[g, kvh, rows, 0:LANES] = jnp.where(keep, ks, 0.0).astype(BF16)
                vb_ref[g, kvh, rows, 0:LANES] = jnp.where(keep, vs, 0.0).astype(BF16)

    pairs_per_kv = ATT_HEADS // ATT_KV_HEADS // 2
    cols = [slice(p * LANES, (p + 1) * LANES) for p in range(ATT_HEADS // 2)]
    groups = [(g, kvh) for g in range(members) for kvh in range(ATT_KV_HEADS)]
    pairs_of = lambda kvh: range(kvh * pairs_per_kv, (kvh + 1) * pairs_per_kv)
    for g in range(members):
        stacked = jnp.concatenate([q_ref[g, :, cs] for cs in cols], axis=0)
        tables = tuple(jnp.concatenate([t] * len(cols), axis=0) for t in q_tabs[g])
        qb_ref[g] = rope(stacked, tables).astype(BF16)
    rows_of = lambda kvh: slice(kvh * pairs_per_kv * blk, (kvh + 1) * pairs_per_kv * blk)
    s = [_dot_nt(qb_ref[g, rows_of(kvh), :], kb_ref[g, kvh]) for g, kvh in groups]
    probs, sink = [], []
    for w, (g, kvh) in enumerate(groups):
        pp, ss = [], []
        for i, p in enumerate(pairs_of(kvh)):
            pe, se = [], []
            for e in range(2):
                logits = jnp.where(valid, s[w][i * blk:(i + 1) * blk, e * 2 * blk:(e + 1) * 2 * blk],
                                   MASKED_LOGIT)
                sink2 = sink_ref[2 * p + e] * LOG2E
                m = jnp.maximum(jnp.max(logits, axis=-1, keepdims=True), sink2)
                pe.append(jnp.exp2(logits - m).astype(BF16))
                se.append(jnp.exp2(sink2 - m))
            pp.append(jnp.concatenate(pe, axis=1))
            ss.append(jnp.where(lo, se[0], se[1]))
        probs.append(jnp.concatenate(pp, axis=0))
        sink.append(ss)
    o = [_dot(probs[w], vb_ref[g, kvh]) for w, (g, kvh) in enumerate(groups)]
    for w, (g, kvh) in enumerate(groups):
        for i, p in enumerate(pairs_of(kvh)):
            op = o[w][i * blk:(i + 1) * blk]
            o_ref[g, :, cols[p]] = (op[:, 0:LANES] / (op[:, LANES:] + sink[w][i])).astype(BF16)

    for src_ref, dst_ref in zip(refs[:n_cast], refs[n_cast + 1:2 * n_cast + 1]):
        dst_ref[...] = src_ref[...].astype(BF16)


def _attention(sinks, freq8, pos4, proj3, kv3, q_col_block, later_weights):
    bsz, seq = proj3.shape[0], proj3.shape[1]
    blk = WINDOW
    members = ATT_MEMBERS
    steps = (bsz // members) * (seq // blk)
    cast_specs = []
    for w in later_weights:
        rows = w.shape[0] // steps
        assert rows * steps == w.shape[0] and rows % 16 == 0
        cast_specs.append(pl.BlockSpec((rows, w.shape[1]), lambda b, i: (b * (seq // blk) + i, 0)))
    expand, shift = _attn_constants()
    full = lambda a: pl.BlockSpec(a.shape, lambda b, i: (0,) * a.ndim)
    est = (2 * members * (blk * ATT_WIDTH * 2 * 2 + blk * 2 * KV_WIDTH * 2)
           + 2 * members * ATT_KV_HEADS * 4 * blk * 2 * LANES * 2 + members * ATT_WIDTH * blk * 2
           + members * 64 * blk * 4 * blk * 4)
    return pl.pallas_call(
        _attn_kernel,
        grid=(bsz // members, seq // blk),
        in_specs=[pl.BlockSpec(memory_space=pltpu.SMEM),
                  full(freq8), full(expand), full(shift),
                  pl.BlockSpec((members, None, 1, blk), lambda b, i: (b, i, 0, 0)),
                  pl.BlockSpec((members, blk, ATT_WIDTH), lambda b, i: (b, i, q_col_block)),
                  pl.BlockSpec((members, blk, 2 * KV_WIDTH), lambda b, i: (b, i, 0))] + cast_specs,
        out_specs=[pl.BlockSpec((members, blk, ATT_WIDTH), lambda b, i: (b, i, 0))] + cast_specs,
        out_shape=[jax.ShapeDtypeStruct((bsz, seq, ATT_WIDTH), BF16)]
                  + [jax.ShapeDtypeStruct(w.shape, BF16) for w in later_weights],
        scratch_shapes=[pltpu.VMEM((members, ATT_HEADS // 2 * blk, LANES), BF16),
                        pltpu.VMEM((members, ATT_KV_HEADS, 4 * blk, LANES), BF16),
                        pltpu.VMEM((members, ATT_KV_HEADS, 4 * blk, 2 * LANES), BF16)],
        compiler_params=pltpu.CompilerParams(
            dimension_semantics=("parallel", "arbitrary"), vmem_limit_bytes=_vmem_limit(est)),
        name="swa_attn",
    )(sinks, freq8, expand, shift, pos4, proj3, kv3, *later_weights)


def _hgrn_constants():
    t = np.arange(CHUNK)[:, None]
    j = np.arange(CHUNK)[None, :]
    tri = (j <= t).astype(np.float32)
    cumsum = np.concatenate([tri, tri], axis=1)
    reduce = np.zeros((DIAG, HG_DIM, LANES), np.float32)
    for d in range(DIAG):
        reduce[d, :, DIAG - 1 - d] = 1.0
    return jnp.asarray(cumsum, BF16), jnp.asarray(reduce.reshape(DIAG * HG_DIM, LANES), BF16)


def _hgrn_kernel(cm_ref, red_ref, lbp_ref, gn_ref, q_ref, f_ref, i_ref, g_ref, o_ref, st_ref):
    members, tokens = q_ref.shape[0], q_ref.shape[1]

    @pl.when(pl.program_id(1) == 0)
    def _():
        st_ref[...] = jnp.zeros_like(st_ref)

    lbp = lbp_ref[...]
    e = jnp.exp(lbp - jnp.max(lbp, axis=0, keepdims=True))
    lb = e[0:1, :] / jnp.sum(e, axis=0, keepdims=True)
    half_key_scale = 0.5 * (1.0 - lb)

    row = lax.broadcasted_iota(jnp.int32, (CHUNK, CHUNK), 0)
    col = lax.broadcasted_iota(jnp.int32, (CHUNK, CHUNK), 1)
    same = {w: row // w == col // w for w in (DIAG, 2 * DIAG, 4 * DIAG)}
    groups = CHUNK // DIAG

    heads = [slice(h * HG_DIM, (h + 1) * HG_DIM) for h in range(HG_HEADS)]
    chunks = [(m, slice(c * CHUNK, (c + 1) * CHUNK))
              for m in range(members) for c in range(tokens // CHUNK)]
    per_chunk = []

    for m, rs in chunks:
        half_q = 0.5 * q_ref[m, rs, :].astype(F32)
        qs = half_q + half_q * jnp.tanh(half_q)
        key = half_key_scale - half_key_scale * jnp.tanh(0.5 * f_ref[m, rs, :].astype(F32))
        f = 1.0 - key
        val = i_ref[m, rs, :]

        b = _dot(cm_ref[...], jnp.concatenate(_split_bf16(jnp.log2(f), 2), axis=0).astype(BF16))
        eb = jnp.exp2(b)
        q_state = (qs * eb).astype(BF16)
        k_state = (key * jnp.exp2(b[CHUNK - 1:CHUNK, :] - b)).astype(BF16)
        decay = eb[CHUNK - 1:CHUNK, :]

        q_lv, k_lv = [], []
        for lv in LEVELS:
            qp, kp = [], []
            for piece in range(CHUNK // lv):
                ps = slice(piece * lv, (piece + 1) * lv)
                zero = jnp.zeros((lv, HG_WIDTH), F32)
                if piece % 2:
                    ref = b[piece * lv - 1:piece * lv, :]
                    qp.append(qs[ps] * jnp.exp2(b[ps] - ref)); kp.append(zero)
                else:
                    ref = b[(piece + 1) * lv - 1:(piece + 1) * lv, :]
                    qp.append(zero); kp.append(key[ps] * jnp.exp2(ref - b[ps]))
            q_lv.append(jnp.concatenate(qp, axis=0).astype(BF16))
            k_lv.append(jnp.concatenate(kp, axis=0).astype(BF16))

        carry = key.reshape(groups, DIAG, HG_WIDTH)
        f3 = f.reshape(groups, DIAG, HG_WIDTH)
        qs_bf16 = qs.astype(BF16)
        diag = [qs_bf16 * key.astype(BF16)]
        for d in range(1, DIAG):
            carry = f3 * pltpu.roll(carry, 1, axis=1)
            diag.append(qs_bf16 * carry.reshape(CHUNK, HG_WIDTH).astype(BF16))

        parts = [[_dot_nt(q_lv[n][:, hs], k_lv[n][:, hs]) for n in range(len(LEVELS))]
                 for hs in heads]
        r = _dot(jnp.concatenate([jnp.concatenate([dg[:, hs] for dg in diag], axis=1)
                                  for hs in heads], axis=0), red_ref[...])
        update = [_dot_tn(val[:, hs], k_state[:, hs]) for hs in heads]
        per_chunk.append((parts, r, update, q_state, val, decay))

    g_all = []
    for parts, r, _, _, _, _ in per_chunk:
        g_chunk = []
        for h in range(HG_HEADS):
            g_diag = pltpu.roll(r[h * CHUNK:(h + 1) * CHUNK], LANES - (DIAG - 1), axis=1,
                                stride=1, stride_axis=0)[:, 0:CHUNK]
            g_mat = jnp.where(same[DIAG], g_diag,
                              jnp.where(same[2 * DIAG], parts[h][2],
                                        jnp.where(same[4 * DIAG], parts[h][1], parts[h][0])))
            g_chunk.append(g_mat.astype(BF16))
        g_all.append(g_chunk)

    state = [[st_ref[m, h] for h in range(HG_HEADS)] for m in range(members)]
    for (m, rs), g_chunk, (_, _, update, q_state, val, decay) in zip(chunks, g_all, per_chunk):
        for h, hs in enumerate(heads):
            o = _dot_nt(q_state[:, hs], state[m][h].astype(BF16)) + _dot(g_chunk[h], val[:, hs])
            state[m][h] = state[m][h] * decay[:, hs] + update[h]
            y = _rms(o) * gn_ref[...]
            o_ref[m, rs, hs] = (y * _sigmoid(g_ref[m, rs, hs].astype(F32))).astype(BF16)
    for m in range(members):
        for h in range(HG_HEADS):
            st_ref[m, h] = state[m][h]


def _hgrn(lb_params, gnorm, proj3, col_blocks):
    bsz, seq = proj3.shape[0], proj3.shape[1]
    members, tokens = HG_MEMBERS, 4 * CHUNK
    cmat, rmat = _hgrn_constants()
    spec = lambda cb: pl.BlockSpec((members, tokens, HG_WIDTH), lambda b, i: (b, i, cb))
    full = lambda a: pl.BlockSpec(a.shape, lambda b, i: (0,) * a.ndim)
    est = (2 * 5 * members * tokens * HG_WIDTH * 2 + members * HG_HEADS * HG_DIM * HG_DIM * 4
           + 40 * CHUNK * HG_WIDTH * 4)
    return pl.pallas_call(
        _hgrn_kernel,
        grid=(bsz // members, seq // tokens),
        in_specs=[full(cmat), full(rmat), full(lb_params), full(gnorm),
                  spec(col_blocks[0]), spec(col_blocks[1]), spec(col_blocks[2]), spec(col_blocks[3])],
        out_specs=spec(0),
        out_shape=jax.ShapeDtypeStruct((bsz, seq, HG_WIDTH), BF16),
        scratch_shapes=[pltpu.VMEM((members, HG_HEADS, HG_DIM, HG_DIM), F32)],
        compiler_params=pltpu.CompilerParams(
            dimension_semantics=("parallel", "arbitrary"), vmem_limit_bytes=_vmem_limit(est)),
        name="hgrn2",
    )(cmat, rmat, lb_params, gnorm, proj3, proj3, proj3, proj3)


def _mix_kernel(x_ref, mod_ref, g_ref, a_ref, hg_ref, ga_ref, gh_ref, wa_ref, wh_ref, wo_ref,
                o_ref, m_ref, *, tc):
    d = x_ref.shape[1]
    for c in range(d // tc):
        cs = slice(c * tc, (c + 1) * tc)
        ya = _dot(a_ref[...], wa_ref[:, cs])
        yh = _dot(hg_ref[...], wh_ref[:, cs])
        merged = (_sigmoid(ga_ref[:, cs].astype(F32)) * ya
                  + _sigmoid(gh_ref[:, cs].astype(F32)) * yh)
        m_ref[:, cs] = merged.astype(BF16)
    y = _dot(m_ref[...], wo_ref[...])
    o_ref[...] = _gated_norm_residual(x_ref[...], y, mod_ref[2:3, :], g_ref[...])


def _mix_out(x2, mod3, gain, attn, hg, proj, wa, wh, wo, seq, ga_block, gh_block):
    n, d = x2.shape
    tm, tc = 512, 512
    per_batch = seq // tm
    const = lambda shape: pl.BlockSpec(shape, lambda i: (0, 0), pipeline_mode=pl.Buffered(1))
    est = ((wa.size + wh.size + wo.size) * 2 + 2 * 2 * tm * d * 4 + 2 * 2 * tm * d * 2
           + 2 * 2 * tm * ATT_WIDTH * 2 + tm * d * 2 + 3 * tm * d * 4)
    return pl.pallas_call(
        functools.partial(_mix_kernel, tc=tc),
        grid=(n // tm,),
        in_specs=[pl.BlockSpec((tm, d), lambda i: (i, 0)),
                  pl.BlockSpec((None, N_MOD, d), lambda i: (i // per_batch, 0, 0)),
                  pl.BlockSpec((1, d), lambda i: (0, 0)),
                  pl.BlockSpec((tm, ATT_WIDTH), lambda i: (i, 0)),
                  pl.BlockSpec((tm, HG_WIDTH), lambda i: (i, 0)),
                  pl.BlockSpec((tm, d), lambda i: (i, ga_block)),
                  pl.BlockSpec((tm, d), lambda i: (i, gh_block)),
                  const(wa.shape), const(wh.shape), const(wo.shape)],
        out_specs=pl.BlockSpec((tm, d), lambda i: (i, 0)),
        out_shape=jax.ShapeDtypeStruct((n, d), F32),
        scratch_shapes=[pltpu.VMEM((tm, d), BF16)],
        compiler_params=pltpu.CompilerParams(
            dimension_semantics=("parallel",), vmem_limit_bytes=_vmem_limit(est)),
        name="mix_out",
    )(x2, mod3, gain, attn, hg, proj, proj, wa, wh, wo)


def _ffn_up_kernel(x_ref, mod_ref, gpre_ref, wg_ref, wu_ref, wd_ref, o_ref, wd_bf16_ref,
                   h_even_ref, h_odd_ref, *, slices):
    i, j = pl.program_id(0), pl.program_id(1)

    def modulated_norm(rows):
        y = _rms(x_ref[rows, :]) * gpre_ref[...]
        return (y * (1.0 + mod_ref[4:5, :]) + mod_ref[3:4, :]).astype(BF16)

    @pl.when((i == 0) & (j == 0))
    def _():
        h_even_ref[...] = modulated_norm(slice(None))

    def step(h_ref, h_next_ref):
        rows_per = x_ref.shape[0] // slices
        first = pl.multiple_of(jnp.clip(j - 1, 0, slices - 1) * rows_per, rows_per)
        rows = pl.ds(first, rows_per)
        h_next_ref[rows, :] = modulated_norm(rows)
        wd_bf16_ref[...] = wd_ref[...].astype(BF16)

        h = h_ref[...]
        g = _dot(h, wg_ref[...])
        u = _dot(h, wu_ref[...])
        o_ref[...] = (g * _sigmoid(g) * u).astype(BF16)

    pl.when(i % 2 == 0)(lambda: step(h_even_ref, h_odd_ref))
    pl.when(i % 2 == 1)(lambda: step(h_odd_ref, h_even_ref))


def _ffn_down_kernel(a_ref, x_ref, mod_ref, gpost_ref, wd_ref, o_ref):
    y = _dot(a_ref[...], wd_ref[...])
    o_ref[...] = _gated_norm_residual(x_ref[...], y, mod_ref[5:6, :], gpost_ref[...])


def _ffn(x1, mod3, gpre, gpost, w_in, w_down, seq):
    n, d = x1.shape
    hidden = w_down.shape[0]

    tm, th = 1024, 512
    nh = hidden // th
    per_batch = seq // tm
    last = n // tm - 1
    x_tile = lambda i, j: jnp.minimum(i + jnp.where(j > 0, 1, 0), last)
    steps = (n // tm) * nh
    wd_rows = hidden // steps
    assert wd_rows * steps == hidden and wd_rows % 16 == 0
    est = (2 * tm * d * 4 + 2 * tm * d * 2 + 2 * 2 * d * th * 2 + 2 * tm * th * 2
           + 4 * tm * th * 4 + 2 * wd_rows * d * 6)
    act, w_down_bf16 = pl.pallas_call(
        functools.partial(_ffn_up_kernel, slices=8),
        grid=(n // tm, nh),
        in_specs=[pl.BlockSpec((tm, d), lambda i, j: (x_tile(i, j), 0)),
                  pl.BlockSpec((None, N_MOD, d), lambda i, j: (x_tile(i, j) // per_batch, 0, 0)),
                  pl.BlockSpec((1, d), lambda i, j: (0, 0)),
                  pl.BlockSpec((d, th), lambda i, j: (0, j)),
                  pl.BlockSpec((d, th), lambda i, j: (0, nh + j)),
                  pl.BlockSpec((wd_rows, d), lambda i, j: (i * nh + j, 0))],
        out_specs=[pl.BlockSpec((tm, th), lambda i, j: (i, j)),
                   pl.BlockSpec((wd_rows, d), lambda i, j: (i * nh + j, 0))],
        out_shape=[jax.ShapeDtypeStruct((n, hidden), BF16),
                   jax.ShapeDtypeStruct((hidden, d), BF16)],
        scratch_shapes=[pltpu.VMEM((tm, d), BF16), pltpu.VMEM((tm, d), BF16)],
        compiler_params=pltpu.CompilerParams(
            dimension_semantics=("arbitrary", "arbitrary"), vmem_limit_bytes=_vmem_limit(est)),
        name="ffn_up",
    )(x1, mod3, gpre, w_in, w_in, w_down)
    w_down = w_down_bf16

    tm = 512
    per_batch = seq // tm
    est = hidden * d * 2 + 2 * tm * hidden * 2 + 2 * 2 * tm * d * 4 + 2 * tm * d * 4
    return pl.pallas_call(
        _ffn_down_kernel,
        grid=(n // tm,),
        in_specs=[pl.BlockSpec((tm, hidden), lambda i: (i, 0)),
                  pl.BlockSpec((tm, d), lambda i: (i, 0)),
                  pl.BlockSpec((None, N_MOD, d), lambda i: (i // per_batch, 0, 0)),
                  pl.BlockSpec((1, d), lambda i: (0, 0)),
                  pl.BlockSpec((hidden, d), lambda i: (0, 0), pipeline_mode=pl.Buffered(1))],
        out_specs=pl.BlockSpec((tm, d), lambda i: (i, 0)),
        out_shape=jax.ShapeDtypeStruct((n, d), F32),
        compiler_params=pltpu.CompilerParams(
            dimension_semantics=("parallel",), vmem_limit_bytes=_vmem_limit(est)),
        name="ffn_down",
    )(act, x1, mod3, gpost, w_down)


def kernel(x, c, positions, w_ada, b_ada, g_pre_mix, g_post_mix, g_pre_ffn, g_post_ffn, w_in,
           attn_sinks, w_attn_proj, hg_lower_bounds, hg_norm, w_hgrn_proj, w_out, w_ffn_in,
           w_ffn_out):
    bsz, seq, d = x.shape
    n = bsz * seq
    assert d == D_MODEL and w_ada.shape[0] == 1 and seq % 1024 == 0 and bsz % ATT_MEMBERS == 0

    ga_block, gh_block = 0, 1
    q_block = 2 * D_MODEL // ATT_WIDTH
    hg_blocks = tuple((2 * D_MODEL + ATT_WIDTH) // HG_WIDTH + k for k in range(4))

    x2 = x.reshape(n, d)
    mod, w_in_bf16 = _adaln(c, w_ada[0], b_ada[0], w_in[0])
    mod3 = mod.reshape(bsz, N_MOD, d)

    proj, kv, w_up = _in_proj(x2, mod3, g_pre_mix, w_in_bf16, seq, ((w_ffn_in[0], (D_MODEL, LANES)),))

    inv_freq = ROPE_THETA ** (-jnp.arange(0, ROT_DIM, 2, dtype=F32) / ROT_DIM)
    freq8 = jnp.broadcast_to(inv_freq[:, None], (ROT_HALF, WINDOW))
    pos4 = positions.astype(F32).reshape(bsz, seq // WINDOW, 1, WINDOW)
    attn, wa, wh, wo = _attention(attn_sinks[0], freq8, pos4, proj.reshape(bsz, seq, -1),
                                  kv.reshape(bsz, seq, -1), q_block,
                                  (w_attn_proj[0], w_hgrn_proj[0], w_out[0]))
    attn = attn.reshape(n, ATT_WIDTH)

    hg = _hgrn(hg_lower_bounds, hg_norm, proj.reshape(bsz, seq, -1), hg_blocks).reshape(n, HG_WIDTH)

    x1 = _mix_out(x2, mod3, g_post_mix, attn, hg, proj, wa, wh, wo, seq, ga_block, gh_block)

    out = _ffn(x1, mod3, g_pre_ffn, g_post_ffn, w_up, w_ffn_out[0], seq)
    return out.reshape(bsz, seq, d)
```
